```python
import jax
import jax.numpy as jnp
from jax import lax

D_MODEL = 1024
BATCH = 4
SEQ = 4096
DEPTH = 2
DEC_BATCH = 128
DEC_SEQ = 4
PAST_LEN = 2048
PAGE_SIZE = 128

HEAD_DIM = 64
H_SB = 6
H_DSA = 5
H_MOBA = 5
N_HEADS = H_SB + H_DSA + H_MOBA
IDX_HEADS = 8
IDX_DIM = 64
DSA_TOPK = 256
MOBA_BLOCK = 256
MOBA_TOPK = 3
N_EXPERTS = 16
N_GROUPS = 4
EXPERTS_PER_GROUP = N_EXPERTS // N_GROUPS
MOE_TOPK = 2
D_EXPERT = 512
N_BRANCH = 3
ROPE_THETA = 10000.0
Q_BLOCK = 128
LN_EPS = 1e-5
QKV_W = N_HEADS * HEAD_DIM
N_IN = 3 * QKV_W + IDX_HEADS * IDX_DIM + IDX_DIM + IDX_HEADS + N_BRANCH * D_MODEL
HEAD_SLICES = ((0, H_SB), (H_SB, H_SB + H_DSA), (H_SB + H_DSA, N_HEADS))
ALPHA = (2 * DEPTH) ** 0.25
BETA = (8 * DEPTH) ** -0.25

kernel_name = 'hybrid_sb_dsa_moba_moe_step'


def layer_norm(x, g, b):
    xf = x.astype(jnp.float32)
    mu = jnp.mean(xf, -1, keepdims=True)
    var = jnp.mean(jnp.square(xf - mu), -1, keepdims=True)
    y = (xf - mu) * lax.rsqrt(var + LN_EPS) * g.astype(jnp.float32) + b.astype(jnp.float32)
    return y.astype(x.dtype)


def rope(x, pos):
    half = x.shape[-1] // 2
    inv = ROPE_THETA ** (-jnp.arange(half, dtype=jnp.float32) / half)
    ang = pos.astype(jnp.float32)[:, None] * inv[None, :]
    cos = jnp.cos(ang)[None, :, None, :]
    sin = jnp.sin(ang)[None, :, None, :]
    x1 = x[..., :half].astype(jnp.float32)
    x2 = x[..., half:].astype(jnp.float32)
    return jnp.concatenate([x1 * cos - x2 * sin, x2 * cos + x1 * sin], -1).astype(x.dtype)


def sweep_queries(fn, q_pos, *q_args):
    T = q_pos.shape[0]
    qb = Q_BLOCK if T % Q_BLOCK == 0 else T
    nb = T // qb
    if nb == 1:
        return fn(q_pos, *q_args)
    split = lambda a: jnp.moveaxis(a.reshape(a.shape[0], nb, qb, *a.shape[2:]), 1, 0)
    out = lax.map(lambda args: fn(*args), (q_pos.reshape(nb, qb),) + tuple(split(a) for a in q_args))
    out = jnp.moveaxis(out, 0, 1)
    return out.reshape(out.shape[0], T, *out.shape[3:])


def stick_breaking(q, k, v, q_pos):
    k_pos = jnp.arange(k.shape[1])
    scale = HEAD_DIM ** -0.5

    def block(qp, qb):
        z = jnp.einsum('bqhd,bkhd->bhqk', qb, k, preferred_element_type=jnp.float32) * scale
        before = (k_pos[None, :] < qp[:, None])[None, None]
        log_keep = jnp.where(before, jax.nn.log_sigmoid(-z), 0.0)
        later = lax.cumsum(log_keep, axis=3, reverse=True) - log_keep
        a = jnp.where(before, jnp.exp(jax.nn.log_sigmoid(z) + later), 0.0)
        return jnp.einsum('bhqk,bkhd->bqhd', a.astype(v.dtype), v)

    return sweep_queries(block, q_pos, q)


def dsa_attention(q, k, v, qi, wi, ki, q_pos):
    L = k.shape[1]
    n_keep = min(DSA_TOPK, L // 4)
    k_pos = jnp.arange(L)
    scale = HEAD_DIM ** -0.5
    take = jax.vmap(lambda rows, idx: rows[idx])

    def block(qp, qb, qib, wib):
        sc = jnp.einsum('bqhe,ble->bqhl', qib, ki, preferred_element_type=jnp.float32)
        score = jnp.einsum('bqh,bqhl->bql', wib.astype(jnp.float32), jax.nn.relu(sc))
        causal = (k_pos[None, :] <= qp[:, None])[None]
        score = jnp.where(causal, score, -jnp.inf)
        _, sel = lax.top_k(score, n_keep)
        valid = sel <= qp[None, :, None]
        kg = take(k, sel)
        vg = take(v, sel)
        logits = jnp.einsum('bqhd,bqnhd->bhqn', qb, kg, preferred_element_type=jnp.float32) * scale
        logits = jnp.where(valid[:, None], logits, -jnp.inf)
        p = jax.nn.softmax(logits, axis=-1)
        return jnp.einsum('bhqn,bqnhd->bqhd', p.astype(v.dtype), vg)

    return sweep_queries(block, q_pos, q, qi, wi)


def moba_attention(q, k, v, q_pos):
    B, L, H, D = k.shape
    nb = -(-L // MOBA_BLOCK)
    pad = nb * MOBA_BLOCK - L
    n_sel = min(MOBA_TOPK, nb)
    scale = HEAD_DIM ** -0.5
    to_blocks = lambda a: jnp.pad(a, ((0, 0), (0, pad), (0, 0), (0, 0))).reshape(
        B, nb, MOBA_BLOCK, H, D).transpose(0, 3, 1, 2, 4)
    k_blk = to_blocks(k)
    v_blk = to_blocks(v)
    k_mean = jnp.mean(k_blk.astype(jnp.float32), axis=3)
    take = jax.vmap(jax.vmap(lambda blks, idx: blks[idx]))

    def block(qp, qb):
        Q = qb.shape[1]
        own = qp // MOBA_BLOCK
        g = jnp.einsum('bqhd,bhnd->bhqn', qb.astype(jnp.float32), k_mean)
        past_blk = jnp.arange(nb)[None, :] < own[:, None]
        g = jnp.where(past_blk[None, None], g, -jnp.inf)
        _, sel = lax.top_k(g, n_sel)
        sel_ok = sel < own[None, None, :, None]
        kg = take(k_blk, sel)
        vg = take(v_blk, sel)
        s_sel = jnp.einsum('bqhd,bhqnsd->bhqns', qb, kg, preferred_element_type=jnp.float32) * scale
        s_sel = jnp.where(sel_ok[..., None], s_sel, -jnp.inf).reshape(B, H, Q, n_sel * MOBA_BLOCK)
        own_idx = own[:, None] * MOBA_BLOCK + jnp.arange(MOBA_BLOCK)[None, :]
        own_ok = own_idx <= qp[:, None]
        gidx = jnp.minimum(own_idx, L - 1)
        ko = k[:, gidx]
        vo = v[:, gidx]
        s_own = jnp.einsum('bqhd,bqshd->bhqs', qb, ko, preferred_element_type=jnp.float32) * scale
        s_own = jnp.where(own_ok[None, None], s_own, -jnp.inf)
        p = jax.nn.softmax(jnp.concatenate([s_sel, s_own], -1), axis=-1)
        p_sel = p[..., :n_sel * MOBA_BLOCK].reshape(B, H, Q, n_sel, MOBA_BLOCK).astype(v.dtype)
        p_own = p[..., n_sel * MOBA_BLOCK:].astype(v.dtype)
        return (jnp.einsum('bhqns,bhqnsd->bqhd', p_sel, vg)
                + jnp.einsum('bhqs,bqshd->bqhd', p_own, vo))

    return sweep_queries(block, q_pos, q)


def token_mixer(x, pos, past, w_in, w_br_a, w_br_b, w_br_c, b_gate, w_out):
    B, T, _ = x.shape
    proj = jnp.einsum('btd,de->bte', x, w_in)
    widths = (QKV_W, QKV_W, QKV_W, IDX_HEADS * IDX_DIM, IDX_DIM, IDX_HEADS)
    cuts = [sum(widths[:i + 1]) for i in range(len(widths))]
    q, k, v, qi, ki, wi, g = jnp.split(proj, cuts, axis=-1)
    heads = lambda a: a.reshape(B, T, N_HEADS, HEAD_DIM)
    q, k, v = heads(q), heads(k), heads(v)
    q = jnp.concatenate([q[:, :, :H_SB], rope(q[:, :, H_SB:], pos)], 2)
    k = jnp.concatenate([k[:, :, :H_SB], rope(k[:, :, H_SB:], pos)], 2)
    qi = rope(qi.reshape(B, T, IDX_HEADS, IDX_DIM), pos)
    ki = rope(ki[:, :, None, :], pos)[:, :, 0]
    pk_a, pv_a, pk_b, pv_b, pk_c, pv_c, p_ki = past
    (a0, a1), (b0, b1), (c0, c1) = HEAD_SLICES
    cat = lambda p, n: jnp.concatenate([p, n], 1)
    y_a = stick_breaking(q[:, :, a0:a1], cat(pk_a, k[:, :, a0:a1]), cat(pv_a, v[:, :, a0:a1]), pos)
    y_b = dsa_attention(q[:, :, b0:b1], cat(pk_b, k[:, :, b0:b1]), cat(pv_b, v[:, :, b0:b1]),
                        qi, wi, cat(p_ki, ki), pos)
    y_c = moba_attention(q[:, :, c0:c1], cat(pk_c, k[:, :, c0:c1]), cat(pv_c, v[:, :, c0:c1]), pos)
    gates = jax.nn.sigmoid((g + b_gate).astype(jnp.float32)).astype(x.dtype)
    gates = gates.reshape(B, T, N_BRANCH, D_MODEL)
    flat = lambda y: y.reshape(B, T, -1)
    merged = (gates[:, :, 0] * jnp.einsum('btf,fd->btd', flat(y_a), w_br_a)
              + gates[:, :, 1] * jnp.einsum('btf,fd->btd', flat(y_b), w_br_b)
              + gates[:, :, 2] * jnp.einsum('btf,fd->btd', flat(y_c), w_br_c))
    out = jnp.einsum('btd,de->bte', merged, w_out)
    return out, k, v, ki


def moe(x, w_router, router_bias, w_gate, w_up, w_down):
    B, T, D = x.shape
    xt = x.reshape(B * T, D)
    s = jax.nn.sigmoid(jnp.einsum('nd,de->ne', xt, w_router, preferred_element_type=jnp.float32))
    sb = s + router_bias.astype(jnp.float32)
    grp_score = lax.top_k(sb.reshape(-1, N_GROUPS, EXPERTS_PER_GROUP), MOE_TOPK)[0].sum(-1)
    best = jnp.argmax(grp_score, -1)
    in_grp = (jnp.arange(N_EXPERTS) // EXPERTS_PER_GROUP)[None, :] == best[:, None]
    _, sel = lax.top_k(jnp.where(in_grp, sb, -jnp.inf), MOE_TOPK)
    w_sel = jnp.take_along_axis(s, sel, -1)
    w_sel = w_sel / jnp.sum(w_sel, -1, keepdims=True)
    combine = jnp.sum(jax.nn.one_hot(sel, N_EXPERTS, dtype=jnp.float32) * w_sel[..., None], 1)
    h = jax.nn.silu(jnp.einsum('nd,edf->nef', xt, w_gate)) * jnp.einsum('nd,edf->nef', xt, w_up)
    h = h * combine[:, :, None].astype(h.dtype)
    return jnp.einsum('nef,efd->nd', h, w_down).reshape(B, T, D)


def gather_past(cache_k, cache_v, cache_idx_k, page_table, l):
    n_seq, n_pages = page_table.shape
    rows = lambda a: a.reshape(n_seq, n_pages * PAGE_SIZE, *a.shape[3:])
    past = []
    for lo, hi in HEAD_SLICES:
        past.append(rows(cache_k[l, page_table, :, lo:hi]))
        past.append(rows(cache_v[l, page_table, :, lo:hi]))
    past.append(rows(cache_idx_k[l, page_table]))
    return tuple(past)


def setup_inputs(seed: int = 0) -> dict:
    key = jax.random.key(seed)
    ks = jax.random.split(key, 24)
    f32 = jnp.float32
    nrm = lambda k, shape, s: jax.random.normal(k, shape, f32) * s
    n_pages = PAST_LEN // PAGE_SIZE
    n_pool = (DEC_BATCH * n_pages * 5) // 4
    page_table = jax.random.permutation(ks[0], n_pool)[:DEC_BATCH * n_pages]
    page_table = page_table.reshape(DEC_BATCH, n_pages).astype(jnp.int32)
    wa, wb, wc = H_SB * HEAD_DIM, H_DSA * HEAD_DIM, H_MOBA * HEAD_DIM
    return {
        'x_prompt': nrm(ks[1], (BATCH, SEQ, D_MODEL), 1.0),
        'x_sample': nrm(ks[2], (DEC_BATCH, DEC_SEQ, D_MODEL), 1.0),
        'cache_k': nrm(ks[3], (DEPTH, n_pool, PAGE_SIZE, N_HEADS, HEAD_DIM), 1.0),
        'cache_v': nrm(ks[4], (DEPTH, n_pool, PAGE_SIZE, N_HEADS, HEAD_DIM), 1.0),
        'cache_idx_k': nrm(ks[5], (DEPTH, n_pool, PAGE_SIZE, IDX_DIM), 1.0),
        'page_table': page_table,
        'w_in': nrm(ks[6], (DEPTH, D_MODEL, N_IN), D_MODEL ** -0.5),
        'w_br_a': nrm(ks[7], (DEPTH, wa, D_MODEL), wa ** -0.5),
        'w_br_b': nrm(ks[8], (DEPTH, wb, D_MODEL), wb ** -0.5),
        'w_br_c': nrm(ks[9], (DEPTH, wc, D_MODEL), wc ** -0.5),
        'b_gate': nrm(ks[10], (DEPTH, N_BRANCH * D_MODEL), 0.1),
        'w_out': nrm(ks[11], (DEPTH, D_MODEL, D_MODEL), BETA * D_MODEL ** -0.5),
        'ln1_g': 1.0 + nrm(ks[12], (DEPTH, D_MODEL), 0.02),
        'ln1_b': nrm(ks[13], (DEPTH, D_MODEL), 0.02),
        'ln2_g': 1.0 + nrm(ks[14], (DEPTH, D_MODEL), 0.02),
        'ln2_b': nrm(ks[15], (DEPTH, D_MODEL), 0.02),
        'w_router': nrm(ks[16], (D_MODEL, N_EXPERTS), D_MODEL ** -0.5),
        'router_bias': nrm(ks[17], (N_EXPERTS,), 0.01),
        'w_exp_gate': nrm(ks[18], (DEPTH, N_EXPERTS, D_MODEL, D_EXPERT), D_MODEL ** -0.5),
        'w_exp_up': nrm(ks[19], (DEPTH, N_EXPERTS, D_MODEL, D_EXPERT), D_MODEL ** -0.5),
        'w_exp_down': nrm(ks[20], (DEPTH, N_EXPERTS, D_EXPERT, D_MODEL), BETA * D_EXPERT ** -0.5),
    }


def reference(x_prompt, x_sample, cache_k, cache_v, cache_idx_k, page_table,
              w_in, w_br_a, w_br_b, w_br_c, b_gate, w_out, ln1_g, ln1_b, ln2_g, ln2_b,
              w_router, router_bias, w_exp_gate, w_exp_up, w_exp_down):
    def layer(l, x, pos, past):
        mix, k, v, ki = token_mixer(x, pos, past, w_in[l], w_br_a[l], w_br_b[l], w_br_c[l],
                                    b_gate[l], w_out[l])
        x = layer_norm(ALPHA * x + mix, ln1_g[l], ln1_b[l])
        ffn = moe(x, w_router, router_bias, w_exp_gate[l], w_exp_up[l], w_exp_down[l])
        x = layer_norm(ALPHA * x + ffn, ln2_g[l], ln2_b[l])
        return x, k, v, ki

    n_b, seq, _ = x_prompt.shape
    pos_p = jnp.arange(seq, dtype=jnp.int32)
    empty_past = tuple(jnp.zeros((n_b, 0, hi - lo, HEAD_DIM), x_prompt.dtype)
                       for lo, hi in HEAD_SLICES for _ in range(2))
    empty_past = empty_past + (jnp.zeros((n_b, 0, IDX_DIM), x_prompt.dtype),)
    y_p = x_prompt
    rows_p = []
    for l in range(DEPTH):
        y_p, k, v, ki = layer(l, y_p, pos_p, empty_past)
        rows_p.append((k, v, ki))

    n_pages = page_table.shape[1]
    past_len = n_pages * PAGE_SIZE
    pos_s = past_len + jnp.arange(x_sample.shape[1], dtype=jnp.int32)
    y_s = x_sample
    rows_s = []
    for l in range(DEPTH):
        past = gather_past(cache_k, cache_v, cache_idx_k, page_table, l)
        y_s, k, v, ki = layer(l, y_s, pos_s, past)
        rows_s.append((k, v, ki))

    stack = lambda rows, i: jnp.stack([r[i] for r in rows], 0)
    return (y_p, y_s, stack(rows_p, 0), stack(rows_p, 1), stack(rows_p, 2),
            stack(rows_s, 0), stack(rows_s, 1), stack(rows_s, 2))
```

```python
import functools

import jax
import jax.numpy as jnp
from jax import lax
from jax.experimental import pallas as pl
from jax.experimental.pallas import tpu as pltpu

F32 = jnp.float32
BF16 = jnp.bfloat16

HEAD_DIM = 64
H_SB, H_DSA, H_MOBA = 6, 5, 5
N_HEADS = H_SB + H_DSA + H_MOBA
IDX_HEADS = 8
IDX_DIM = 64
DSA_TOPK = 256
MOBA_BLOCK = 256
MOBA_TOPK = 3
N_EXPERTS = 16
EXPERTS_PER_GROUP = 4
N_GROUPS = N_EXPERTS // EXPERTS_PER_GROUP
ROPE_THETA = 10000.0
LN_EPS = 1e-5
DEPTH = 2
ALPHA = (2 * DEPTH) ** 0.25
PAGE_SIZE = 128

LANES = 128
NEG = -1e30
INT_MIN = -2 ** 31
VMEM_LIMIT = 56 * 1024 * 1024

_NT = (((1,), (1,)), ((), ()))


def _nt(a, b):
    return lax.dot_general(a, b, _NT, preferred_element_type=F32)


def _dot(a, b):
    return jnp.dot(a, b, preferred_element_type=F32)


def _split_bf16(x):
    hi = x.astype(BF16)
    lo = (x - hi.astype(F32)).astype(BF16)
    return hi, lo


def _softplus(z):
    return jnp.maximum(z, 0.0) + jnp.log(1.0 + jnp.exp(-jnp.abs(z)))


def _params(sem):
    return pltpu.CompilerParams(dimension_semantics=sem, vmem_limit_bytes=VMEM_LIMIT)


def _proj_kernel(x_ref, w_ref, cos_ref, sin_ref, flag_ref, o_ref, *, rope):
    acc = _dot(x_ref[...], w_ref[...])
    if not rope:
        o_ref[...] = acc.astype(o_ref.dtype)
        return
    cos = cos_ref[...]
    sin = sin_ref[...]
    lane = lax.broadcasted_iota(jnp.int32, (1, LANES), 1)
    first_half = (lane % HEAD_DIM) < (HEAD_DIM // 2)
    for c in range(acc.shape[1] // LANES):
        sl = slice(c * LANES, (c + 1) * LANES)
        a = acc[:, sl]
        sw = jnp.where(first_half, pltpu.roll(a, LANES - HEAD_DIM // 2, 1),
                       pltpu.roll(a, HEAD_DIM // 2, 1))
        r = a * cos + sw * sin
        o_ref[:, sl] = jnp.where(flag_ref[:, sl] > 0.0, r, a).astype(o_ref.dtype)


def _proj(x, w, cos, sin, flag, *, rope, out_dtype, tm, tn):
    n, d = x.shape
    width = w.shape[1]
    return pl.pallas_call(
        functools.partial(_proj_kernel, rope=rope),
        grid=(width // tn, n // tm),
        in_specs=[
            pl.BlockSpec((tm, d), lambda j, i: (i, 0)),
            pl.BlockSpec((d, tn), lambda j, i: (0, j)),
            pl.BlockSpec((tm, LANES), lambda j, i: (i, 0)),
            pl.BlockSpec((tm, LANES), lambda j, i: (i, 0)),
            pl.BlockSpec((1, tn), lambda j, i: (0, j)),
        ],
        out_specs=pl.BlockSpec((tm, tn), lambda j, i: (i, j)),
        out_shape=jax.ShapeDtypeStruct((n, width), out_dtype),
        compiler_params=_params(("parallel", "parallel")),
        name="in_proj_rope" if rope else "in_proj_gates",
    )(x, w, cos, sin, flag)


def _sb_kernel(q_ref, k_ref, v_ref, tri_ref, o_ref, *, tq):
    i = pl.program_id(2)
    q = q_ref[0, 0]
    tri = tri_ref[...]
    row = lax.broadcasted_iota(jnp.int32, (tq, tq), 0)
    col = lax.broadcasted_iota(jnp.int32, (tq, tq), 1)
    before = col < row

    def tile(j, c, acc, diag):
        start = pl.multiple_of(j * tq, tq)
        kj = k_ref[0, 0, pl.ds(start, tq), :]
        vj = v_ref[0, 0, pl.ds(start, tq), :]
        z = _nt(q, kj)
        sp = _softplus(z)
        lk = -sp
        if diag:
            lk = jnp.where(before, lk, 0.0)
        hi, lo = _split_bf16(lk)
        later = _dot(hi, tri) + _dot(lo, tri)
        a = jnp.exp(z - sp + later + c)
        if diag:
            a = jnp.where(before, a, 0.0)
        acc = acc + _dot(a.astype(BF16), vj)
        c = c + jnp.sum(lk, axis=1, keepdims=True)
        return c, acc

    c, acc = tile(i, jnp.zeros((tq, 1), F32), jnp.zeros((tq, HEAD_DIM), F32), True)
    c, acc = lax.fori_loop(0, i, lambda t, ca: tile(i - 1 - t, ca[0], ca[1], False), (c, acc))
    o_ref[0, 0] = acc.astype(o_ref.dtype)


def _tri(n):
    r = lax.broadcasted_iota(jnp.int32, (n, n), 0)
    c = lax.broadcasted_iota(jnp.int32, (n, n), 1)
    return (r > c).astype(BF16)


def _sb_attention(q, k, v, *, tq):
    b, h, t, d = q.shape
    return pl.pallas_call(
        functools.partial(_sb_kernel, tq=tq),
        grid=(b, h, t // tq),
        in_specs=[
            pl.BlockSpec((1, 1, tq, d), lambda b_, h_, i: (b_, h_, i, 0)),
            pl.BlockSpec((1, 1, t, d), lambda b_, h_, i: (b_, h_, 0, 0)),
            pl.BlockSpec((1, 1, t, d), lambda b_, h_, i: (b_, h_, 0, 0)),
            pl.BlockSpec((tq, tq), lambda b_, h_, i: (0, 0)),
        ],
        out_specs=pl.BlockSpec((1, 1, tq, d), lambda b_, h_, i: (b_, h_, i, 0)),
        out_shape=jax.ShapeDtypeStruct((b, h, t, d), BF16),
        compiler_params=_params(("parallel", "parallel", "arbitrary")),
        name="sb_attention",
    )(q, k, v, _tri(tq))


def _sort_key(score):
    bits = lax.bitcast_convert_type(score + 0.0, jnp.int32)
    return bits ^ ((bits >> 31) & jnp.int32(0x7FFFFFFF))


def _kth_largest(count_ge, rows, k):
    def body(it, ans):
        inc = jnp.left_shift(jnp.int32(1), 31 - it)
        cand = ans + inc
        return jnp.where(count_ge(cand) >= k, cand, ans)
    return lax.fori_loop(0, 32, body, jnp.full((rows, 1), INT_MIN, jnp.int32))


def _tie_cut(count_eq_below, need, rows, n_bits):
    def body(it, a):
        cand = a + jnp.left_shift(jnp.int32(1), n_bits - 1 - it)
        return jnp.where(count_eq_below(cand) < need, cand, a)
    return lax.fori_loop(0, n_bits, body, jnp.zeros((rows, 1), jnp.int32))


def _dsa_kernel(qi_ref, wi_ref, ki_ref, q_ref, k_ref, v_ref, o_ref, key_scr,
                *, tq, tk, n_keep, n_bits):
    i = pl.program_id(1)
    n_slab = (i * tq + tq + tk - 1) // tk
    qpos = i * tq + lax.broadcasted_iota(jnp.int32, (tq, 1), 0)
    kiota = lax.broadcasted_iota(jnp.int32, (1, tk), 1)
    wi = wi_ref[0]

    def fold(x):
        acc = x[:, :LANES]
        for c in range(1, tk // LANES):
            acc = acc + x[:, c * LANES:(c + 1) * LANES]
        return acc

    def scores(j, _):
        kij = ki_ref[0, pl.ds(pl.multiple_of(j * tk, tk), tk), :]
        score = jnp.zeros((tq, tk), F32)
        for h in range(IDX_HEADS):
            score = score + wi[:, h:h + 1] * jnp.maximum(_nt(qi_ref[0, h], kij), 0.0)
        causal = (j * tk + kiota) <= qpos
        key_scr[j] = _sort_key(jnp.where(causal, score, -jnp.inf))
        return 0

    lax.fori_loop(0, n_slab, scores, 0)

    def count(pred):
        def body(j, acc):
            return acc + fold(jnp.where(pred(key_scr[j], j * tk + kiota), 1.0, 0.0))
        acc = lax.fori_loop(0, n_slab, body, jnp.zeros((tq, LANES), F32))
        return jnp.sum(acc, axis=1, keepdims=True)

    thr = _kth_largest(lambda cand: count(lambda key, kp: key >= cand), tq, float(n_keep))
    need = float(n_keep) - count(lambda key, kp: key > thr)
    n_eq = count(lambda key, kp: key == thr)
    cut = lax.cond(
        jnp.max(n_eq - need) > 0.0,
        lambda: _tie_cut(lambda cand: count(lambda key, kp: (key == thr) & (kp < cand)),
                         need, tq, n_bits),
        lambda: jnp.full((tq, 1), 2 ** n_bits, jnp.int32))

    def attend(j, carry):
        key = key_scr[j]
        kp = j * tk + kiota
        sel = ((key > thr) | ((key == thr) & (kp <= cut))) & (kp <= qpos)
        start = pl.multiple_of(j * tk, tk)
        out = []
        for h in range(H_DSA):
            m, l, acc = carry[3 * h:3 * h + 3]
            kj = k_ref[0, h, pl.ds(start, tk), :]
            vj = v_ref[0, h, pl.ds(start, tk), :]
            s = jnp.where(sel, _nt(q_ref[0, h], kj), NEG)
            m_new = jnp.maximum(m, jnp.max(s, axis=1, keepdims=True))
            alpha = jnp.exp(m - m_new)
            p = jnp.where(sel, jnp.exp(s - m_new), 0.0)
            l = alpha * l + jnp.sum(p, axis=1, keepdims=True)
            acc = alpha * acc + _dot(p.astype(BF16), vj)
            out += [m_new, l, acc]
        return tuple(out)

    init = (jnp.full((tq, 1), NEG, F32), jnp.zeros((tq, 1), F32),
            jnp.zeros((tq, HEAD_DIM), F32)) * H_DSA
    res = lax.fori_loop(0, n_slab, attend, init)
    for h in range(H_DSA):
        o_ref[0, h] = (res[3 * h + 2] / res[3 * h + 1]).astype(o_ref.dtype)


def _dsa_attention(qi, wi, ki, q, k, v, *, tq, tk):
    b, h, t, d = q.shape
    n_keep = min(DSA_TOPK, t // 4)
    n_bits = max(1, (t - 1).bit_length())
    return pl.pallas_call(
        functools.partial(_dsa_kernel, tq=tq, tk=tk, n_keep=n_keep, n_bits=n_bits),
        grid=(b, t // tq),
        in_specs=[
            pl.BlockSpec((1, IDX_HEADS, tq, IDX_DIM), lambda b_, i: (b_, 0, i, 0)),
            pl.BlockSpec((1, tq, IDX_HEADS), lambda b_, i: (b_, i, 0)),
            pl.BlockSpec((1, t, IDX_DIM), lambda b_, i: (b_, 0, 0)),
            pl.BlockSpec((1, h, tq, d), lambda b_, i: (b_, 0, i, 0)),
            pl.BlockSpec((1, h, t, d), lambda b_, i: (b_, 0, 0, 0)),
            pl.BlockSpec((1, h, t, d), lambda b_, i: (b_, 0, 0, 0)),
        ],
        out_specs=pl.BlockSpec((1, h, tq, d), lambda b_, i: (b_, 0, i, 0)),
        out_shape=jax.ShapeDtypeStruct((b, h, t, d), BF16),
        scratch_shapes=[pltpu.VMEM((t // tk, tq, tk), jnp.int32)],
        compiler_params=_params(("parallel", "arbitrary")),
        name="dsa_attention",
    )(qi, wi, ki, q, k, v)


def _block_mean_kernel(k_ref, o_ref):
    o_ref[0] = jnp.mean(k_ref[0], axis=0, keepdims=True)


def _block_means(k_tm):
    b, t, w = k_tm.shape
    nb = t // MOBA_BLOCK
    out = pl.pallas_call(
        _block_mean_kernel,
        grid=(b * nb,),
        in_specs=[pl.BlockSpec((1, MOBA_BLOCK, w), lambda n: (n, 0, 0))],
        out_specs=pl.BlockSpec((1, 1, w), lambda n: (n, 0, 0)),
        out_shape=jax.ShapeDtypeStruct((b * nb, 1, w), F32),
        compiler_params=_params(("parallel",)),
        name="moba_block_means",
    )(k_tm.reshape(b * nb, MOBA_BLOCK, w))
    return out.reshape(b, nb, w)


def _top_mask(cols, n_top):
    out = []
    for n, g in enumerate(cols):
        rank = jnp.zeros_like(g)
        for m, o in enumerate(cols):
            if m == n:
                continue
            ahead = (o >= g) if m < n else (o > g)
            rank = rank + jnp.where(ahead, 1.0, 0.0)
        out.append(rank < float(n_top))
    return out


def _moba_kernel(q_ref, k_ref, v_ref, km_ref, o_ref, *, nb):
    tq = MOBA_BLOCK
    i = pl.program_id(2)
    q = q_ref[0, 0]
    km_hi, km_lo = _split_bf16(km_ref[0, 0])
    g = _nt(q, km_hi) + _nt(q, km_lo)
    n_iota = lax.broadcasted_iota(jnp.int32, (tq, nb), 1)
    fi = n_iota.astype(F32)
    valid = n_iota < i
    gm = jnp.where(valid, g, -jnp.inf)
    sel = jnp.zeros((tq, nb), jnp.bool_)
    for _ in range(min(MOBA_TOPK, nb)):
        m = jnp.max(gm, axis=1, keepdims=True)
        first = jnp.min(jnp.where(gm == m, fi, float(nb)), axis=1, keepdims=True)
        pick = fi == first
        sel = sel | pick
        gm = jnp.where(pick, -jnp.inf, gm)
    self_f = jnp.where(sel & valid, 1.0, 0.0)

    def block(n, m, l, acc, mask):
        start = pl.multiple_of(n * tq, tq)
        kj = k_ref[0, 0, pl.ds(start, tq), :]
        vj = v_ref[0, 0, pl.ds(start, tq), :]
        s = jnp.where(mask, _nt(q, kj), NEG)
        m_new = jnp.maximum(m, jnp.max(s, axis=1, keepdims=True))
        alpha = jnp.exp(m - m_new)
        p = jnp.where(mask, jnp.exp(s - m_new), 0.0)
        l = alpha * l + jnp.sum(p, axis=1, keepdims=True)
        acc = alpha * acc + _dot(p.astype(BF16), vj)
        return m_new, l, acc

    row = lax.broadcasted_iota(jnp.int32, (tq, tq), 0)
    col = lax.broadcasted_iota(jnp.int32, (tq, tq), 1)
    state = block(i, jnp.full((tq, 1), NEG, F32), jnp.zeros((tq, 1), F32),
                  jnp.zeros((tq, HEAD_DIM), F32), col <= row)

    def past(n, st):
        chosen = jnp.max(jnp.where(n_iota == n, self_f, 0.0), axis=1, keepdims=True) > 0.0
        return block(n, st[0], st[1], st[2], chosen)

    m, l, acc = lax.fori_loop(0, i, past, state)
    o_ref[0, 0] = (acc / l).astype(o_ref.dtype)


def _moba_attention(q, k, v, kmean):
    b, h, t, d = q.shape
    nb = t // MOBA_BLOCK
    return pl.pallas_call(
        functools.partial(_moba_kernel, nb=nb),
        grid=(b, h, nb),
        in_specs=[
            pl.BlockSpec((1, 1, MOBA_BLOCK, d), lambda b_, h_, i: (b_, h_, i, 0)),
            pl.BlockSpec((1, 1, t, d), lambda b_, h_, i: (b_, h_, 0, 0)),
            pl.BlockSpec((1, 1, t, d), lambda b_, h_, i: (b_, h_, 0, 0)),
            pl.BlockSpec((1, 1, nb, d), lambda b_, h_, i: (b_, h_, 0, 0)),
        ],
        out_specs=pl.BlockSpec((1, 1, MOBA_BLOCK, d), lambda b_, h_, i: (b_, h_, i, 0)),
        out_shape=jax.ShapeDtypeStruct((b, h, t, d), BF16),
        compiler_params=_params(("parallel", "parallel", "arbitrary")),
        name="moba_attention",
    )(q, k, v, kmean)


ROWS_PER_HEAD = 8


def _sample_kernel(pt_ref, qbd_ref, qi_ref, wi_ref, knew_ref, vnew_ref, kinew_ref, *rest,
                   n_pages, n_new, n_keep, n_bits):
    del pt_ref
    k_pages = rest[:n_pages]
    v_pages = rest[n_pages:2 * n_pages]
    i_pages = rest[2 * n_pages:3 * n_pages]
    o_ref = rest[3 * n_pages]
    k_scr, v_scr, ki_scr, p_scr = rest[3 * n_pages + 1:]
    past = n_pages * PAGE_SIZE
    lk_len = past + PAGE_SIZE
    r8 = ROWS_PER_HEAD

    for p in range(n_pages):
        sl = slice(p * PAGE_SIZE, (p + 1) * PAGE_SIZE)
        k_scr[sl, :] = k_pages[p][...].astype(BF16)
        v_scr[sl, :] = v_pages[p][...].astype(BF16)
        ki_scr[sl, :] = i_pages[p][...].astype(BF16)
    k_scr[past:, :] = knew_ref[0]
    v_scr[past:, :] = vnew_ref[0]
    ki_scr[past:, :] = kinew_ref[0]

    s_all = _nt(qbd_ref[0], k_scr[...])
    kpos = lax.broadcasted_iota(jnp.int32, (1, lk_len), 1)

    def qpos(rows):
        return past + lax.broadcasted_iota(jnp.int32, (rows, 1), 0) % r8

    ra = H_SB * r8
    z = s_all[:ra]
    before = kpos < qpos(ra)
    sp = _softplus(z)
    lk = jnp.where(before, -sp, 0.0)
    hi, lo = _split_bf16(lk)
    tri = _tri(LANES)
    c = jnp.zeros((ra, 1), F32)
    for blk in reversed(range(lk_len // LANES)):
        sl = slice(blk * LANES, (blk + 1) * LANES)
        later = _dot(hi[:, sl], tri) + _dot(lo[:, sl], tri) + c
        a = jnp.where(before[:, sl], jnp.exp(z[:, sl] - sp[:, sl] + later), 0.0)
        p_scr[:ra, sl] = a.astype(BF16)
        c = c + jnp.sum(lk[:, sl], axis=1, keepdims=True)

    sc = jnp.maximum(_nt(qi_ref[0], ki_scr[...]), 0.0) * wi_ref[0]
    score = jnp.sum(sc.reshape(r8, IDX_HEADS, lk_len), axis=1)
    causal8 = kpos <= qpos(r8)
    key = _sort_key(jnp.where(causal8, score, -jnp.inf))

    def count(pred):
        return jnp.sum(jnp.where(pred, 1.0, 0.0), axis=1, keepdims=True)

    thr = _kth_largest(lambda cand: count(key >= cand), r8, float(n_keep))
    need = float(n_keep) - count(key > thr)
    n_eq = count(key == thr)
    cut = lax.cond(
        jnp.max(n_eq - need) > 0.0,
        lambda: _tie_cut(lambda cand: count((key == thr) & (kpos < cand)), need, r8, n_bits),
        lambda: jnp.full((r8, 1), 2 ** n_bits, jnp.int32))
    sel8 = ((key > thr) | ((key == thr) & (kpos <= cut))) & causal8
    rb = H_DSA * r8
    sel = jnp.concatenate([jnp.where(sel8, 1.0, 0.0)] * H_DSA, axis=0) > 0.0
    s = jnp.where(sel, s_all[ra:ra + rb], NEG)
    p = jnp.where(sel, jnp.exp(s - jnp.max(s, axis=1, keepdims=True)), 0.0)
    l_b = jnp.sum(p, axis=1, keepdims=True)
    p_scr[ra:ra + rb, :] = p.astype(BF16)

    rc = H_MOBA * r8
    sm = s_all[ra + rb:]
    n_blk = past // MOBA_BLOCK
    gate = [jnp.sum(sm[:, n * MOBA_BLOCK:(n + 1) * MOBA_BLOCK], axis=1, keepdims=True)
            for n in range(n_blk)]
    chosen = _top_mask(gate, min(MOBA_TOPK, n_blk + 1))
    own_ok = kpos[:, past:] <= qpos(rc)
    pieces = [jnp.where(chosen[n], sm[:, n * MOBA_BLOCK:(n + 1) * MOBA_BLOCK], NEG)
              for n in range(n_blk)]
    pieces.append(jnp.where(own_ok, sm[:, past:], NEG))
    m_c = pieces[0].max(axis=1, keepdims=True)
    for pc in pieces[1:]:
        m_c = jnp.maximum(m_c, pc.max(axis=1, keepdims=True))
    l_c = jnp.zeros((rc, 1), F32)
    off = 0
    for pc in pieces:
        e = jnp.where(pc > 0.5 * NEG, jnp.exp(pc - m_c), 0.0)
        l_c = l_c + jnp.sum(e, axis=1, keepdims=True)
        p_scr[ra + rb:, off:off + pc.shape[1]] = e.astype(BF16)
        off += pc.shape[1]

    out = _dot(p_scr[...], v_scr[...])
    inv = jnp.concatenate([jnp.ones((ra, 1), F32), 1.0 / l_b, 1.0 / l_c], axis=0)
    out = out * inv
    col_head = lax.broadcasted_iota(jnp.int32, (r8, N_HEADS * HEAD_DIM), 1) // HEAD_DIM
    y = jnp.zeros((r8, N_HEADS * HEAD_DIM), F32)
    for h in range(N_HEADS):
        y = y + jnp.where(col_head == h, out[h * r8:(h + 1) * r8], 0.0)
    o_ref[0] = y


def _sample_attention(layer, page_table, qbd, qi, wi, knew, vnew, kinew,
                      cache_k, cache_v, cache_ik, *, n_new):
    n_seq, n_pages = page_table.shape
    width = N_HEADS * HEAD_DIM
    past = n_pages * PAGE_SIZE
    lk_len = past + PAGE_SIZE
    n_keep = min(DSA_TOPK, (past + n_new) // 4)
    n_bits = max(1, (lk_len - 1).bit_length())
    rows = N_HEADS * ROWS_PER_HEAD

    def page_spec(j, w):
        return pl.BlockSpec((None, None, PAGE_SIZE, w), lambda b, pt: (layer, pt[b, j], 0, 0))

    def seq_spec(*dims):
        return pl.BlockSpec((1,) + dims, lambda b, pt: (b,) + (0,) * len(dims))

    in_specs = [seq_spec(rows, width), seq_spec(ROWS_PER_HEAD * IDX_HEADS, IDX_DIM),
                seq_spec(ROWS_PER_HEAD * IDX_HEADS, 1), seq_spec(PAGE_SIZE, width),
                seq_spec(PAGE_SIZE, width), seq_spec(PAGE_SIZE, IDX_DIM)]
    in_specs += [page_spec(j, width) for j in range(n_pages)]
    in_specs += [page_spec(j, width) for j in range(n_pages)]
    in_specs += [page_spec(j, IDX_DIM) for j in range(n_pages)]
    grid_spec = pltpu.PrefetchScalarGridSpec(
        num_scalar_prefetch=1,
        grid=(n_seq,),
        in_specs=in_specs,
        out_specs=pl.BlockSpec((1, ROWS_PER_HEAD, width), lambda b, pt: (b, 0, 0)),
        scratch_shapes=[pltpu.VMEM((lk_len, width), BF16), pltpu.VMEM((lk_len, width), BF16),
                        pltpu.VMEM((lk_len, IDX_DIM), BF16), pltpu.VMEM((rows, lk_len), BF16)],
    )
    out = pl.pallas_call(
        functools.partial(_sample_kernel, n_pages=n_pages, n_new=n_new, n_keep=n_keep,
                          n_bits=n_bits),
        grid_spec=grid_spec,
        out_shape=jax.ShapeDtypeStruct((n_seq, ROWS_PER_HEAD, width), F32),
        compiler_params=_params(("arbitrary",)),
        name="sample_attention",
    )(page_table, qbd, qi, wi, knew, vnew, kinew,
      *([cache_k] * n_pages), *([cache_v] * n_pages), *([cache_ik] * n_pages))
    return out[:, :n_new]


def _layer_norm(h, g, b):
    mu = jnp.mean(h, axis=-1, keepdims=True)
    d = h - mu
    var = jnp.mean(d * d, axis=-1, keepdims=True)
    return d * lax.rsqrt(var + LN_EPS) * g + b


def _merge_kernel(ya_ref, yb_ref, yc_ref, g_ref, bg_ref, wa_ref, wb_ref, wc_ref, wo_ref,
                  x_ref, lg_ref, lb_ref, o_ref, ob_ref):
    dm = x_ref.shape[1]
    merged = jnp.zeros(x_ref.shape, F32)
    for n, (y_ref, w_ref) in enumerate(((ya_ref, wa_ref), (yb_ref, wb_ref), (yc_ref, wc_ref))):
        sl = slice(n * dm, (n + 1) * dm)
        gate = jax.nn.sigmoid(g_ref[:, sl].astype(F32) + bg_ref[:, sl])
        merged = merged + gate * _dot(y_ref[...], w_ref[...])
    mix = _dot(merged.astype(BF16), wo_ref[...])
    y = _layer_norm(ALPHA * x_ref[...] + mix, lg_ref[...], lb_ref[...])
    o_ref[...] = y
    ob_ref[...] = y.astype(BF16)


def _merge(ya, yb, yc, g, bg, wa, wb, wc, wo, x, lg, lb, *, tm):
    n, dm = x.shape
    row = lambda w: pl.BlockSpec((tm, w), lambda i: (i, 0))
    full = lambda a: pl.BlockSpec(a.shape, lambda i: (0, 0))
    return pl.pallas_call(
        _merge_kernel,
        grid=(n // tm,),
        in_specs=[row(ya.shape[1]), row(yb.shape[1]), row(yc.shape[1]), row(g.shape[1]),
                  full(bg), full(wa), full(wb), full(wc), full(wo), row(dm), full(lg), full(lb)],
        out_specs=[row(dm), row(dm)],
        out_shape=[jax.ShapeDtypeStruct((n, dm), F32), jax.ShapeDtypeStruct((n, dm), BF16)],
        compiler_params=_params(("parallel",)),
        name="merge_out_ln",
    )(ya, yb, yc, g, bg, wa, wb, wc, wo, x, lg, lb)


def _route(x, xb, wr_ref, rb_ref, comb_ref):
    w_hi, w_lo = _split_bf16(wr_ref[...])
    x_lo = (x - xb.astype(F32)).astype(BF16)
    logits = _dot(xb, w_hi) + _dot(x_lo, w_hi) + _dot(xb, w_lo)
    s = jax.nn.sigmoid(logits)
    sb = s + rb_ref[...]
    s_col = [s[:, e:e + 1] for e in range(N_EXPERTS)]
    b_col = [sb[:, e:e + 1] for e in range(N_EXPERTS)]
    best_score = None
    for gi in range(N_GROUPS):
        a, b, c, d = b_col[gi * EXPERTS_PER_GROUP:(gi + 1) * EXPERTS_PER_GROUP]
        m1, n1 = jnp.maximum(a, b), jnp.minimum(a, b)
        m2, n2 = jnp.maximum(c, d), jnp.minimum(c, d)
        score = jnp.maximum(m1, m2) + jnp.maximum(jnp.minimum(m1, m2), jnp.maximum(n1, n2))
        if gi == 0:
            best_score, best = score, jnp.zeros_like(score)
        else:
            better = score > best_score
            best = jnp.where(better, float(gi), best)
            best_score = jnp.where(better, score, best_score)
    picked = []
    for gi in range(N_GROUPS):
        cols = b_col[gi * EXPERTS_PER_GROUP:(gi + 1) * EXPERTS_PER_GROUP]
        top = _top_mask(cols, 2)
        in_group = best == float(gi)
        for j in range(EXPERTS_PER_GROUP):
            picked.append(jnp.where(in_group & top[j], s_col[gi * EXPERTS_PER_GROUP + j], 0.0))
    total = picked[0]
    for c in picked[1:]:
        total = total + c
    for e in range(N_EXPERTS):
        comb_ref[e] = picked[e] / total


def _moe_kernel(x_ref, xb_ref, wr_ref, rb_ref, wg_ref, wu_ref, wd_ref, lg_ref, lb_ref,
                o_ref, ob_ref, comb_ref, acc_ref):
    e = pl.program_id(1)

    @pl.when(e == 0)
    def _():
        _route(x_ref[...], xb_ref[...], wr_ref, rb_ref, comb_ref)
        acc_ref[...] = jnp.zeros_like(acc_ref)

    xb = xb_ref[...]
    gate = _dot(xb, wg_ref[0])
    h = gate * jax.nn.sigmoid(gate) * _dot(xb, wu_ref[0]) * comb_ref[e]
    acc_ref[...] += _dot(h.astype(BF16), wd_ref[0])

    @pl.when(e == pl.num_programs(1) - 1)
    def _():
        y = _layer_norm(ALPHA * x_ref[...] + acc_ref[...], lg_ref[...], lb_ref[...])
        o_ref[...] = y
        ob_ref[...] = y.astype(BF16)


def _moe(x, xb, wr, rb, wg, wu, wd, lg, lb, *, tm):
    n, dm = x.shape
    n_e, _, de = wg.shape
    row = pl.BlockSpec((tm, dm), lambda i, e: (i, 0))
    full = lambda a: pl.BlockSpec(a.shape, lambda i, e: (0, 0))
    return pl.pallas_call(
        _moe_kernel,
        grid=(n // tm, n_e),
        in_specs=[row, row, full(wr), full(rb),
                  pl.BlockSpec((1, dm, de), lambda i, e: (e, 0, 0)),
                  pl.BlockSpec((1, dm, de), lambda i, e: (e, 0, 0)),
                  pl.BlockSpec((1, de, dm), lambda i, e: (e, 0, 0)),
                  full(lg), full(lb)],
        out_specs=[row, row],
        out_shape=[jax.ShapeDtypeStruct((n, dm), F32), jax.ShapeDtypeStruct((n, dm), BF16)],
        scratch_shapes=[pltpu.VMEM((n_e, tm, 1), F32), pltpu.VMEM((tm, dm), F32)],
        compiler_params=_params(("parallel", "arbitrary")),
        name="moe_ln",
    )(x, xb, wr, rb, wg, wu, wd, lg, lb)


QKV_W = N_HEADS * HEAD_DIM
QI_W = IDX_HEADS * IDX_DIM
ROPE_W = 3 * QKV_W + QI_W + 2 * LANES
OFF_QI = 3 * QKV_W
OFF_KI = OFF_QI + QI_W
OFF_WI = OFF_KI + LANES
TILE_TOKENS = 512
TILE_COLS = 768


def _pad_cols(a, width):
    return jnp.pad(a, ((0, 0), (0, width - a.shape[1])))


def _layer_weights(w_in_l):
    dm = w_in_l.shape[0]
    cuts = [QKV_W, 2 * QKV_W, 3 * QKV_W, 3 * QKV_W + QI_W, 3 * QKV_W + QI_W + IDX_DIM,
            3 * QKV_W + QI_W + IDX_DIM + IDX_HEADS]
    wq, wk, wv, wqi, wki, wwi, wg = jnp.split(w_in_l, cuts, axis=1)
    rope_w = _pad_cols(jnp.concatenate(
        [wq, wk, wv, wqi, _pad_cols(wki, LANES), _pad_cols(wwi, LANES)], axis=1),
        -(-ROPE_W // TILE_COLS) * TILE_COLS)
    col = jnp.arange(rope_w.shape[1])
    head = (col % QKV_W) // HEAD_DIM
    flag = ((col < 2 * QKV_W) & (head >= H_SB)) | ((col >= OFF_QI) & (col < OFF_KI + IDX_DIM))
    del dm
    return rope_w.astype(BF16), flag.astype(F32)[None, :], wg.astype(BF16)


def _rope_tables(pos):
    half = HEAD_DIM // 2
    inv = ROPE_THETA ** (-jnp.arange(half, dtype=F32) / half)
    ang = pos.astype(F32)[:, None] * inv[None, :]
    cos, sin = jnp.cos(ang), jnp.sin(ang)
    reps = LANES // HEAD_DIM
    return (jnp.concatenate([cos, cos] * reps, axis=1), jnp.concatenate([-sin, sin] * reps, axis=1))


def _heads_major(a, n_heads, dtype, scale=None):
    b, t, _ = a.shape
    a = a.reshape(b, t, n_heads, HEAD_DIM)
    if scale is not None:
        a = a * scale
    return a.transpose(0, 2, 1, 3).astype(dtype)


def kernel(x_prompt, x_sample, cache_k, cache_v, cache_idx_k, page_table, w_in, w_br_a, w_br_b,
           w_br_c, b_gate, w_out, ln1_g, ln1_b, ln2_g, ln2_b, w_router, router_bias,
           w_exp_gate, w_exp_up, w_exp_down):
    n_b, seq, dm = x_prompt.shape
    n_seq, n_new, _ = x_sample.shape
    depth = w_in.shape[0]
    n_p = n_b * seq
    n_s = n_seq * n_new
    n_pages = page_table.shape[1]
    past = n_pages * PAGE_SIZE
    scale = HEAD_DIM ** -0.5
    a0, b0, c0 = 0, H_SB, H_SB + H_DSA

    pos = jnp.concatenate([jnp.tile(jnp.arange(seq, dtype=jnp.int32), n_b),
                           jnp.tile(past + jnp.arange(n_new, dtype=jnp.int32), n_seq)])
    cos, sin = _rope_tables(pos)
    x = jnp.concatenate([x_prompt.reshape(n_p, dm), x_sample.reshape(n_s, dm)], axis=0)
    xb = x.astype(BF16)

    ck = cache_k.reshape(cache_k.shape[0], cache_k.shape[1], PAGE_SIZE, QKV_W)
    cv = cache_v.reshape(cache_v.shape[0], cache_v.shape[1], PAGE_SIZE, QKV_W)
    eye_h = jnp.eye(N_HEADS, dtype=F32)

    rows_p, rows_s = [], []
    for l in range(depth):
        w_rope, flag, w_gate = _layer_weights(w_in[l])
        proj = _proj(xb, w_rope, cos, sin, flag, rope=True, out_dtype=F32,
                     tm=TILE_TOKENS, tn=TILE_COLS)
        gates = _proj(xb, w_gate, cos, sin, flag, rope=False, out_dtype=BF16,
                      tm=TILE_TOKENS, tn=TILE_COLS)

        pp = proj[:n_p].reshape(n_b, seq, -1)
        q_p, k_p, v_p = (pp[..., i * QKV_W:(i + 1) * QKV_W] for i in range(3))
        ki_p = pp[..., OFF_KI:OFF_KI + IDX_DIM]
        wi_p = pp[..., OFF_WI:OFF_WI + IDX_HEADS]
        qh = _heads_major(q_p, N_HEADS, BF16, scale)
        kh = _heads_major(k_p, N_HEADS, BF16)
        vh = _heads_major(v_p, N_HEADS, BF16)
        qih = _heads_major(pp[..., OFF_QI:OFF_KI], IDX_HEADS, BF16)
        y_a = _sb_attention(qh[:, a0:b0], kh[:, a0:b0], vh[:, a0:b0], tq=256)
        y_b = _dsa_attention(qih, wi_p, ki_p.astype(BF16), qh[:, b0:c0], kh[:, b0:c0],
                             vh[:, b0:c0], tq=128, tk=512)
        kmean = _block_means(k_p[..., c0 * HEAD_DIM:])
        kmean = kmean.reshape(n_b, -1, H_MOBA, HEAD_DIM).transpose(0, 2, 1, 3)
        y_c = _moba_attention(qh[:, c0:], kh[:, c0:], vh[:, c0:], kmean)
        tok = lambda y: y.transpose(0, 2, 1, 3).reshape(n_p, -1)
        rows_p.append((k_p.reshape(n_b, seq, N_HEADS, HEAD_DIM),
                       v_p.reshape(n_b, seq, N_HEADS, HEAD_DIM), ki_p))

        ps = proj[n_p:].reshape(n_seq, n_new, -1)
        q_s, k_s, v_s = (ps[..., i * QKV_W:(i + 1) * QKV_W] for i in range(3))
        ki_s = ps[..., OFF_KI:OFF_KI + IDX_DIM]
        wi_s = ps[..., OFF_WI:OFF_WI + IDX_HEADS]
        pad_q = lambda a: jnp.pad(a, ((0, 0), (0, ROWS_PER_HEAD - n_new)) + ((0, 0),) * (a.ndim - 2))
        pad_k = lambda a: jnp.pad(a, ((0, 0), (0, PAGE_SIZE - n_new), (0, 0))).astype(BF16)
        q8 = pad_q(q_s.reshape(n_seq, n_new, N_HEADS, HEAD_DIM) * scale).transpose(0, 2, 1, 3)
        qbd = (q8[:, :, :, None, :] * eye_h[None, :, None, :, None]).reshape(
            n_seq, N_HEADS * ROWS_PER_HEAD, QKV_W).astype(BF16)
        qi8 = pad_q(ps[..., OFF_QI:OFF_KI]).reshape(n_seq, ROWS_PER_HEAD * IDX_HEADS, IDX_DIM)
        wi8 = pad_q(wi_s).reshape(n_seq, ROWS_PER_HEAD * IDX_HEADS, 1)
        y_s = _sample_attention(l, page_table, qbd, qi8.astype(BF16), wi8, pad_k(k_s), pad_k(v_s),
                                pad_k(ki_s), ck, cv, cache_idx_k, n_new=n_new)
        y_s = y_s.reshape(n_s, QKV_W).astype(BF16)
        rows_s.append((k_s.reshape(n_seq, n_new, N_HEADS, HEAD_DIM),
                       v_s.reshape(n_seq, n_new, N_HEADS, HEAD_DIM), ki_s))

        wa_cols, wb_cols = H_SB * HEAD_DIM, (H_SB + H_DSA) * HEAD_DIM
        ya = jnp.concatenate([tok(y_a), y_s[:, :wa_cols]], axis=0)
        yb = jnp.concatenate([tok(y_b), y_s[:, wa_cols:wb_cols]], axis=0)
        yc = jnp.concatenate([tok(y_c), y_s[:, wb_cols:]], axis=0)
        x, xb = _merge(ya, yb, yc, gates, b_gate[l][None, :], w_br_a[l].astype(BF16),
                       w_br_b[l].astype(BF16), w_br_c[l].astype(BF16), w_out[l].astype(BF16),
                       x, ln1_g[l][None, :], ln1_b[l][None, :], tm=TILE_TOKENS)
        x, xb = _moe(x, xb, w_router, router_bias[None, :], w_exp_gate[l].astype(BF16),
                     w_exp_up[l].astype(BF16), w_exp_down[l].astype(BF16),
                     ln2_g[l][None, :], ln2_b[l][None, :], tm=TILE_TOKENS)

    stack = lambda rows, i: jnp.stack([r[i] for r in rows], 0)
    return (x[:n_p].reshape(n_b, seq, dm), x[n_p:].reshape(n_seq, n_new, dm),
            stack(rows_p, 0), stack(rows_p, 1), stack(rows_p, 2),
            stack(rows_s, 0), stack(rows_s, 1), stack(rows_s, 2))
```

```python
import functools

import jax
import jax.numpy as jnp
from jax import lax
from jax.experimental import pallas as pl
from jax.experimental.pallas import tpu as pltpu

F32 = jnp.float32
BF16 = jnp.bfloat16

HEAD_DIM = 64
H_SB, H_DSA, H_MOBA = 6, 5, 5
N_HEADS = H_SB + H_DSA + H_MOBA
IDX_HEADS = 8
IDX_DIM = 64
DSA_TOPK = 256
MOBA_BLOCK = 256
MOBA_TOPK = 3
N_EXPERTS = 16
EXPERTS_PER_GROUP = 4
N_GROUPS = N_EXPERTS // EXPERTS_PER_GROUP
ROPE_THETA = 10000.0
LN_EPS = 1e-5
DEPTH = 2
ALPHA = (2 * DEPTH) ** 0.25
PAGE_SIZE = 128

LANES = 128
SLAB = 256
PAIR = 2 * HEAD_DIM
NEG = -1e30
INT_MIN = -2 ** 31
VMEM_LIMIT = 56 * 1024 * 1024

QKV_W = N_HEADS * HEAD_DIM
QI_W = IDX_HEADS * IDX_DIM
T_OFF_QI = QKV_W
T_OFF_V = QKV_W + QI_W
T_ROWS = 2 * QKV_W + QI_W
N_OFF_KI = 2 * QKV_W
N_OFF_WI = N_OFF_KI + IDX_DIM
TILE_TOKENS = 512
TILE_COLS = 768
N_COLS = 3 * TILE_COLS
TILE_ROWS_T = 640

_NT = (((1,), (1,)), ((), ()))


def _nt(a, b):
    return lax.dot_general(a, b, _NT, preferred_element_type=F32)


def _dot(a, b):
    return jnp.dot(a, b, preferred_element_type=F32)


def _split_bf16(x):
    hi = x.astype(BF16)
    lo = (x - hi.astype(F32)).astype(BF16)
    return hi, lo


def _softplus(z):
    return jnp.maximum(z, 0.0) + jnp.log(1.0 + jnp.exp(-jnp.abs(z)))


def _params(sem):
    return pltpu.CompilerParams(dimension_semantics=sem, vmem_limit_bytes=VMEM_LIMIT)


def _proj_kernel(x_ref, w_ref, cos_ref, sin_ref, flag_ref, o_ref, ob_ref, *, rope):
    acc = _dot(x_ref[...], w_ref[...])
    if not rope:
        ob_ref[...] = acc.astype(ob_ref.dtype)
        return
    cos = cos_ref[...]
    sin = sin_ref[...]
    lane = lax.broadcasted_iota(jnp.int32, (1, LANES), 1)
    first_half = (lane % HEAD_DIM) < (HEAD_DIM // 2)
    for c in range(acc.shape[1] // LANES):
        sl = slice(c * LANES, (c + 1) * LANES)
        a = acc[:, sl]
        sw = jnp.where(first_half, pltpu.roll(a, LANES - HEAD_DIM // 2, 1),
                       pltpu.roll(a, HEAD_DIM // 2, 1))
        r = jnp.where(flag_ref[:, sl] > 0.0, a * cos + sw * sin, a)
        o_ref[:, sl] = r
        ob_ref[:, sl] = r.astype(BF16)


def _proj(x, w, cos, sin, flag, *, rope, tm, tn, name):
    n, d = x.shape
    width = w.shape[1]
    tile = pl.BlockSpec((tm, tn), lambda j, i: (i, j))
    if rope:
        out_specs = [tile, tile]
        out_shape = [jax.ShapeDtypeStruct((n, width), F32), jax.ShapeDtypeStruct((n, width), BF16)]
        body = functools.partial(_proj_kernel, rope=True)
    else:
        out_specs = tile
        out_shape = jax.ShapeDtypeStruct((n, width), BF16)
        flag = jnp.zeros((1, width), F32)
        body = lambda x_r, w_r, c_r, s_r, f_r, ob_r: _proj_kernel(
            x_r, w_r, c_r, s_r, f_r, None, ob_r, rope=False)
    return pl.pallas_call(
        body,
        grid=(width // tn, n // tm),
        in_specs=[
            pl.BlockSpec((tm, d), lambda j, i: (i, 0)),
            pl.BlockSpec((d, tn), lambda j, i: (0, j)),
            pl.BlockSpec((tm, LANES), lambda j, i: (i, 0)),
            pl.BlockSpec((tm, LANES), lambda j, i: (i, 0)),
            pl.BlockSpec((1, tn), lambda j, i: (0, j)),
        ],
        out_specs=out_specs,
        out_shape=out_shape,
        compiler_params=_params(("parallel", "parallel")),
        name=name,
    )(x, w, cos, sin, flag)


def _proj_t_kernel(flag_ref, w_ref, x_ref, cos_ref, sin_ref, o_ref, *, tn, tm):
    j = pl.program_id(1)
    acc = _nt(w_ref[...], x_ref[...])
    cos = cos_ref[...]
    sin = sin_ref[...]
    half = HEAD_DIM // 2
    for g in range(tn // HEAD_DIM):
        a = acc[g * HEAD_DIM:(g + 1) * HEAD_DIM]
        sw = jnp.concatenate([a[half:], a[:half]], axis=0)
        r = jnp.where(flag_ref[j * (tn // HEAD_DIM) + g] > 0, a * cos + sw * sin, a).astype(BF16)
        for s in range(tm // SLAB):
            o_ref[s, g * HEAD_DIM:(g + 1) * HEAD_DIM, :] = r[:, s * SLAB:(s + 1) * SLAB]


def _proj_t(flags, w_t, x, cos_t, sin_t, *, tm, tn):
    n, d = x.shape
    rows = w_t.shape[0]
    spt = tm // SLAB
    grid_spec = pltpu.PrefetchScalarGridSpec(
        num_scalar_prefetch=1,
        grid=(n // tm, rows // tn),
        in_specs=[
            pl.BlockSpec((tn, d), lambda i, j, f: (j, 0)),
            pl.BlockSpec((tm, d), lambda i, j, f: (i, 0)),
            pl.BlockSpec((HEAD_DIM, tm), lambda i, j, f: (0, i)),
            pl.BlockSpec((HEAD_DIM, tm), lambda i, j, f: (0, i)),
        ],
        out_specs=pl.BlockSpec((spt, tn, SLAB), lambda i, j, f: (i, j, 0)),
    )
    return pl.pallas_call(
        functools.partial(_proj_t_kernel, tn=tn, tm=tm),
        grid_spec=grid_spec,
        out_shape=jax.ShapeDtypeStruct((n // SLAB, rows, SLAB), BF16),
        compiler_params=_params(("parallel", "arbitrary")),
        name="in_proj_feature_major",
    )(flags, w_t, x, cos_t, sin_t)


def _head_masks(q_pair):
    first = lax.broadcasted_iota(jnp.int32, (PAIR, 1), 0) < HEAD_DIM
    zero = jnp.zeros_like(q_pair)
    return jnp.where(first, q_pair, zero), jnp.where(first, zero, q_pair)


def _online_update(state, scores, masks, v_t):
    m, l, acc = state
    scores = [jnp.where(mk, s, NEG) for s, mk in zip(scores, masks)]
    m_new = m
    for s in scores:
        m_new = jnp.maximum(m_new, jnp.max(s, axis=0, keepdims=True))
    alpha = jnp.exp(m - m_new)
    probs = [jnp.where(mk, jnp.exp(s - m_new), 0.0) for s, mk in zip(scores, masks)]
    l = alpha * l
    for p in probs:
        l = l + jnp.sum(p, axis=0, keepdims=True)
    p_cat = probs[0] if len(probs) == 1 else jnp.concatenate(probs, axis=0)
    acc = alpha * acc + _dot(v_t, p_cat.astype(BF16))
    return m_new, l, acc


def _sb_kernel(q_ref, k_ref, v_ref, tri_ref, o_ref):
    t = SLAB
    i = pl.program_id(2)
    qm = _head_masks(q_ref[...])
    tri = tri_ref[...]
    key_i = lax.broadcasted_iota(jnp.int32, (t, t), 0)
    qry_i = lax.broadcasted_iota(jnp.int32, (t, t), 1)
    before = key_i < qry_i

    def tiles(js, carry, diag):
        chains = [(n, hh) for n in range(len(js)) for hh in range(2)]
        ks = [k_ref[pl.ds(pl.multiple_of(j * t, t), t), :] for j in js]
        vs = [v_ref[j] for j in js]
        z = {ch: _dot(ks[ch[0]], qm[ch[1]]) for ch in chains}
        sp = {ch: _softplus(z[ch]) for ch in chains}
        lk = {ch: -sp[ch] for ch in chains}
        if diag:
            lk = {ch: jnp.where(before, lk[ch], 0.0) for ch in chains}
        parts = {ch: _split_bf16(lk[ch]) for ch in chains}
        later = {ch: _dot(tri, parts[ch][0]) + _dot(tri, parts[ch][1]) for ch in chains}
        out = []
        for hh in range(2):
            c, acc = carry[2 * hh:2 * hh + 2]
            a_all = []
            for n in range(len(js)):
                ch = (n, hh)
                a = jnp.exp(z[ch] - sp[ch] + later[ch] + c)
                if diag:
                    a = jnp.where(before, a, 0.0)
                a_all.append(a.astype(BF16))
                c = c + jnp.sum(lk[ch], axis=0, keepdims=True)
            v_cat = jnp.concatenate([v[hh * HEAD_DIM:(hh + 1) * HEAD_DIM] for v in vs], axis=1)
            acc = acc + _dot(v_cat, jnp.concatenate(a_all, axis=0))
            out += [c, acc]
        return tuple(out)

    init = (jnp.zeros((1, t), F32), jnp.zeros((HEAD_DIM, t), F32)) * 2
    carry = tiles([i], init, True)
    carry = lax.fori_loop(
        0, i // 2, lambda n, ca: tiles([i - 1 - 2 * n, i - 2 - 2 * n], ca, False), carry)
    carry = lax.cond(i % 2 == 1, lambda ca: tiles([0], ca, False), lambda ca: ca, carry)
    o_ref[...] = jnp.concatenate([carry[1], carry[3]], axis=0).astype(o_ref.dtype)


def _tri_upper(n):
    r = lax.broadcasted_iota(jnp.int32, (n, n), 0)
    c = lax.broadcasted_iota(jnp.int32, (n, n), 1)
    return (c > r).astype(BF16)


def _sb_attention(proj_t, k_tok, *, n_b, seq):
    nq = seq // SLAB
    n_pairs = H_SB // 2
    v_blk0 = T_OFF_V // PAIR
    return pl.pallas_call(
        _sb_kernel,
        grid=(n_b, n_pairs, nq),
        in_specs=[
            pl.BlockSpec((None, PAIR, SLAB), lambda b, p, i: (b * nq + i, p, 0)),
            pl.BlockSpec((seq, PAIR), lambda b, p, i: (b, p)),
            pl.BlockSpec((nq, PAIR, SLAB), lambda b, p, i: (b, v_blk0 + p, 0)),
            pl.BlockSpec((SLAB, SLAB), lambda b, p, i: (0, 0)),
        ],
        out_specs=pl.BlockSpec((None, PAIR, SLAB), lambda b, p, i: (b * nq + i, p, 0)),
        out_shape=jax.ShapeDtypeStruct((n_b * nq, n_pairs * PAIR, SLAB), BF16),
        compiler_params=_params(("parallel", "parallel", "arbitrary")),
        name="sb_attention",
    )(proj_t, k_tok, proj_t, _tri_upper(SLAB))


def _sort_key(score):
    bits = lax.bitcast_convert_type(score + 0.0, jnp.int32)
    return bits ^ ((bits >> 31) & jnp.int32(0x7FFFFFFF))


def _kth_largest(count_ge, shape, k):
    def body(it, ans):
        cand = ans + jnp.left_shift(jnp.int32(1), 31 - it)
        return jnp.where(count_ge(cand) >= k, cand, ans)
    return lax.fori_loop(0, 32, body, jnp.full(shape, INT_MIN, jnp.int32))


def _tie_cut(count_eq_below, need, shape, n_bits):
    def body(it, a):
        cand = a + jnp.left_shift(jnp.int32(1), n_bits - 1 - it)
        return jnp.where(count_eq_below(cand) < need, cand, a)
    return lax.fori_loop(0, n_bits, body, jnp.zeros(shape, jnp.int32))


DSA_BLOCK_HEADS = 6


def _dsa_kernel(qi_ref, kiwi_ref, ki_ref, q_ref, k_ref, v_ref, o_ref,
                key_scr, m_scr, l_scr, acc_scr, *, n_keep, n_bits):
    t = SLAB
    i = pl.program_id(1)
    n_slab = i + 1
    qpos = i * t + lax.broadcasted_iota(jnp.int32, (1, t), 1)
    kiota = lax.broadcasted_iota(jnp.int32, (t, 1), 0)
    wi_t = kiwi_ref[...].T

    def scores(j, _):
        kij = ki_ref[pl.ds(pl.multiple_of(j * t, t), t), :][:, :IDX_DIM]
        score = jnp.zeros((t, t), F32)
        for h in range(IDX_HEADS):
            sc = _dot(kij, qi_ref[h * IDX_DIM:(h + 1) * IDX_DIM, :])
            score = score + wi_t[IDX_DIM + h:IDX_DIM + h + 1, :] * jnp.maximum(sc, 0.0)
        causal = (j * t + kiota) <= qpos
        key_scr[j] = _sort_key(jnp.where(causal, score, -jnp.inf))
        return 0

    lax.fori_loop(0, n_slab, scores, 0)

    def count(pred):
        def body(j, acc):
            ones = jnp.where(pred(key_scr[j], j * t + kiota), 1.0, 0.0)
            return acc + jnp.sum(ones.reshape(t // 8, 8, t), axis=0)
        acc = lax.fori_loop(0, n_slab, body, jnp.zeros((8, t), F32))
        return jnp.sum(acc, axis=0, keepdims=True)

    thr = _kth_largest(lambda cand: count(lambda key, kp: key >= cand), (1, t), float(n_keep))
    need = float(n_keep) - count(lambda key, kp: key > thr)
    n_eq = count(lambda key, kp: key == thr)
    cut = lax.cond(
        jnp.max(n_eq - need) > 0.0,
        lambda: _tie_cut(lambda cand: count(lambda key, kp: (key == thr) & (kp < cand)),
                         need, (1, t), n_bits),
        lambda: jnp.full((1, t), 2 ** n_bits, jnp.int32))

    m_scr[...] = jnp.full(m_scr.shape, NEG, F32)
    l_scr[...] = jnp.zeros(l_scr.shape, F32)
    acc_scr[...] = jnp.zeros(acc_scr.shape, F32)
    qm = []
    for pr in range(DSA_BLOCK_HEADS // 2):
        qm += list(_head_masks(q_ref[pr * PAIR:(pr + 1) * PAIR, :]))

    def attend(j, _):
        key = key_scr[j]
        kp = j * t + kiota
        sel = ((key > thr) | ((key == thr) & (kp <= cut))) & (kp <= qpos)
        start = pl.multiple_of(j * t, t)
        vj = v_ref[j]
        kj = [k_ref[pl.ds(start, t), pr * PAIR:(pr + 1) * PAIR] for pr in range((H_DSA + 1) // 2)]
        z = [_dot(kj[h // 2], qm[h]) for h in range(H_DSA)]
        for h in range(H_DSA):
            state = (m_scr[h], l_scr[h], acc_scr[h])
            m, l, acc = _online_update(state, [z[h]], [sel], vj[h * HEAD_DIM:(h + 1) * HEAD_DIM])
            m_scr[h] = m
            l_scr[h] = l
            acc_scr[h] = acc
        return 0

    lax.fori_loop(0, n_slab, attend, 0)
    for h in range(H_DSA):
        o_ref[h * HEAD_DIM:(h + 1) * HEAD_DIM, :] = (acc_scr[h] / l_scr[h]).astype(o_ref.dtype)
    o_ref[H_DSA * HEAD_DIM:, :] = jnp.zeros((HEAD_DIM, t), o_ref.dtype)


def _dsa_attention(proj_t, k_tok, kiwi_f32, *, n_b, seq):
    nq = seq // SLAB
    wide = DSA_BLOCK_HEADS * HEAD_DIM
    n_keep = min(DSA_TOPK, seq // 4)
    n_bits = max(1, (seq - 1).bit_length())
    kiwi_blk = N_OFF_KI // LANES
    return pl.pallas_call(
        functools.partial(_dsa_kernel, n_keep=n_keep, n_bits=n_bits),
        grid=(n_b, nq),
        in_specs=[
            pl.BlockSpec((None, QI_W, SLAB), lambda b, i: (b * nq + i, T_OFF_QI // QI_W, 0)),
            pl.BlockSpec((SLAB, LANES), lambda b, i: (b * nq + i, kiwi_blk)),
            pl.BlockSpec((seq, LANES), lambda b, i: (b, kiwi_blk)),
            pl.BlockSpec((None, wide, SLAB), lambda b, i: (b * nq + i, 1, 0)),
            pl.BlockSpec((seq, wide), lambda b, i: (b, 1)),
            pl.BlockSpec((nq, wide, SLAB), lambda b, i: (b, T_OFF_V // wide + 1, 0)),
        ],
        out_specs=pl.BlockSpec((None, wide, SLAB), lambda b, i: (b * nq + i, 0, 0)),
        out_shape=jax.ShapeDtypeStruct((n_b * nq, wide, SLAB), BF16),
        scratch_shapes=[pltpu.VMEM((nq, SLAB, SLAB), jnp.int32),
                        pltpu.VMEM((H_DSA, 1, SLAB), F32), pltpu.VMEM((H_DSA, 1, SLAB), F32),
                        pltpu.VMEM((H_DSA, HEAD_DIM, SLAB), F32)],
        compiler_params=_params(("parallel", "arbitrary")),
        name="dsa_attention",
    )(proj_t, kiwi_f32, k_tok, proj_t, k_tok, proj_t)


def _block_mean_kernel(k_ref, o_ref):
    o_ref[0] = jnp.mean(k_ref[:, :QKV_W], axis=0, keepdims=True)


def _block_means(k_tok_f32, n_tokens):
    nb = n_tokens // MOBA_BLOCK
    return pl.pallas_call(
        _block_mean_kernel,
        grid=(nb,),
        in_specs=[pl.BlockSpec((MOBA_BLOCK, QKV_W), lambda n: (n, 0))],
        out_specs=pl.BlockSpec((1, 1, QKV_W), lambda n: (n, 0, 0)),
        out_shape=jax.ShapeDtypeStruct((nb, 1, QKV_W), F32),
        compiler_params=_params(("parallel",)),
        name="moba_block_means",
    )(k_tok_f32)


MOBA_FIRST_PAIR = (H_SB + H_DSA) // 2


def _moba_kernel(q_ref, k_ref, v_ref, km_ref, o_ref, sel_scr, *, nb):
    t = MOBA_BLOCK
    i = pl.program_id(2)
    qm = _head_masks(q_ref[...])
    km_hi, km_lo = _split_bf16(km_ref[...])
    n_iota = lax.broadcasted_iota(jnp.int32, (nb, t), 0)
    fi = n_iota.astype(F32)
    valid = n_iota < i
    for hh in range(2):
        gm = jnp.where(valid, _dot(km_hi, qm[hh]) + _dot(km_lo, qm[hh]), -jnp.inf)
        sel = jnp.zeros((nb, t), jnp.bool_)
        for _ in range(min(MOBA_TOPK, nb)):
            m = jnp.max(gm, axis=0, keepdims=True)
            first = jnp.min(jnp.where(gm == m, fi, float(nb)), axis=0, keepdims=True)
            pick = fi == first
            sel = sel | pick
            gm = jnp.where(pick, -jnp.inf, gm)
        sel_scr[hh] = jnp.where(sel & valid, 1.0, 0.0)

    def blocks(ns, carry, mask_of):
        ks = [k_ref[pl.ds(pl.multiple_of(n * t, t), t), :] for n in ns]
        vs = [v_ref[n] for n in ns]
        z = [[_dot(kj, qm[hh]) for kj in ks] for hh in range(2)]
        out = []
        for hh in range(2):
            v_cat = jnp.concatenate([v[hh * HEAD_DIM:(hh + 1) * HEAD_DIM] for v in vs], axis=1)
            out += list(_online_update(carry[3 * hh:3 * hh + 3], z[hh],
                                       [mask_of(hh, n) for n in ns], v_cat))
        return tuple(out)

    key_i = lax.broadcasted_iota(jnp.int32, (t, t), 0)
    qry_i = lax.broadcasted_iota(jnp.int32, (t, t), 1)
    chosen = lambda hh, n: sel_scr[hh, pl.ds(n, 1), :] > 0.0
    init = (jnp.full((1, t), NEG, F32), jnp.zeros((1, t), F32), jnp.zeros((HEAD_DIM, t), F32)) * 2
    carry = blocks([i], init, lambda hh, n: key_i <= qry_i)
    carry = lax.fori_loop(0, i // 2, lambda n, ca: blocks([2 * n, 2 * n + 1], ca, chosen), carry)
    carry = lax.cond(i % 2 == 1, lambda ca: blocks([i - 1], ca, chosen), lambda ca: ca, carry)
    o_ref[...] = jnp.concatenate([carry[2] / carry[1], carry[5] / carry[4]],
                                 axis=0).astype(o_ref.dtype)


def _moba_attention(proj_t, k_tok, kmean, *, n_b, seq):
    nq = seq // SLAB
    nb = seq // MOBA_BLOCK
    n_pairs = N_HEADS // 2 - MOBA_FIRST_PAIR
    p0 = MOBA_FIRST_PAIR
    v_blk0 = T_OFF_V // PAIR + p0
    return pl.pallas_call(
        functools.partial(_moba_kernel, nb=nb),
        grid=(n_b, n_pairs, nq),
        in_specs=[
            pl.BlockSpec((None, PAIR, SLAB), lambda b, p, i: (b * nq + i, p0 + p, 0)),
            pl.BlockSpec((seq, PAIR), lambda b, p, i: (b, p0 + p)),
            pl.BlockSpec((nq, PAIR, SLAB), lambda b, p, i: (b, v_blk0 + p, 0)),
            pl.BlockSpec((None, nb, PAIR), lambda b, p, i: (b, 0, p0 + p)),
        ],
        out_specs=pl.BlockSpec((None, PAIR, SLAB), lambda b, p, i: (b * nq + i, p, 0)),
        out_shape=jax.ShapeDtypeStruct((n_b * nq, n_pairs * PAIR, SLAB), BF16),
        scratch_shapes=[pltpu.VMEM((2, nb, SLAB), F32)],
        compiler_params=_params(("parallel", "parallel", "arbitrary")),
        name="moba_attention",
    )(proj_t, k_tok, proj_t, kmean)


ROWS_PER_HEAD = 8


def _top_mask(cols, n_top):
    out = []
    for n, g in enumerate(cols):
        rank = jnp.zeros_like(g)
        for m, o in enumerate(cols):
            if m == n:
                continue
            ahead = (o >= g) if m < n else (o > g)
            rank = rank + jnp.where(ahead, 1.0, 0.0)
        out.append(rank < float(n_top))
    return out


def _sample_kernel(pt_ref, qbd_ref, qi_ref, wi_ref, knew_ref, vnew_ref, kinew_ref, *rest,
                   n_pages, n_keep, n_bits):
    del pt_ref
    k_pages, v_pages = rest[:n_pages], rest[n_pages:2 * n_pages]
    i_pages = rest[2 * n_pages:3 * n_pages]
    o_ref = rest[3 * n_pages]
    k_scr, v_scr, ki_scr, s_scr, p_scr = rest[3 * n_pages + 1:]
    past = n_pages * PAGE_SIZE
    lk_len = past + PAGE_SIZE
    r8 = ROWS_PER_HEAD

    for pg in range(n_pages):
        sl = slice(pg * PAGE_SIZE, (pg + 1) * PAGE_SIZE)
        k_scr[sl, :] = k_pages[pg][...]
        v_scr[sl, :] = v_pages[pg][...]
        ki_scr[sl, :] = i_pages[pg][...]
    k_scr[past:, :] = knew_ref[0]
    v_scr[past:, :] = vnew_ref[0]
    ki_scr[past:, :] = kinew_ref[0]
    s_scr[...] = _nt(qbd_ref[0], k_scr[...])

    kpos = lax.broadcasted_iota(jnp.int32, (1, lk_len), 1)

    def qpos(rows):
        return past + lax.broadcasted_iota(jnp.int32, (rows, 1), 0) % r8

    ra = H_SB * r8
    z = s_scr[:ra, :]
    before = kpos < qpos(ra)
    sp = _softplus(z)
    lk = jnp.where(before, -sp, 0.0)
    hi, lo = _split_bf16(lk)
    tri_r = lax.broadcasted_iota(jnp.int32, (LANES, LANES), 0)
    tri_c = lax.broadcasted_iota(jnp.int32, (LANES, LANES), 1)
    tri = (tri_r > tri_c).astype(BF16)
    c = jnp.zeros((ra, 1), F32)
    for blk in reversed(range(lk_len // LANES)):
        sl = slice(blk * LANES, (blk + 1) * LANES)
        later = _dot(hi[:, sl], tri) + _dot(lo[:, sl], tri) + c
        a = jnp.where(before[:, sl], jnp.exp(z[:, sl] - sp[:, sl] + later), 0.0)
        p_scr[:ra, sl] = a.astype(BF16)
        c = c + jnp.sum(lk[:, sl], axis=1, keepdims=True)

    sc = jnp.maximum(_nt(qi_ref[0], ki_scr[...]), 0.0) * wi_ref[0]
    score = jnp.sum(sc.reshape(r8, IDX_HEADS, lk_len), axis=1)
    causal8 = kpos <= qpos(r8)
    key = _sort_key(jnp.where(causal8, score, -jnp.inf))

    def count(pred):
        return jnp.sum(jnp.where(pred, 1.0, 0.0), axis=1, keepdims=True)

    thr = _kth_largest(lambda cand: count(key >= cand), (r8, 1), float(n_keep))
    need = float(n_keep) - count(key > thr)
    n_eq = count(key == thr)
    cut = lax.cond(
        jnp.max(n_eq - need) > 0.0,
        lambda: _tie_cut(lambda cand: count((key == thr) & (kpos < cand)), need, (r8, 1), n_bits),
        lambda: jnp.full((r8, 1), 2 ** n_bits, jnp.int32))
    sel8 = ((key > thr) | ((key == thr) & (kpos <= cut))) & causal8
    rb = H_DSA * r8
    sel = jnp.concatenate([jnp.where(sel8, 1.0, 0.0)] * H_DSA, axis=0) > 0.0
    s = jnp.where(sel, s_scr[ra:ra + rb, :], NEG)
    p = jnp.where(sel, jnp.exp(s - jnp.max(s, axis=1, keepdims=True)), 0.0)
    l_b = jnp.sum(p, axis=1, keepdims=True)
    p_scr[ra:ra + rb, :] = p.astype(BF16)

    rc = H_MOBA * r8
    sm = s_scr[ra + rb:, :]
    n_blk = past // MOBA_BLOCK
    gate = [jnp.sum(sm[:, n * MOBA_BLOCK:(n + 1) * MOBA_BLOCK], axis=1, keepdims=True)
            for n in range(n_blk)]
    chosen = _top_mask(gate, min(MOBA_TOPK, n_blk + 1))
    own_ok = kpos[:, past:] <= qpos(rc)
    pieces = [jnp.where(chosen[n], sm[:, n * MOBA_BLOCK:(n + 1) * MOBA_BLOCK], NEG)
              for n in range(n_blk)]
    pieces.append(jnp.where(own_ok, sm[:, past:], NEG))
    m_c = pieces[0].max(axis=1, keepdims=True)
    for pc in pieces[1:]:
        m_c = jnp.maximum(m_c, pc.max(axis=1, keepdims=True))
    l_c = jnp.zeros((rc, 1), F32)
    off = 0
    for pc in pieces:
        e = jnp.where(pc > 0.5 * NEG, jnp.exp(pc - m_c), 0.0)
        l_c = l_c + jnp.sum(e, axis=1, keepdims=True)
        p_scr[ra + rb:, off:off + pc.shape[1]] = e.astype(BF16)
        off += pc.shape[1]

    out = _dot(p_scr[...], v_scr[...])
    out = out * jnp.concatenate([jnp.ones((ra, 1), F32), 1.0 / l_b, 1.0 / l_c], axis=0)
    col_head = lax.broadcasted_iota(jnp.int32, (r8, QKV_W), 1) // HEAD_DIM
    y = jnp.zeros((r8, QKV_W), F32)
    for h in range(N_HEADS):
        y = y + jnp.where(col_head == h, out[h * r8:(h + 1) * r8], 0.0)
    o_ref[0] = y


def _sample_attention(layer, page_table, qbd, qi8, wi8, knew, vnew, kinew,
                      cache_k, cache_v, cache_ik, *, n_new):
    n_seq, n_pages = page_table.shape
    past = n_pages * PAGE_SIZE
    lk_len = past + PAGE_SIZE
    n_keep = min(DSA_TOPK, (past + n_new) // 4)
    n_bits = max(1, (lk_len - 1).bit_length())
    rows = N_HEADS * ROWS_PER_HEAD

    def page_spec(j, w):
        return pl.BlockSpec((None, None, PAGE_SIZE, w), lambda b, pt: (layer, pt[b, j], 0, 0))

    def seq_spec(*dims):
        return pl.BlockSpec((1,) + dims, lambda b, pt: (b,) + (0,) * len(dims))

    in_specs = [seq_spec(rows, QKV_W), seq_spec(ROWS_PER_HEAD * IDX_HEADS, IDX_DIM),
                seq_spec(ROWS_PER_HEAD * IDX_HEADS, 1), seq_spec(PAGE_SIZE, QKV_W),
                seq_spec(PAGE_SIZE, QKV_W), seq_spec(PAGE_SIZE, IDX_DIM)]
    in_specs += [page_spec(j, QKV_W) for j in range(n_pages)]
    in_specs += [page_spec(j, QKV_W) for j in range(n_pages)]
    in_specs += [page_spec(j, IDX_DIM) for j in range(n_pages)]
    grid_spec = pltpu.PrefetchScalarGridSpec(
        num_scalar_prefetch=1,
        grid=(n_seq,),
        in_specs=in_specs,
        out_specs=pl.BlockSpec((1, ROWS_PER_HEAD, QKV_W), lambda b, pt: (b, 0, 0)),
        scratch_shapes=[pltpu.VMEM((lk_len, QKV_W), BF16), pltpu.VMEM((lk_len, QKV_W), BF16),
                        pltpu.VMEM((lk_len, IDX_DIM), BF16),
                        pltpu.VMEM((rows, lk_len), F32), pltpu.VMEM((rows, lk_len), BF16)],
    )
    out = pl.pallas_call(
        functools.partial(_sample_kernel, n_pages=n_pages, n_keep=n_keep, n_bits=n_bits),
        grid_spec=grid_spec,
        out_shape=jax.ShapeDtypeStruct((n_seq, ROWS_PER_HEAD, QKV_W), F32),
        compiler_params=_params(("arbitrary",)),
        name="sample_attention",
    )(page_table, qbd, qi8, wi8, knew, vnew, kinew,
      *([cache_k] * n_pages), *([cache_v] * n_pages), *([cache_ik] * n_pages))
    return out[:, :n_new]


def _layer_norm(h, g, b):
    mu = jnp.mean(h, axis=-1, keepdims=True)
    d = h - mu
    var = jnp.mean(d * d, axis=-1, keepdims=True)
    return d * lax.rsqrt(var + LN_EPS) * g + b


def _merge_kernel(ya_ref, yb_ref, yc_ref, g_ref, bg_ref, wa_ref, wb_ref, wc_ref, wo_ref,
                  x_ref, lg_ref, lb_ref, o_ref, ob_ref):
    dm = x_ref.shape[1]
    merged = jnp.zeros(x_ref.shape, F32)
    for n, (y_ref, w_ref) in enumerate(((ya_ref, wa_ref), (yb_ref, wb_ref), (yc_ref, wc_ref))):
        sl = slice(n * dm, (n + 1) * dm)
        gate = jax.nn.sigmoid(g_ref[:, sl].astype(F32) + bg_ref[:, sl])
        merged = merged + gate * _dot(y_ref[...], w_ref[...])
    mix = _dot(merged.astype(BF16), wo_ref[...])
    y = _layer_norm(ALPHA * x_ref[...] + mix, lg_ref[...], lb_ref[...])
    o_ref[...] = y
    ob_ref[...] = y.astype(BF16)


def _merge(ya, yb, yc, g, bg, wa, wb, wc, wo, x, lg, lb, *, tm):
    n, dm = x.shape
    row = lambda w: pl.BlockSpec((tm, w), lambda i: (i, 0))
    full = lambda a: pl.BlockSpec(a.shape, lambda i: (0, 0))
    return pl.pallas_call(
        _merge_kernel,
        grid=(n // tm,),
        in_specs=[row(ya.shape[1]), row(yb.shape[1]), row(yc.shape[1]), row(g.shape[1]),
                  full(bg), full(wa), full(wb), full(wc), full(wo), row(dm), full(lg), full(lb)],
        out_specs=[row(dm), row(dm)],
        out_shape=[jax.ShapeDtypeStruct((n, dm), F32), jax.ShapeDtypeStruct((n, dm), BF16)],
        compiler_params=_params(("parallel",)),
        name="merge_out_ln",
    )(ya, yb, yc, g, bg, wa, wb, wc, wo, x, lg, lb)


def _route(x, xb, wr_ref, rb_ref, comb_ref):
    w_hi, w_lo = _split_bf16(wr_ref[...])
    x_lo = (x - xb.astype(F32)).astype(BF16)
    logits = _dot(xb, w_hi) + _dot(x_lo, w_hi) + _dot(xb, w_lo)
    s = jax.nn.sigmoid(logits)
    sb = s + rb_ref[...]
    s_col = [s[:, e:e + 1] for e in range(N_EXPERTS)]
    b_col = [sb[:, e:e + 1] for e in range(N_EXPERTS)]
    best_score = None
    for gi in range(N_GROUPS):
        a, b, c, d = b_col[gi * EXPERTS_PER_GROUP:(gi + 1) * EXPERTS_PER_GROUP]
        m1, n1 = jnp.maximum(a, b), jnp.minimum(a, b)
        m2, n2 = jnp.maximum(c, d), jnp.minimum(c, d)
        score = jnp.maximum(m1, m2) + jnp.maximum(jnp.minimum(m1, m2), jnp.maximum(n1, n2))
        if gi == 0:
            best_score, best = score, jnp.zeros_like(score)
        else:
            better = score > best_score
            best = jnp.where(better, float(gi), best)
            best_score = jnp.where(better, score, best_score)
    picked = []
    for gi in range(N_GROUPS):
        cols = b_col[gi * EXPERTS_PER_GROUP:(gi + 1) * EXPERTS_PER_GROUP]
        top = _top_mask(cols, 2)
        in_group = best == float(gi)
        for j in range(EXPERTS_PER_GROUP):
            picked.append(jnp.where(in_group & top[j], s_col[gi * EXPERTS_PER_GROUP + j], 0.0))
    total = picked[0]
    for c in picked[1:]:
        total = total + c
    for e in range(N_EXPERTS):
        comb_ref[e] = picked[e] / total


def _moe_kernel(x_ref, xb_ref, wr_ref, rb_ref, wg_ref, wu_ref, wd_ref, lg_ref, lb_ref,
                o_ref, ob_ref, comb_ref, acc_ref):
    e = pl.program_id(1)

    @pl.when(e == 0)
    def _():
        _route(x_ref[...], xb_ref[...], wr_ref, rb_ref, comb_ref)
        acc_ref[...] = jnp.zeros_like(acc_ref)

    xb = xb_ref[...]
    gate = _dot(xb, wg_ref[0])
    h = gate * jax.nn.sigmoid(gate) * _dot(xb, wu_ref[0]) * comb_ref[e]
    acc_ref[...] += _dot(h.astype(BF16), wd_ref[0])

    @pl.when(e == pl.num_programs(1) - 1)
    def _():
        y = _layer_norm(ALPHA * x_ref[...] + acc_ref[...], lg_ref[...], lb_ref[...])
        o_ref[...] = y
        ob_ref[...] = y.astype(BF16)


def _moe(x, xb, wr, rb, wg, wu, wd, lg, lb, *, tm):
    n, dm = x.shape
    n_e, _, de = wg.shape
    row = pl.BlockSpec((tm, dm), lambda i, e: (i, 0))
    full = lambda a: pl.BlockSpec(a.shape, lambda i, e: (0, 0))
    return pl.pallas_call(
        _moe_kernel,
        grid=(n // tm, n_e),
        in_specs=[row, row, full(wr), full(rb),
                  pl.BlockSpec((1, dm, de), lambda i, e: (e, 0, 0)),
                  pl.BlockSpec((1, dm, de), lambda i, e: (e, 0, 0)),
                  pl.BlockSpec((1, de, dm), lambda i, e: (e, 0, 0)),
                  full(lg), full(lb)],
        out_specs=[row, row],
        out_shape=[jax.ShapeDtypeStruct((n, dm), F32), jax.ShapeDtypeStruct((n, dm), BF16)],
        scratch_shapes=[pltpu.VMEM((n_e, tm, 1), F32), pltpu.VMEM((tm, dm), F32)],
        compiler_params=_params(("parallel", "arbitrary")),
        name="moe_ln",
    )(x, xb, wr, rb, wg, wu, wd, lg, lb)


def _pad_cols(a, width):
    return jnp.pad(a, ((0, 0), (0, width - a.shape[1])))


def _layer_weights(w_in_l):
    scale = HEAD_DIM ** -0.5
    cuts = [QKV_W, 2 * QKV_W, 3 * QKV_W, 3 * QKV_W + QI_W, 3 * QKV_W + QI_W + IDX_DIM,
            3 * QKV_W + QI_W + IDX_DIM + IDX_HEADS]
    wq, wk, wv, wqi, wki, wwi, wg = jnp.split(w_in_l, cuts, axis=1)
    w_tok = _pad_cols(jnp.concatenate([wk, wv, wki, wwi], axis=1), N_COLS)
    col = jnp.arange(N_COLS)
    flag_tok = ((col < QKV_W) & (col // HEAD_DIM >= H_SB)) | ((col >= N_OFF_KI) & (col < N_OFF_WI))
    w_feat = jnp.concatenate([wq * scale, wqi, wv], axis=1).T
    grp = jnp.arange(T_ROWS // HEAD_DIM)
    flag_feat = ((grp >= H_SB) & (grp < N_HEADS)) | ((grp >= T_OFF_QI // HEAD_DIM)
                                                     & (grp < T_OFF_V // HEAD_DIM))
    w_sq = jnp.concatenate([wq * scale, wqi], axis=1)
    col_s = jnp.arange(QKV_W + QI_W)
    flag_sq = (col_s // HEAD_DIM >= H_SB)
    f = lambda a: a.astype(F32)[None, :]
    return dict(w_tok=w_tok.astype(BF16), flag_tok=f(flag_tok),
                w_feat=w_feat.astype(BF16), flag_feat=flag_feat.astype(jnp.int32),
                w_sq=w_sq.astype(BF16), flag_sq=f(flag_sq), w_gate=wg.astype(BF16))


def _rope_tables(pos):
    half = HEAD_DIM // 2
    inv = ROPE_THETA ** (-jnp.arange(half, dtype=F32) / half)
    ang = pos.astype(F32)[:, None] * inv[None, :]
    cos, sin = jnp.cos(ang), jnp.sin(ang)
    return jnp.concatenate([cos, cos], axis=1), jnp.concatenate([-sin, sin], axis=1)


def kernel(x_prompt, x_sample, cache_k, cache_v, cache_idx_k, page_table, w_in, w_br_a, w_br_b,
           w_br_c, b_gate, w_out, ln1_g, ln1_b, ln2_g, ln2_b, w_router, router_bias,
           w_exp_gate, w_exp_up, w_exp_down):
    n_b, seq, dm = x_prompt.shape
    n_seq, n_new, _ = x_sample.shape
    depth = w_in.shape[0]
    n_p = n_b * seq
    n_s = n_seq * n_new
    n_pages = page_table.shape[1]
    past = n_pages * PAGE_SIZE
    assert seq % SLAB == 0 and past % MOBA_BLOCK == 0 and n_new <= ROWS_PER_HEAD
    ck = cache_k.astype(BF16).reshape(cache_k.shape[:2] + (PAGE_SIZE, QKV_W))
    cv = cache_v.astype(BF16).reshape(cache_v.shape[:2] + (PAGE_SIZE, QKV_W))
    cik = cache_idx_k.astype(BF16)
    eye_h = jnp.eye(N_HEADS, dtype=F32)

    pos = jnp.concatenate([jnp.tile(jnp.arange(seq, dtype=jnp.int32), n_b),
                           jnp.tile(past + jnp.arange(n_new, dtype=jnp.int32), n_seq)])
    cos64, sin64 = _rope_tables(pos)
    cos = jnp.concatenate([cos64, cos64], axis=1)
    sin = jnp.concatenate([sin64, sin64], axis=1)
    cos_t, sin_t = cos64[:n_p].T, sin64[:n_p].T
    x = jnp.concatenate([x_prompt.reshape(n_p, dm), x_sample.reshape(n_s, dm)], axis=0)
    xb = x.astype(BF16)

    pad_rows = lambda a, n: jnp.pad(a, ((0, 0), (0, n - a.shape[1])) + ((0, 0),) * (a.ndim - 2))
    zeros_b = jnp.zeros((HEAD_DIM, dm), BF16)
    rows_p, rows_s = [], []
    for l in range(depth):
        w = _layer_weights(w_in[l])
        tok_f32, tok_bf = _proj(xb, w["w_tok"], cos, sin, w["flag_tok"], rope=True,
                                tm=TILE_TOKENS, tn=TILE_COLS, name="in_proj_token_major")
        gates = _proj(xb, w["w_gate"], cos, sin, w["flag_tok"], rope=False,
                      tm=TILE_TOKENS, tn=TILE_COLS, name="in_proj_gates")
        proj_t = _proj_t(w["flag_feat"], w["w_feat"], xb[:n_p], cos_t, sin_t,
                         tm=TILE_TOKENS, tn=TILE_ROWS_T)
        sq, _ = _proj(xb[n_p:], w["w_sq"], cos[n_p:], sin[n_p:], w["flag_sq"], rope=True,
                      tm=n_s, tn=TILE_COLS, name="in_proj_sample_q")

        k_all, v_all = tok_f32[:, :QKV_W], tok_f32[:, QKV_W:2 * QKV_W]
        ki_all = tok_f32[:, N_OFF_KI:N_OFF_WI]
        rows_p.append((k_all[:n_p].reshape(n_b, seq, N_HEADS, HEAD_DIM),
                       v_all[:n_p].reshape(n_b, seq, N_HEADS, HEAD_DIM),
                       ki_all[:n_p].reshape(n_b, seq, IDX_DIM)))
        rows_s.append((k_all[n_p:].reshape(n_seq, n_new, N_HEADS, HEAD_DIM),
                       v_all[n_p:].reshape(n_seq, n_new, N_HEADS, HEAD_DIM),
                       ki_all[n_p:].reshape(n_seq, n_new, IDX_DIM)))

        y_a = _sb_attention(proj_t, tok_bf, n_b=n_b, seq=seq)
        y_b = _dsa_attention(proj_t, tok_bf, tok_f32, n_b=n_b, seq=seq)
        kmean = _block_means(tok_f32, n_p).reshape(n_b, seq // MOBA_BLOCK, QKV_W)
        y_c = _moba_attention(proj_t, tok_bf, kmean, n_b=n_b, seq=seq)
        tok = lambda y: y.transpose(0, 2, 1).reshape(n_p, -1)

        q_s = sq[:, :QKV_W].reshape(n_seq, n_new, N_HEADS, HEAD_DIM)
        q8 = pad_rows(q_s, ROWS_PER_HEAD).transpose(0, 2, 1, 3)
        qbd = (q8[:, :, :, None, :] * eye_h[None, :, None, :, None]).reshape(
            n_seq, N_HEADS * ROWS_PER_HEAD, QKV_W).astype(BF16)
        qi8 = pad_rows(sq[:, QKV_W:].reshape(n_seq, n_new, QI_W), ROWS_PER_HEAD).reshape(
            n_seq, ROWS_PER_HEAD * IDX_HEADS, IDX_DIM).astype(BF16)
        tok_s = tok_f32[n_p:].reshape(n_seq, n_new, -1)
        wi8 = pad_rows(tok_s[..., N_OFF_WI:N_OFF_WI + IDX_HEADS], ROWS_PER_HEAD).reshape(
            n_seq, ROWS_PER_HEAD * IDX_HEADS, 1)
        new_rows = lambda a: pad_rows(a, PAGE_SIZE).astype(BF16)
        y_s = _sample_attention(
            l, page_table, qbd, qi8, wi8, new_rows(tok_s[..., :QKV_W]),
            new_rows(tok_s[..., QKV_W:2 * QKV_W]), new_rows(tok_s[..., N_OFF_KI:N_OFF_WI]),
            ck, cv, cik, n_new=n_new)
        y_s = y_s.reshape(n_s, QKV_W).astype(BF16)

        wa_cols, wb_cols = H_SB * HEAD_DIM, (H_SB + H_DSA) * HEAD_DIM
        blk = DSA_BLOCK_HEADS * HEAD_DIM
        ya = jnp.concatenate([tok(y_a), y_s[:, :wa_cols]], axis=0)
        yb = jnp.concatenate([tok(y_b), y_s[:, wa_cols:wa_cols + blk]], axis=0)
        yc = jnp.concatenate([tok(y_c), y_s[:, QKV_W - blk:]], axis=0)
        w_b = jnp.concatenate([w_br_b[l].astype(BF16), zeros_b], axis=0)
        w_c = jnp.concatenate([zeros_b, w_br_c[l].astype(BF16)], axis=0)
        del wb_cols
        x, xb = _merge(ya, yb, yc, gates, b_gate[l][None, :], w_br_a[l].astype(BF16), w_b, w_c,
                       w_out[l].astype(BF16), x, ln1_g[l][None, :], ln1_b[l][None, :],
                       tm=TILE_TOKENS)
        x, xb = _moe(x, xb, w_router, router_bias[None, :], w_exp_gate[l].astype(BF16),
                     w_exp_up[l].astype(BF16), w_exp_down[l].astype(BF16),
                     ln2_g[l][None, :], ln2_b[l][None, :], tm=TILE_TOKENS)

    stack = lambda rows, i: jnp.stack([r[i] for r in rows], 0)
    return (x[:n_p].reshape(n_b, seq, dm), x[n_p:].reshape(n_seq, n_new, dm),
            stack(rows_p, 0), stack(rows_p, 1), stack(rows_p, 2),
            stack(rows_s, 0), stack(rows_s, 1), stack(rows_s, 2))
```

```python
import functools

import jax
import jax.numpy as jnp
from jax import lax
from jax.experimental import pallas as pl
from jax.experimental.pallas import tpu as pltpu

F32 = jnp.float32
BF16 = jnp.bfloat16

HEAD_DIM = 64
H_SB, H_DSA, H_MOBA = 6, 5, 5
N_HEADS = H_SB + H_DSA + H_MOBA
IDX_HEADS = 8
IDX_DIM = 64
DSA_TOPK = 256
MOBA_BLOCK = 256
MOBA_TOPK = 3
N_EXPERTS = 16
EXPERTS_PER_GROUP = 4
N_GROUPS = N_EXPERTS // EXPERTS_PER_GROUP
ROPE_THETA = 10000.0
LN_EPS = 1e-5
DEPTH = 2
ALPHA = (2 * DEPTH) ** 0.25
PAGE_SIZE = 128

LANES = 128
SLAB = 256
PAIR = 2 * HEAD_DIM
NEG = -1e30
INT_MIN = -2 ** 31
VMEM_LIMIT = 56 * 1024 * 1024

QKV_W = N_HEADS * HEAD_DIM
QI_W = IDX_HEADS * IDX_DIM
T_OFF_QI = QKV_W
T_OFF_V = QKV_W + QI_W
T_ROWS = 2 * QKV_W + QI_W
N_OFF_KI = 2 * QKV_W
N_OFF_WI = N_OFF_KI + IDX_DIM
TILE_TOKENS = 512
TILE_COLS = 768
N_COLS = 3 * TILE_COLS
TILE_ROWS_T = 640

_NT = (((1,), (1,)), ((), ()))


def _nt(a, b):
    return lax.dot_general(a, b, _NT, preferred_element_type=F32)


def _dot(a, b):
    return jnp.dot(a, b, preferred_element_type=F32)


def _split_bf16(x):
    hi = x.astype(BF16)
    lo = (x - hi.astype(F32)).astype(BF16)
    return hi, lo


def _softplus(z):
    return jnp.maximum(z, 0.0) + jnp.log(1.0 + jnp.exp(-jnp.abs(z)))


def _params(sem):
    return pltpu.CompilerParams(dimension_semantics=sem, vmem_limit_bytes=VMEM_LIMIT)


def _proj_kernel(x_ref, w_ref, cos_ref, sin_ref, flag_ref, o_ref, ob_ref, *, rope):
    acc = _dot(x_ref[...], w_ref[...])
    if not rope:
        ob_ref[...] = acc.astype(ob_ref.dtype)
        return
    cos = cos_ref[...]
    sin = sin_ref[...]
    lane = lax.broadcasted_iota(jnp.int32, (1, LANES), 1)
    first_half = (lane % HEAD_DIM) < (HEAD_DIM // 2)
    for c in range(acc.shape[1] // LANES):
        sl = slice(c * LANES, (c + 1) * LANES)
        a = acc[:, sl]
        sw = jnp.where(first_half, pltpu.roll(a, LANES - HEAD_DIM // 2, 1),
                       pltpu.roll(a, HEAD_DIM // 2, 1))
        r = jnp.where(flag_ref[:, sl] > 0.0, a * cos + sw * sin, a)
        o_ref[:, sl] = r
        ob_ref[:, sl] = r.astype(BF16)


def _proj(x, w, cos, sin, flag, *, rope, tm, tn, name):
    n, d = x.shape
    width = w.shape[1]
    tile = pl.BlockSpec((tm, tn), lambda j, i: (i, j))
    if rope:
        out_specs = [tile, tile]
        out_shape = [jax.ShapeDtypeStruct((n, width), F32), jax.ShapeDtypeStruct((n, width), BF16)]
        body = functools.partial(_proj_kernel, rope=True)
    else:
        out_specs = tile
        out_shape = jax.ShapeDtypeStruct((n, width), BF16)
        flag = jnp.zeros((1, width), F32)
        body = lambda x_r, w_r, c_r, s_r, f_r, ob_r: _proj_kernel(
            x_r, w_r, c_r, s_r, f_r, None, ob_r, rope=False)
    return pl.pallas_call(
        body,
        grid=(width // tn, n // tm),
        in_specs=[
            pl.BlockSpec((tm, d), lambda j, i: (i, 0)),
            pl.BlockSpec((d, tn), lambda j, i: (0, j)),
            pl.BlockSpec((tm, LANES), lambda j, i: (i, 0)),
            pl.BlockSpec((tm, LANES), lambda j, i: (i, 0)),
            pl.BlockSpec((1, tn), lambda j, i: (0, j)),
        ],
        out_specs=out_specs,
        out_shape=out_shape,
        compiler_params=_params(("parallel", "parallel")),
        name=name,
    )(x, w, cos, sin, flag)


def _proj_t_kernel(flag_ref, w_ref, x_ref, cos_ref, sin_ref, o_ref, *, tn, tm):
    j = pl.program_id(1)
    acc = _nt(w_ref[...], x_ref[...])
    cos = cos_ref[...]
    sin = sin_ref[...]
    half = HEAD_DIM // 2
    for g in range(tn // HEAD_DIM):
        a = acc[g * HEAD_DIM:(g + 1) * HEAD_DIM]
        sw = jnp.concatenate([a[half:], a[:half]], axis=0)
        r = jnp.where(flag_ref[j * (tn // HEAD_DIM) + g] > 0, a * cos + sw * sin, a).astype(BF16)
        for s in range(tm // SLAB):
            o_ref[s, g * HEAD_DIM:(g + 1) * HEAD_DIM, :] = r[:, s * SLAB:(s + 1) * SLAB]


def _proj_t(flags, w_t, x, cos_t, sin_t, *, tm, tn):
    n, d = x.shape
    rows = w_t.shape[0]
    spt = tm // SLAB
    grid_spec = pltpu.PrefetchScalarGridSpec(
        num_scalar_prefetch=1,
        grid=(n // tm, rows // tn),
        in_specs=[
            pl.BlockSpec((tn, d), lambda i, j, f: (j, 0)),
            pl.BlockSpec((tm, d), lambda i, j, f: (i, 0)),
            pl.BlockSpec((HEAD_DIM, tm), lambda i, j, f: (0, i)),
            pl.BlockSpec((HEAD_DIM, tm), lambda i, j, f: (0, i)),
        ],
        out_specs=pl.BlockSpec((spt, tn, SLAB), lambda i, j, f: (i, j, 0)),
    )
    return pl.pallas_call(
        functools.partial(_proj_t_kernel, tn=tn, tm=tm),
        grid_spec=grid_spec,
        out_shape=jax.ShapeDtypeStruct((n // SLAB, rows, SLAB), BF16),
        compiler_params=_params(("parallel", "arbitrary")),
        name="in_proj_feature_major",
    )(flags, w_t, x, cos_t, sin_t)


def _head_masks(q_pair):
    first = lax.broadcasted_iota(jnp.int32, (PAIR, 1), 0) < HEAD_DIM
    zero = jnp.zeros_like(q_pair)
    return jnp.where(first, q_pair, zero), jnp.where(first, zero, q_pair)


def _online_update(state, scores, masks, v_t):
    m, l, acc = state
    scores = [jnp.where(mk, s, NEG) for s, mk in zip(scores, masks)]
    m_new = m
    for s in scores:
        m_new = jnp.maximum(m_new, jnp.max(s, axis=0, keepdims=True))
    alpha = jnp.exp(m - m_new)
    probs = [jnp.where(mk, jnp.exp(s - m_new), 0.0) for s, mk in zip(scores, masks)]
    l = alpha * l
    for p in probs:
        l = l + jnp.sum(p, axis=0, keepdims=True)
    p_cat = probs[0] if len(probs) == 1 else jnp.concatenate(probs, axis=0)
    acc = alpha * acc + _dot(v_t, p_cat.astype(BF16))
    return m_new, l, acc


def _sb_kernel(q_ref, k_ref, v_ref, tri_ref, o_ref):
    t = SLAB
    i = pl.program_id(2)
    qm = _head_masks(q_ref[...])
    tri = tri_ref[...]
    key_i = lax.broadcasted_iota(jnp.int32, (t, t), 0)
    qry_i = lax.broadcasted_iota(jnp.int32, (t, t), 1)
    before = key_i < qry_i

    def tiles(js, carry, diag):
        chains = [(n, hh) for n in range(len(js)) for hh in range(2)]
        ks = [k_ref[pl.ds(pl.multiple_of(j * t, t), t), :] for j in js]
        vs = [v_ref[j] for j in js]
        z = {ch: _dot(ks[ch[0]], qm[ch[1]]) for ch in chains}
        sp = {ch: _softplus(z[ch]) for ch in chains}
        lk = {ch: -sp[ch] for ch in chains}
        if diag:
            lk = {ch: jnp.where(before, lk[ch], 0.0) for ch in chains}
        parts = {ch: _split_bf16(lk[ch]) for ch in chains}
        later = {ch: _dot(tri, parts[ch][0]) + _dot(tri, parts[ch][1]) for ch in chains}
        out = []
        for hh in range(2):
            c, acc = carry[2 * hh:2 * hh + 2]
            a_all = []
            for n in range(len(js)):
                ch = (n, hh)
                a = jnp.exp(z[ch] - sp[ch] + later[ch] + c)
                if diag:
                    a = jnp.where(before, a, 0.0)
                a_all.append(a.astype(BF16))
                c = c + jnp.sum(lk[ch], axis=0, keepdims=True)
            v_cat = jnp.concatenate([v[hh * HEAD_DIM:(hh + 1) * HEAD_DIM] for v in vs], axis=1)
            acc = acc + _dot(v_cat, jnp.concatenate(a_all, axis=0))
            out += [c, acc]
        return tuple(out)

    init = (jnp.zeros((1, t), F32), jnp.zeros((HEAD_DIM, t), F32)) * 2
    carry = tiles([i], init, True)
    carry = lax.fori_loop(
        0, i // 2, lambda n, ca: tiles([i - 1 - 2 * n, i - 2 - 2 * n], ca, False), carry)
    carry = lax.cond(i % 2 == 1, lambda ca: tiles([0], ca, False), lambda ca: ca, carry)
    o_ref[...] = jnp.concatenate([carry[1], carry[3]], axis=0).astype(o_ref.dtype)


def _tri_upper(n):
    r = lax.broadcasted_iota(jnp.int32, (n, n), 0)
    c = lax.broadcasted_iota(jnp.int32, (n, n), 1)
    return (c > r).astype(BF16)


def _sb_attention(proj_t, k_tok, *, n_b, seq):
    nq = seq // SLAB
    n_pairs = H_SB // 2
    v_blk0 = T_OFF_V // PAIR
    return pl.pallas_call(
        _sb_kernel,
        grid=(n_b, n_pairs, nq),
        in_specs=[
            pl.BlockSpec((None, PAIR, SLAB), lambda b, p, i: (b * nq + i, p, 0)),
            pl.BlockSpec((seq, PAIR), lambda b, p, i: (b, p)),
            pl.BlockSpec((nq, PAIR, SLAB), lambda b, p, i: (b, v_blk0 + p, 0)),
            pl.BlockSpec((SLAB, SLAB), lambda b, p, i: (0, 0)),
        ],
        out_specs=pl.BlockSpec((None, PAIR, SLAB), lambda b, p, i: (b * nq + i, p, 0)),
        out_shape=jax.ShapeDtypeStruct((n_b * nq, n_pairs * PAIR, SLAB), BF16),
        compiler_params=_params(("parallel", "parallel", "arbitrary")),
        name="sb_attention",
    )(proj_t, k_tok, proj_t, _tri_upper(SLAB))


def _sort_key(score):
    bits = lax.bitcast_convert_type(score + 0.0, jnp.int32)
    return bits ^ ((bits >> 31) & jnp.int32(0x7FFFFFFF))


def _kth_largest(count_ge, shape, k, bits_per_step=1):
    def body(it, ans):
        unit = jnp.left_shift(jnp.int32(1), 32 - bits_per_step * (it + 1))
        passed = jnp.zeros(shape, jnp.int32)
        for m in range(1, 2 ** bits_per_step):
            passed = passed + jnp.where(count_ge(ans + m * unit) >= k, 1, 0)
        return ans + passed * unit
    return lax.fori_loop(0, 32 // bits_per_step, body, jnp.full(shape, INT_MIN, jnp.int32))


def _tie_cut(count_eq_below, need, shape, n_bits):
    def body(it, a):
        cand = a + jnp.left_shift(jnp.int32(1), n_bits - 1 - it)
        return jnp.where(count_eq_below(cand) < need, cand, a)
    return lax.fori_loop(0, n_bits, body, jnp.zeros(shape, jnp.int32))


DSA_BLOCK_HEADS = 6


def _dsa_kernel(qi_ref, kiwi_ref, ki_ref, q_ref, k_ref, v_ref, o_ref,
                key_scr, m_scr, l_scr, acc_scr, *, n_keep, n_bits):
    t = SLAB
    i = pl.program_id(1)
    n_slab = i + 1
    qpos = i * t + lax.broadcasted_iota(jnp.int32, (1, t), 1)
    kiota = lax.broadcasted_iota(jnp.int32, (t, 1), 0)
    wi_t = kiwi_ref[...].T

    def scores(j, _):
        kij = ki_ref[pl.ds(pl.multiple_of(j * t, t), t), :][:, :IDX_DIM]
        score = jnp.zeros((t, t), F32)
        for h in range(IDX_HEADS):
            sc = _dot(kij, qi_ref[h * IDX_DIM:(h + 1) * IDX_DIM, :])
            score = score + wi_t[IDX_DIM + h:IDX_DIM + h + 1, :] * jnp.maximum(sc, 0.0)
        causal = (j * t + kiota) <= qpos
        key_scr[j] = _sort_key(jnp.where(causal, score, -jnp.inf))
        return 0

    lax.fori_loop(0, n_slab, scores, 0)

    def count(pred):
        def body(j, acc):
            ones = jnp.where(pred(key_scr[j], j * t + kiota), 1.0, 0.0)
            return acc + jnp.sum(ones.reshape(t // 8, 8, t), axis=0)
        acc = lax.fori_loop(0, n_slab, body, jnp.zeros((8, t), F32))
        return jnp.sum(acc, axis=0, keepdims=True)

    thr = _kth_largest(lambda cand: count(lambda key, kp: key >= cand), (1, t), float(n_keep))
    need = float(n_keep) - count(lambda key, kp: key > thr)
    n_eq = count(lambda key, kp: key == thr)
    cut = lax.cond(
        jnp.max(n_eq - need) > 0.0,
        lambda: _tie_cut(lambda cand: count(lambda key, kp: (key == thr) & (kp < cand)),
                         need, (1, t), n_bits),
        lambda: jnp.full((1, t), 2 ** n_bits, jnp.int32))

    m_scr[...] = jnp.full(m_scr.shape, NEG, F32)
    l_scr[...] = jnp.zeros(l_scr.shape, F32)
    acc_scr[...] = jnp.zeros(acc_scr.shape, F32)
    qm = []
    for pr in range(DSA_BLOCK_HEADS // 2):
        qm += list(_head_masks(q_ref[pr * PAIR:(pr + 1) * PAIR, :]))

    def attend(j, _):
        key = key_scr[j]
        kp = j * t + kiota
        sel = ((key > thr) | ((key == thr) & (kp <= cut))) & (kp <= qpos)
        start = pl.multiple_of(j * t, t)
        vj = v_ref[j]
        kj = [k_ref[pl.ds(start, t), pr * PAIR:(pr + 1) * PAIR] for pr in range((H_DSA + 1) // 2)]
        z = [_dot(kj[h // 2], qm[h]) for h in range(H_DSA)]
        for h in range(H_DSA):
            state = (m_scr[h], l_scr[h], acc_scr[h])
            m, l, acc = _online_update(state, [z[h]], [sel], vj[h * HEAD_DIM:(h + 1) * HEAD_DIM])
            m_scr[h] = m
            l_scr[h] = l
            acc_scr[h] = acc
        return 0

    lax.fori_loop(0, n_slab, attend, 0)
    for h in range(H_DSA):
        o_ref[h * HEAD_DIM:(h + 1) * HEAD_DIM, :] = (acc_scr[h] / l_scr[h]).astype(o_ref.dtype)
    o_ref[H_DSA * HEAD_DIM:, :] = jnp.zeros((HEAD_DIM, t), o_ref.dtype)


def _dsa_attention(proj_t, k_tok, kiwi_f32, *, n_b, seq):
    nq = seq // SLAB
    wide = DSA_BLOCK_HEADS * HEAD_DIM
    n_keep = min(DSA_TOPK, seq // 4)
    n_bits = max(1, (seq - 1).bit_length())
    kiwi_blk = N_OFF_KI // LANES
    return pl.pallas_call(
        functools.partial(_dsa_kernel, n_keep=n_keep, n_bits=n_bits),
        grid=(n_b, nq),
        in_specs=[
            pl.BlockSpec((None, QI_W, SLAB), lambda b, i: (b * nq + i, T_OFF_QI // QI_W, 0)),
            pl.BlockSpec((SLAB, LANES), lambda b, i: (b * nq + i, kiwi_blk)),
            pl.BlockSpec((seq, LANES), lambda b, i: (b, kiwi_blk)),
            pl.BlockSpec((None, wide, SLAB), lambda b, i: (b * nq + i, 1, 0)),
            pl.BlockSpec((seq, wide), lambda b, i: (b, 1)),
            pl.BlockSpec((nq, wide, SLAB), lambda b, i: (b, T_OFF_V // wide + 1, 0)),
        ],
        out_specs=pl.BlockSpec((None, wide, SLAB), lambda b, i: (b * nq + i, 0, 0)),
        out_shape=jax.ShapeDtypeStruct((n_b * nq, wide, SLAB), BF16),
        scratch_shapes=[pltpu.VMEM((nq, SLAB, SLAB), jnp.int32),
                        pltpu.VMEM((H_DSA, 1, SLAB), F32), pltpu.VMEM((H_DSA, 1, SLAB), F32),
                        pltpu.VMEM((H_DSA, HEAD_DIM, SLAB), F32)],
        compiler_params=_params(("parallel", "arbitrary")),
        name="dsa_attention",
    )(proj_t, kiwi_f32, k_tok, proj_t, k_tok, proj_t)


def _block_mean_kernel(k_ref, o_ref):
    o_ref[0] = jnp.mean(k_ref[:, :QKV_W], axis=0, keepdims=True)


def _block_means(k_tok_f32, n_tokens):
    nb = n_tokens // MOBA_BLOCK
    return pl.pallas_call(
        _block_mean_kernel,
        grid=(nb,),
        in_specs=[pl.BlockSpec((MOBA_BLOCK, QKV_W), lambda n: (n, 0))],
        out_specs=pl.BlockSpec((1, 1, QKV_W), lambda n: (n, 0, 0)),
        out_shape=jax.ShapeDtypeStruct((nb, 1, QKV_W), F32),
        compiler_params=_params(("parallel",)),
        name="moba_block_means",
    )(k_tok_f32)


MOBA_FIRST_PAIR = (H_SB + H_DSA) // 2


def _moba_kernel(q_ref, k_ref, v_ref, km_ref, o_ref, sel_scr, *, nb):
    t = MOBA_BLOCK
    i = pl.program_id(2)
    qm = _head_masks(q_ref[...])
    km_hi, km_lo = _split_bf16(km_ref[...])
    n_iota = lax.broadcasted_iota(jnp.int32, (nb, t), 0)
    fi = n_iota.astype(F32)
    valid = n_iota < i
    for hh in range(2):
        gm = jnp.where(valid, _dot(km_hi, qm[hh]) + _dot(km_lo, qm[hh]), -jnp.inf)
        sel = jnp.zeros((nb, t), jnp.bool_)
        for _ in range(min(MOBA_TOPK, nb)):
            m = jnp.max(gm, axis=0, keepdims=True)
            first = jnp.min(jnp.where(gm == m, fi, float(nb)), axis=0, keepdims=True)
            pick = fi == first
            sel = sel | pick
            gm = jnp.where(pick, -jnp.inf, gm)
        sel_scr[hh] = jnp.where(sel & valid, 1.0, 0.0)

    def blocks(ns, carry, mask_of):
        ks = [k_ref[pl.ds(pl.multiple_of(n * t, t), t), :] for n in ns]
        vs = [v_ref[n] for n in ns]
        z = [[_dot(kj, qm[hh]) for kj in ks] for hh in range(2)]
        out = []
        for hh in range(2):
            v_cat = jnp.concatenate([v[hh * HEAD_DIM:(hh + 1) * HEAD_DIM] for v in vs], axis=1)
            out += list(_online_update(carry[3 * hh:3 * hh + 3], z[hh],
                                       [mask_of(hh, n) for n in ns], v_cat))
        return tuple(out)

    key_i = lax.broadcasted_iota(jnp.int32, (t, t), 0)
    qry_i = lax.broadcasted_iota(jnp.int32, (t, t), 1)
    chosen = lambda hh, n: sel_scr[hh, pl.ds(n, 1), :] > 0.0
    init = (jnp.full((1, t), NEG, F32), jnp.zeros((1, t), F32), jnp.zeros((HEAD_DIM, t), F32)) * 2
    carry = blocks([i], init, lambda hh, n: key_i <= qry_i)
    carry = lax.fori_loop(0, i // 2, lambda n, ca: blocks([2 * n, 2 * n + 1], ca, chosen), carry)
    carry = lax.cond(i % 2 == 1, lambda ca: blocks([i - 1], ca, chosen), lambda ca: ca, carry)
    o_ref[...] = jnp.concatenate([carry[2] / carry[1], carry[5] / carry[4]],
                                 axis=0).astype(o_ref.dtype)


def _moba_attention(proj_t, k_tok, kmean, *, n_b, seq):
    nq = seq // SLAB
    nb = seq // MOBA_BLOCK
    n_pairs = N_HEADS // 2 - MOBA_FIRST_PAIR
    p0 = MOBA_FIRST_PAIR
    v_blk0 = T_OFF_V // PAIR + p0
    return pl.pallas_call(
        functools.partial(_moba_kernel, nb=nb),
        grid=(n_b, n_pairs, nq),
        in_specs=[
            pl.BlockSpec((None, PAIR, SLAB), lambda b, p, i: (b * nq + i, p0 + p, 0)),
            pl.BlockSpec((seq, PAIR), lambda b, p, i: (b, p0 + p)),
            pl.BlockSpec((nq, PAIR, SLAB), lambda b, p, i: (b, v_blk0 + p, 0)),
            pl.BlockSpec((None, nb, PAIR), lambda b, p, i: (b, 0, p0 + p)),
        ],
        out_specs=pl.BlockSpec((None, PAIR, SLAB), lambda b, p, i: (b * nq + i, p, 0)),
        out_shape=jax.ShapeDtypeStruct((n_b * nq, n_pairs * PAIR, SLAB), BF16),
        scratch_shapes=[pltpu.VMEM((2, nb, SLAB), F32)],
        compiler_params=_params(("parallel", "parallel", "arbitrary")),
        name="moba_attention",
    )(proj_t, k_tok, proj_t, kmean)


ROWS_PER_HEAD = 8
PAGES_PER_STEP = 4


def _top_mask(cols, n_top):
    out = []
    for n, g in enumerate(cols):
        rank = jnp.zeros_like(g)
        for m, o in enumerate(cols):
            if m == n:
                continue
            ahead = (o >= g) if m < n else (o > g)
            rank = rank + jnp.where(ahead, 1.0, 0.0)
        out.append(rank < float(n_top))
    return out


def _sample_mixers(s_scr, p_scr, ki_scr, qi_ref, wi_ref, *, past, n_keep, n_bits):
    lk_len = past + PAGE_SIZE
    r8 = ROWS_PER_HEAD
    kpos = lax.broadcasted_iota(jnp.int32, (1, lk_len), 1)

    def qpos(rows):
        return past + lax.broadcasted_iota(jnp.int32, (rows, 1), 0) % r8

    ra = H_SB * r8
    z = s_scr[:ra, :]
    before = kpos < qpos(ra)
    sp = _softplus(z)
    lk = jnp.where(before, -sp, 0.0)
    hi, lo = _split_bf16(lk)
    tri_r = lax.broadcasted_iota(jnp.int32, (LANES, LANES), 0)
    tri_c = lax.broadcasted_iota(jnp.int32, (LANES, LANES), 1)
    tri = (tri_r > tri_c).astype(BF16)
    c = jnp.zeros((ra, 1), F32)
    for blk in reversed(range(lk_len // LANES)):
        sl = slice(blk * LANES, (blk + 1) * LANES)
        later = _dot(hi[:, sl], tri) + _dot(lo[:, sl], tri) + c
        a = jnp.where(before[:, sl], jnp.exp(z[:, sl] - sp[:, sl] + later), 0.0)
        p_scr[:ra, sl] = a.astype(BF16)
        c = c + jnp.sum(lk[:, sl], axis=1, keepdims=True)

    sc = jnp.maximum(_nt(qi_ref[0], ki_scr[...]), 0.0) * wi_ref[0]
    score = jnp.sum(sc.reshape(r8, IDX_HEADS, lk_len), axis=1)
    causal8 = kpos <= qpos(r8)
    key = _sort_key(jnp.where(causal8, score, -jnp.inf))

    def count(pred):
        return jnp.sum(jnp.where(pred, 1.0, 0.0), axis=1, keepdims=True)

    thr = _kth_largest(lambda cand: count(key >= cand), (r8, 1), float(n_keep), bits_per_step=2)
    need = float(n_keep) - count(key > thr)
    n_eq = count(key == thr)
    cut = lax.cond(
        jnp.max(n_eq - need) > 0.0,
        lambda: _tie_cut(lambda cand: count((key == thr) & (kpos < cand)), need, (r8, 1), n_bits),
        lambda: jnp.full((r8, 1), 2 ** n_bits, jnp.int32))
    sel8 = ((key > thr) | ((key == thr) & (kpos <= cut))) & causal8
    rb = H_DSA * r8
    sel = jnp.concatenate([jnp.where(sel8, 1.0, 0.0)] * H_DSA, axis=0) > 0.0
    s = jnp.where(sel, s_scr[ra:ra + rb, :], NEG)
    p = jnp.where(sel, jnp.exp(s - jnp.max(s, axis=1, keepdims=True)), 0.0)
    l_b = jnp.sum(p, axis=1, keepdims=True)
    p_scr[ra:ra + rb, :] = p.astype(BF16)

    rc = H_MOBA * r8
    sm = s_scr[ra + rb:, :]
    n_blk = past // MOBA_BLOCK
    gate = [jnp.sum(sm[:, n * MOBA_BLOCK:(n + 1) * MOBA_BLOCK], axis=1, keepdims=True)
            for n in range(n_blk)]
    chosen = _top_mask(gate, min(MOBA_TOPK, n_blk + 1))
    own_ok = kpos[:, past:] <= qpos(rc)
    pieces = [jnp.where(chosen[n], sm[:, n * MOBA_BLOCK:(n + 1) * MOBA_BLOCK], NEG)
              for n in range(n_blk)]
    pieces.append(jnp.where(own_ok, sm[:, past:], NEG))
    m_c = pieces[0].max(axis=1, keepdims=True)
    for pc in pieces[1:]:
        m_c = jnp.maximum(m_c, pc.max(axis=1, keepdims=True))
    l_c = jnp.zeros((rc, 1), F32)
    off = 0
    for pc in pieces:
        e = jnp.where(pc > 0.5 * NEG, jnp.exp(pc - m_c), 0.0)
        l_c = l_c + jnp.sum(e, axis=1, keepdims=True)
        p_scr[ra + rb:, off:off + pc.shape[1]] = e.astype(BF16)
        off += pc.shape[1]

    return jnp.concatenate([jnp.ones((ra, 1), F32), 1.0 / l_b, 1.0 / l_c], axis=0)


def _sample_kernel(pt_ref, q_ref, qi_ref, wi_ref, knew_ref, vnew_ref, kinew_ref, *rest,
                   n_pages, n_keep, n_bits):
    del pt_ref
    pps = PAGES_PER_STEP
    k_pages, v_pages, i_pages = rest[:pps], rest[pps:2 * pps], rest[2 * pps:3 * pps]
    o_ref = rest[3 * pps]
    k_scr, v_scr, ki_scr, s_scr, p_scr = rest[3 * pps + 1:]
    past = n_pages * PAGE_SIZE
    r8 = ROWS_PER_HEAD
    step = pl.program_id(1)

    for pg in range(pps):
        rows = pl.ds(pl.multiple_of((step * pps + pg) * PAGE_SIZE, PAGE_SIZE), PAGE_SIZE)
        k_scr[:, rows, :] = pltpu.einshape("thd->htd", k_pages[pg][...]).astype(BF16)
        v_scr[:, rows, :] = pltpu.einshape("thd->htd", v_pages[pg][...]).astype(BF16)
        ki_scr[rows, :] = i_pages[pg][...].astype(BF16)

    @pl.when(step == pl.num_programs(1) - 1)
    def _():
        k_scr[:, past:, :] = knew_ref[0]
        v_scr[:, past:, :] = vnew_ref[0]
        ki_scr[past:, :] = kinew_ref[0]
        for h in range(N_HEADS):
            s_scr[h * r8:(h + 1) * r8, :] = _nt(q_ref[0, h], k_scr[h])
        inv = _sample_mixers(s_scr, p_scr, ki_scr, qi_ref, wi_ref, past=past, n_keep=n_keep,
                             n_bits=n_bits)
        for h in range(N_HEADS):
            rows = slice(h * r8, (h + 1) * r8)
            o_ref[0, :, h * HEAD_DIM:(h + 1) * HEAD_DIM] = _dot(p_scr[rows, :], v_scr[h]) * inv[rows]


def _sample_attention(layer, page_table, q8, qi8, wi8, knew, vnew, kinew,
                      cache_k, cache_v, cache_ik, *, n_new):
    n_seq, n_pages = page_table.shape
    pps = PAGES_PER_STEP
    past = n_pages * PAGE_SIZE
    lk_len = past + PAGE_SIZE
    n_keep = min(DSA_TOPK, (past + n_new) // 4)
    n_bits = max(1, (lk_len - 1).bit_length())
    rows = N_HEADS * ROWS_PER_HEAD

    def page_spec(j, *dims):
        return pl.BlockSpec((None, None, PAGE_SIZE) + dims,
                            lambda b, s, pt: (layer, pt[b, s * pps + j], 0) + (0,) * len(dims))

    def seq_spec(*dims):
        return pl.BlockSpec((1,) + dims, lambda b, s, pt: (b,) + (0,) * len(dims))

    in_specs = [seq_spec(N_HEADS, ROWS_PER_HEAD, HEAD_DIM),
                seq_spec(ROWS_PER_HEAD * IDX_HEADS, IDX_DIM),
                seq_spec(ROWS_PER_HEAD * IDX_HEADS, 1),
                seq_spec(N_HEADS, PAGE_SIZE, HEAD_DIM), seq_spec(N_HEADS, PAGE_SIZE, HEAD_DIM),
                seq_spec(PAGE_SIZE, IDX_DIM)]
    in_specs += [page_spec(j, N_HEADS, HEAD_DIM) for j in range(pps)]
    in_specs += [page_spec(j, N_HEADS, HEAD_DIM) for j in range(pps)]
    in_specs += [page_spec(j, IDX_DIM) for j in range(pps)]
    grid_spec = pltpu.PrefetchScalarGridSpec(
        num_scalar_prefetch=1,
        grid=(n_seq, n_pages // pps),
        in_specs=in_specs,
        out_specs=pl.BlockSpec((1, ROWS_PER_HEAD, QKV_W), lambda b, s, pt: (b, 0, 0)),
        scratch_shapes=[pltpu.VMEM((N_HEADS, lk_len, HEAD_DIM), BF16),
                        pltpu.VMEM((N_HEADS, lk_len, HEAD_DIM), BF16),
                        pltpu.VMEM((lk_len, IDX_DIM), BF16),
                        pltpu.VMEM((rows, lk_len), F32), pltpu.VMEM((rows, lk_len), BF16)],
    )
    out = pl.pallas_call(
        functools.partial(_sample_kernel, n_pages=n_pages, n_keep=n_keep, n_bits=n_bits),
        grid_spec=grid_spec,
        out_shape=jax.ShapeDtypeStruct((n_seq, ROWS_PER_HEAD, QKV_W), F32),
        compiler_params=_params(("arbitrary", "arbitrary")),
        name="sample_attention",
    )(page_table, q8, qi8, wi8, knew, vnew, kinew,
      *([cache_k] * pps), *([cache_v] * pps), *([cache_ik] * pps))
    return out[:, :n_new]


def _layer_norm(h, g, b):
    mu = jnp.mean(h, axis=-1, keepdims=True)
    d = h - mu
    var = jnp.mean(d * d, axis=-1, keepdims=True)
    return d * lax.rsqrt(var + LN_EPS) * g + b


def _merge_kernel(ya_ref, yb_ref, yc_ref, g_ref, bg_ref, wa_ref, wb_ref, wc_ref, wo_ref,
                  x_ref, lg_ref, lb_ref, o_ref, ob_ref):
    dm = x_ref.shape[1]
    merged = jnp.zeros(x_ref.shape, F32)
    for n, (y_ref, w_ref) in enumerate(((ya_ref, wa_ref), (yb_ref, wb_ref), (yc_ref, wc_ref))):
        sl = slice(n * dm, (n + 1) * dm)
        gate = jax.nn.sigmoid(g_ref[:, sl].astype(F32) + bg_ref[:, sl])
        merged = merged + gate * _dot(y_ref[...], w_ref[...])
    mix = _dot(merged.astype(BF16), wo_ref[...])
    y = _layer_norm(ALPHA * x_ref[...] + mix, lg_ref[...], lb_ref[...])
    o_ref[...] = y
    ob_ref[...] = y.astype(BF16)


def _merge(ya, yb, yc, g, bg, wa, wb, wc, wo, x, lg, lb, *, tm):
    n, dm = x.shape
    row = lambda w: pl.BlockSpec((tm, w), lambda i: (i, 0))
    full = lambda a: pl.BlockSpec(a.shape, lambda i: (0, 0))
    return pl.pallas_call(
        _merge_kernel,
        grid=(n // tm,),
        in_specs=[row(ya.shape[1]), row(yb.shape[1]), row(yc.shape[1]), row(g.shape[1]),
                  full(bg), full(wa), full(wb), full(wc), full(wo), row(dm), full(lg), full(lb)],
        out_specs=[row(dm), row(dm)],
        out_shape=[jax.ShapeDtypeStruct((n, dm), F32), jax.ShapeDtypeStruct((n, dm), BF16)],
        compiler_params=_params(("parallel",)),
        name="merge_out_ln",
    )(ya, yb, yc, g, bg, wa, wb, wc, wo, x, lg, lb)


def _route(x, xb, wr_ref, rb_ref, comb_ref):
    w_hi, w_lo = _split_bf16(wr_ref[...])
    x_lo = (x - xb.astype(F32)).astype(BF16)
    logits = _dot(xb, w_hi) + _dot(x_lo, w_hi) + _dot(xb, w_lo)
    s = jax.nn.sigmoid(logits)
    sb = s + rb_ref[...]
    s_col = [s[:, e:e + 1] for e in range(N_EXPERTS)]
    b_col = [sb[:, e:e + 1] for e in range(N_EXPERTS)]
    best_score = None
    for gi in range(N_GROUPS):
        a, b, c, d = b_col[gi * EXPERTS_PER_GROUP:(gi + 1) * EXPERTS_PER_GROUP]
        m1, n1 = jnp.maximum(a, b), jnp.minimum(a, b)
        m2, n2 = jnp.maximum(c, d), jnp.minimum(c, d)
        score = jnp.maximum(m1, m2) + jnp.maximum(jnp.minimum(m1, m2), jnp.maximum(n1, n2))
        if gi == 0:
            best_score, best = score, jnp.zeros_like(score)
        else:
            better = score > best_score
            best = jnp.where(better, float(gi), best)
            best_score = jnp.where(better, score, best_score)
    picked = []
    for gi in range(N_GROUPS):
        cols = b_col[gi * EXPERTS_PER_GROUP:(gi + 1) * EXPERTS_PER_GROUP]
        top = _top_mask(cols, 2)
        in_group = best == float(gi)
        for j in range(EXPERTS_PER_GROUP):
            picked.append(jnp.where(in_group & top[j], s_col[gi * EXPERTS_PER_GROUP + j], 0.0))
    total = picked[0]
    for c in picked[1:]:
        total = total + c
    for e in range(N_EXPERTS):
        comb_ref[e] = picked[e] / total


def _moe_kernel(x_ref, xb_ref, wr_ref, rb_ref, wg_ref, wu_ref, wd_ref, lg_ref, lb_ref,
                o_ref, ob_ref, comb_ref, acc_ref):
    e = pl.program_id(1)

    @pl.when(e == 0)
    def _():
        _route(x_ref[...], xb_ref[...], wr_ref, rb_ref, comb_ref)
        acc_ref[...] = jnp.zeros_like(acc_ref)

    xb = xb_ref[...]
    gate = _dot(xb, wg_ref[0])
    h = gate * jax.nn.sigmoid(gate) * _dot(xb, wu_ref[0]) * comb_ref[e]
    acc_ref[...] += _dot(h.astype(BF16), wd_ref[0])

    @pl.when(e == pl.num_programs(1) - 1)
    def _():
        y = _layer_norm(ALPHA * x_ref[...] + acc_ref[...], lg_ref[...], lb_ref[...])
        o_ref[...] = y
        ob_ref[...] = y.astype(BF16)


def _moe(x, xb, wr, rb, wg, wu, wd, lg, lb, *, tm):
    n, dm = x.shape
    n_e, _, de = wg.shape
    row = pl.BlockSpec((tm, dm), lambda i, e: (i, 0))
    full = lambda a: pl.BlockSpec(a.shape, lambda i, e: (0, 0))
    return pl.pallas_call(
        _moe_kernel,
        grid=(n // tm, n_e),
        in_specs=[row, row, full(wr), full(rb),
                  pl.BlockSpec((1, dm, de), lambda i, e: (e, 0, 0)),
                  pl.BlockSpec((1, dm, de), lambda i, e: (e, 0, 0)),
                  pl.BlockSpec((1, de, dm), lambda i, e: (e, 0, 0)),
                  full(lg), full(lb)],
        out_specs=[row, row],
        out_shape=[jax.ShapeDtypeStruct((n, dm), F32), jax.ShapeDtypeStruct((n, dm), BF16)],
        scratch_shapes=[pltpu.VMEM((n_e, tm, 1), F32), pltpu.VMEM((tm, dm), F32)],
        compiler_params=_params(("parallel", "arbitrary")),
        name="moe_ln",
    )(x, xb, wr, rb, wg, wu, wd, lg, lb)


def _pad_cols(a, width):
    return jnp.pad(a, ((0, 0), (0, width - a.shape[1])))


def _layer_weights(w_in_l):
    scale = HEAD_DIM ** -0.5
    cuts = [QKV_W, 2 * QKV_W, 3 * QKV_W, 3 * QKV_W + QI_W, 3 * QKV_W + QI_W + IDX_DIM,
            3 * QKV_W + QI_W + IDX_DIM + IDX_HEADS]
    wq, wk, wv, wqi, wki, wwi, wg = jnp.split(w_in_l, cuts, axis=1)
    w_tok = _pad_cols(jnp.concatenate([wk, wv, wki, wwi], axis=1), N_COLS)
    col = jnp.arange(N_COLS)
    flag_tok = ((col < QKV_W) & (col // HEAD_DIM >= H_SB)) | ((col >= N_OFF_KI) & (col < N_OFF_WI))
    w_feat = jnp.concatenate([wq * scale, wqi, wv], axis=1).T
    grp = jnp.arange(T_ROWS // HEAD_DIM)
    flag_feat = ((grp >= H_SB) & (grp < N_HEADS)) | ((grp >= T_OFF_QI // HEAD_DIM)
                                                     & (grp < T_OFF_V // HEAD_DIM))
    w_sq = jnp.concatenate([wq * scale, wqi], axis=1)
    col_s = jnp.arange(QKV_W + QI_W)
    flag_sq = (col_s // HEAD_DIM >= H_SB)
    f = lambda a: a.astype(F32)[None, :]
    return dict(w_tok=w_tok.astype(BF16), flag_tok=f(flag_tok),
                w_feat=w_feat.astype(BF16), flag_feat=flag_feat.astype(jnp.int32),
                w_sq=w_sq.astype(BF16), flag_sq=f(flag_sq), w_gate=wg.astype(BF16))


def _rope_tables(pos):
    half = HEAD_DIM // 2
    inv = ROPE_THETA ** (-jnp.arange(half, dtype=F32) / half)
    ang = pos.astype(F32)[:, None] * inv[None, :]
    cos, sin = jnp.cos(ang), jnp.sin(ang)
    return jnp.concatenate([cos, cos], axis=1), jnp.concatenate([-sin, sin], axis=1)


def kernel(x_prompt, x_sample, cache_k, cache_v, cache_idx_k, page_table, w_in, w_br_a, w_br_b,
           w_br_c, b_gate, w_out, ln1_g, ln1_b, ln2_g, ln2_b, w_router, router_bias,
           w_exp_gate, w_exp_up, w_exp_down):
    n_b, seq, dm = x_prompt.shape
    n_seq, n_new, _ = x_sample.shape
    depth = w_in.shape[0]
    n_p = n_b * seq
    n_s = n_seq * n_new
    n_pages = page_table.shape[1]
    past = n_pages * PAGE_SIZE
    assert seq % SLAB == 0 and past % MOBA_BLOCK == 0 and n_new <= ROWS_PER_HEAD
    assert n_pages % PAGES_PER_STEP == 0

    pos = jnp.concatenate([jnp.tile(jnp.arange(seq, dtype=jnp.int32), n_b),
                           jnp.tile(past + jnp.arange(n_new, dtype=jnp.int32), n_seq)])
    cos64, sin64 = _rope_tables(pos)
    cos = jnp.concatenate([cos64, cos64], axis=1)
    sin = jnp.concatenate([sin64, sin64], axis=1)
    cos_t, sin_t = cos64[:n_p].T, sin64[:n_p].T
    x = jnp.concatenate([x_prompt.reshape(n_p, dm), x_sample.reshape(n_s, dm)], axis=0)
    xb = x.astype(BF16)

    pad_rows = lambda a, n: jnp.pad(a, ((0, 0), (0, n - a.shape[1])) + ((0, 0),) * (a.ndim - 2))
    zeros_b = jnp.zeros((HEAD_DIM, dm), BF16)
    rows_p, rows_s = [], []
    for l in range(depth):
        w = _layer_weights(w_in[l])
        tok_f32, tok_bf = _proj(xb, w["w_tok"], cos, sin, w["flag_tok"], rope=True,
                                tm=TILE_TOKENS, tn=TILE_COLS, name="in_proj_token_major")
        gates = _proj(xb, w["w_gate"], cos, sin, w["flag_tok"], rope=False,
                      tm=TILE_TOKENS, tn=TILE_COLS, name="in_proj_gates")
        proj_t = _proj_t(w["flag_feat"], w["w_feat"], xb[:n_p], cos_t, sin_t,
                         tm=TILE_TOKENS, tn=TILE_ROWS_T)
        sq, _ = _proj(xb[n_p:], w["w_sq"], cos[n_p:], sin[n_p:], w["flag_sq"], rope=True,
                      tm=n_s, tn=TILE_COLS, name="in_proj_sample_q")

        k_all, v_all = tok_f32[:, :QKV_W], tok_f32[:, QKV_W:2 * QKV_W]
        ki_all = tok_f32[:, N_OFF_KI:N_OFF_WI]
        rows_p.append((k_all[:n_p].reshape(n_b, seq, N_HEADS, HEAD_DIM),
                       v_all[:n_p].reshape(n_b, seq, N_HEADS, HEAD_DIM),
                       ki_all[:n_p].reshape(n_b, seq, IDX_DIM)))
        rows_s.append((k_all[n_p:].reshape(n_seq, n_new, N_HEADS, HEAD_DIM),
                       v_all[n_p:].reshape(n_seq, n_new, N_HEADS, HEAD_DIM),
                       ki_all[n_p:].reshape(n_seq, n_new, IDX_DIM)))

        y_a = _sb_attention(proj_t, tok_bf, n_b=n_b, seq=seq)
        y_b = _dsa_attention(proj_t, tok_bf, tok_f32, n_b=n_b, seq=seq)
        kmean = _block_means(tok_f32, n_p).reshape(n_b, seq // MOBA_BLOCK, QKV_W)
        y_c = _moba_attention(proj_t, tok_bf, kmean, n_b=n_b, seq=seq)
        tok = lambda y: y.transpose(0, 2, 1).reshape(n_p, -1)

        q_s = sq[:, :QKV_W].reshape(n_seq, n_new, N_HEADS, HEAD_DIM)
        q8 = pad_rows(q_s, ROWS_PER_HEAD).transpose(0, 2, 1, 3).astype(BF16)
        qi8 = pad_rows(sq[:, QKV_W:].reshape(n_seq, n_new, QI_W), ROWS_PER_HEAD).reshape(
            n_seq, ROWS_PER_HEAD * IDX_HEADS, IDX_DIM).astype(BF16)
        tok_s = tok_f32[n_p:].reshape(n_seq, n_new, -1)
        wi8 = pad_rows(tok_s[..., N_OFF_WI:N_OFF_WI + IDX_HEADS], ROWS_PER_HEAD).reshape(
            n_seq, ROWS_PER_HEAD * IDX_HEADS, 1)
        new_heads = lambda a: pad_rows(a.reshape(n_seq, n_new, N_HEADS, HEAD_DIM),
                                       PAGE_SIZE).transpose(0, 2, 1, 3).astype(BF16)
        y_s = _sample_attention(
            l, page_table, q8, qi8, wi8, new_heads(tok_s[..., :QKV_W]),
            new_heads(tok_s[..., QKV_W:2 * QKV_W]),
            pad_rows(tok_s[..., N_OFF_KI:N_OFF_WI], PAGE_SIZE).astype(BF16),
            cache_k, cache_v, cache_idx_k, n_new=n_new)
        y_s = y_s.reshape(n_s, QKV_W).astype(BF16)

        wa_cols, wb_cols = H_SB * HEAD_DIM, (H_SB + H_DSA) * HEAD_DIM
        blk = DSA_BLOCK_HEADS * HEAD_DIM
        ya = jnp.concatenate([tok(y_a), y_s[:, :wa_cols]], axis=0)
        yb = jnp.concatenate([tok(y_b), y_s[:, wa_cols:wa_cols + blk]], axis=0)
        yc = jnp.concatenate([tok(y_c), y_s[:, QKV_W - blk:]], axis=0)
        w_b = jnp.concatenate([w_br_b[l].astype(BF16), zeros_b], axis=0)
        w_c = jnp.concatenate([zeros_b, w_br_c[l].astype(BF16)], axis=0)
        del wb_cols
        x, xb = _merge(ya, yb, yc, gates, b_gate[l][None, :], w_br_a[l].astype(BF16), w_b, w_c,
                       w_out[l].astype(BF16), x, ln1_g[l][None, :], ln1_b[l][None, :],
                       tm=TILE_TOKENS)
        x, xb = _moe(x, xb, w_router, router_bias[None, :], w_exp_gate[l].astype(BF16),
                     w_exp_up[l].astype(BF16), w_exp_down[l].astype(BF16),
                     ln2_g[l][None, :], ln2_b[l][None, :], tm=TILE_TOKENS)

    stack = lambda rows, i: jnp.stack([r[i] for r in rows], 0)
    return (x[:n_p].reshape(n_b, seq, dm), x[n_p:].reshape(n_seq, n_new, dm),
            stack(rows_p, 0), stack(rows_p, 1), stack(rows_p, 2),
            stack(rows_s, 0), stack(rows_s, 1), stack(rows_s, 2))
```

```python
import functools

import jax
import jax.numpy as jnp
from jax import lax
from jax.experimental import pallas as pl
from jax.experimental.pallas import tpu as pltpu

F32 = jnp.float32
BF16 = jnp.bfloat16

HEAD_DIM = 64
H_SB, H_DSA, H_MOBA = 6, 5, 5
N_HEADS = H_SB + H_DSA + H_MOBA
IDX_HEADS = 8
IDX_DIM = 64
DSA_TOPK = 256
MOBA_BLOCK = 256
MOBA_TOPK = 3
N_EXPERTS = 16
EXPERTS_PER_GROUP = 4
N_GROUPS = N_EXPERTS // EXPERTS_PER_GROUP
ROPE_THETA = 10000.0
LN_EPS = 1e-5
DEPTH = 2
ALPHA = (2 * DEPTH) ** 0.25
PAGE_SIZE = 128

LANES = 128
SLAB = 256
PAIR = 2 * HEAD_DIM
NEG = -1e30
INT_MIN = -2 ** 31
VMEM_LIMIT = 56 * 1024 * 1024

QKV_W = N_HEADS * HEAD_DIM
QI_W = IDX_HEADS * IDX_DIM
T_OFF_QI = QKV_W
T_OFF_V = QKV_W + QI_W
T_ROWS = 2 * QKV_W + QI_W
N_OFF_KI = 2 * QKV_W
N_OFF_WI = N_OFF_KI + IDX_DIM
TILE_TOKENS = 512
TILE_COLS = 768
N_COLS = 3 * TILE_COLS
TILE_ROWS_T = 640

_NT = (((1,), (1,)), ((), ()))


def _nt(a, b):
    return lax.dot_general(a, b, _NT, preferred_element_type=F32)


def _dot(a, b):
    return jnp.dot(a, b, preferred_element_type=F32)


def _split_bf16(x):
    hi = x.astype(BF16)
    lo = (x - hi.astype(F32)).astype(BF16)
    return hi, lo


def _softplus(z):
    return jnp.maximum(z, 0.0) + jnp.log(1.0 + jnp.exp(-jnp.abs(z)))


def _params(sem):
    return pltpu.CompilerParams(dimension_semantics=sem, vmem_limit_bytes=VMEM_LIMIT)


def _proj_kernel(x_ref, w_ref, cos_ref, sin_ref, flag_ref, o_ref, ob_ref, *, rope):
    acc = _dot(x_ref[...], w_ref[...])
    if not rope:
        ob_ref[...] = acc.astype(ob_ref.dtype)
        return
    cos = cos_ref[...]
    sin = sin_ref[...]
    lane = lax.broadcasted_iota(jnp.int32, (1, LANES), 1)
    first_half = (lane % HEAD_DIM) < (HEAD_DIM // 2)
    for c in range(acc.shape[1] // LANES):
        sl = slice(c * LANES, (c + 1) * LANES)
        a = acc[:, sl]
        sw = jnp.where(first_half, pltpu.roll(a, LANES - HEAD_DIM // 2, 1),
                       pltpu.roll(a, HEAD_DIM // 2, 1))
        r = jnp.where(flag_ref[:, sl] > 0.0, a * cos + sw * sin, a)
        o_ref[:, sl] = r
        ob_ref[:, sl] = r.astype(BF16)


def _proj(x, w, cos, sin, flag, *, rope, tm, tn, name):
    n, d = x.shape
    width = w.shape[1]
    tile = pl.BlockSpec((tm, tn), lambda j, i: (i, j))
    if rope:
        out_specs = [tile, tile]
        out_shape = [jax.ShapeDtypeStruct((n, width), F32), jax.ShapeDtypeStruct((n, width), BF16)]
        body = functools.partial(_proj_kernel, rope=True)
    else:
        out_specs = tile
        out_shape = jax.ShapeDtypeStruct((n, width), BF16)
        flag = jnp.zeros((1, width), F32)
        body = lambda x_r, w_r, c_r, s_r, f_r, ob_r: _proj_kernel(
            x_r, w_r, c_r, s_r, f_r, None, ob_r, rope=False)
    return pl.pallas_call(
        body,
        grid=(width // tn, n // tm),
        in_specs=[
            pl.BlockSpec((tm, d), lambda j, i: (i, 0)),
            pl.BlockSpec((d, tn), lambda j, i: (0, j)),
            pl.BlockSpec((tm, LANES), lambda j, i: (i, 0)),
            pl.BlockSpec((tm, LANES), lambda j, i: (i, 0)),
            pl.BlockSpec((1, tn), lambda j, i: (0, j)),
        ],
        out_specs=out_specs,
        out_shape=out_shape,
        compiler_params=_params(("parallel", "parallel")),
        name=name,
    )(x, w, cos, sin, flag)


def _proj_t_kernel(flag_ref, w_ref, x_ref, cos_ref, sin_ref, o_ref, *, tn, tm):
    j = pl.program_id(1)
    acc = _nt(w_ref[...], x_ref[...])
    cos = cos_ref[...]
    sin = sin_ref[...]
    half = HEAD_DIM // 2
    for g in range(tn // HEAD_DIM):
        a = acc[g * HEAD_DIM:(g + 1) * HEAD_DIM]
        sw = jnp.concatenate([a[half:], a[:half]], axis=0)
        r = jnp.where(flag_ref[j * (tn // HEAD_DIM) + g] > 0, a * cos + sw * sin, a).astype(BF16)
        for s in range(tm // SLAB):
            o_ref[s, g * HEAD_DIM:(g + 1) * HEAD_DIM, :] = r[:, s * SLAB:(s + 1) * SLAB]


def _proj_t(flags, w_t, x, cos_t, sin_t, *, tm, tn):
    n, d = x.shape
    rows = w_t.shape[0]
    spt = tm // SLAB
    grid_spec = pltpu.PrefetchScalarGridSpec(
        num_scalar_prefetch=1,
        grid=(n // tm, rows // tn),
        in_specs=[
            pl.BlockSpec((tn, d), lambda i, j, f: (j, 0)),
            pl.BlockSpec((tm, d), lambda i, j, f: (i, 0)),
            pl.BlockSpec((HEAD_DIM, tm), lambda i, j, f: (0, i)),
            pl.BlockSpec((HEAD_DIM, tm), lambda i, j, f: (0, i)),
        ],
        out_specs=pl.BlockSpec((spt, tn, SLAB), lambda i, j, f: (i, j, 0)),
    )
    return pl.pallas_call(
        functools.partial(_proj_t_kernel, tn=tn, tm=tm),
        grid_spec=grid_spec,
        out_shape=jax.ShapeDtypeStruct((n // SLAB, rows, SLAB), BF16),
        compiler_params=_params(("parallel", "arbitrary")),
        name="in_proj_feature_major",
    )(flags, w_t, x, cos_t, sin_t)


def _head_masks(q_pair):
    first = lax.broadcasted_iota(jnp.int32, (PAIR, 1), 0) < HEAD_DIM
    zero = jnp.zeros_like(q_pair)
    return jnp.where(first, q_pair, zero), jnp.where(first, zero, q_pair)


def _online_update(state, scores, masks, v_t):
    m, l, acc = state
    scores = [jnp.where(mk, s, NEG) for s, mk in zip(scores, masks)]
    m_new = m
    for s in scores:
        m_new = jnp.maximum(m_new, jnp.max(s, axis=0, keepdims=True))
    alpha = jnp.exp(m - m_new)
    probs = [jnp.where(mk, jnp.exp(s - m_new), 0.0) for s, mk in zip(scores, masks)]
    l = alpha * l
    for p in probs:
        l = l + jnp.sum(p, axis=0, keepdims=True)
    p_cat = probs[0] if len(probs) == 1 else jnp.concatenate(probs, axis=0)
    acc = alpha * acc + _dot(v_t, p_cat.astype(BF16))
    return m_new, l, acc


def _sb_kernel(q_ref, k_ref, v_ref, tri_ref, o_ref):
    t = SLAB
    i = pl.program_id(2)
    qm = _head_masks(q_ref[...])
    tri = tri_ref[...]
    key_i = lax.broadcasted_iota(jnp.int32, (t, t), 0)
    qry_i = lax.broadcasted_iota(jnp.int32, (t, t), 1)
    before = key_i < qry_i

    def tiles(js, carry, diag):
        chains = [(n, hh) for n in range(len(js)) for hh in range(2)]
        ks = [k_ref[pl.ds(pl.multiple_of(j * t, t), t), :] for j in js]
        vs = [v_ref[j] for j in js]
        z = {ch: _dot(ks[ch[0]], qm[ch[1]]) for ch in chains}
        sp = {ch: _softplus(z[ch]) for ch in chains}
        lk = {ch: -sp[ch] for ch in chains}
        if diag:
            lk = {ch: jnp.where(before, lk[ch], 0.0) for ch in chains}
        parts = {ch: _split_bf16(lk[ch]) for ch in chains}
        later = {ch: _dot(tri, parts[ch][0]) + _dot(tri, parts[ch][1]) for ch in chains}
        out = []
        for hh in range(2):
            c, acc = carry[2 * hh:2 * hh + 2]
            a_all = []
            for n in range(len(js)):
                ch = (n, hh)
                a = jnp.exp(z[ch] - sp[ch] + later[ch] + c)
                if diag:
                    a = jnp.where(before, a, 0.0)
                a_all.append(a.astype(BF16))
                c = c + jnp.sum(lk[ch], axis=0, keepdims=True)
            v_cat = jnp.concatenate([v[hh * HEAD_DIM:(hh + 1) * HEAD_DIM] for v in vs], axis=1)
            acc = acc + _dot(v_cat, jnp.concatenate(a_all, axis=0))
            out += [c, acc]
        return tuple(out)

    init = (jnp.zeros((1, t), F32), jnp.zeros((HEAD_DIM, t), F32)) * 2
    carry = tiles([i], init, True)
    carry = lax.fori_loop(
        0, i // 2, lambda n, ca: tiles([i - 1 - 2 * n, i - 2 - 2 * n], ca, False), carry)
    carry = lax.cond(i % 2 == 1, lambda ca: tiles([0], ca, False), lambda ca: ca, carry)
    o_ref[...] = jnp.concatenate([carry[1], carry[3]], axis=0).astype(o_ref.dtype)


def _tri_upper(n):
    r = lax.broadcasted_iota(jnp.int32, (n, n), 0)
    c = lax.broadcasted_iota(jnp.int32, (n, n), 1)
    return (c > r).astype(BF16)


def _sb_attention(proj_t, k_tok, *, n_b, seq):
    nq = seq // SLAB
    n_pairs = H_SB // 2
    v_blk0 = T_OFF_V // PAIR
    return pl.pallas_call(
        _sb_kernel,
        grid=(n_b, n_pairs, nq),
        in_specs=[
            pl.BlockSpec((None, PAIR, SLAB), lambda b, p, i: (b * nq + i, p, 0)),
            pl.BlockSpec((seq, PAIR), lambda b, p, i: (b, p)),
            pl.BlockSpec((nq, PAIR, SLAB), lambda b, p, i: (b, v_blk0 + p, 0)),
            pl.BlockSpec((SLAB, SLAB), lambda b, p, i: (0, 0)),
        ],
        out_specs=pl.BlockSpec((None, PAIR, SLAB), lambda b, p, i: (b * nq + i, p, 0)),
        out_shape=jax.ShapeDtypeStruct((n_b * nq, n_pairs * PAIR, SLAB), BF16),
        compiler_params=_params(("parallel", "parallel", "arbitrary")),
        name="sb_attention",
    )(proj_t, k_tok, proj_t, _tri_upper(SLAB))


def _sort_key(score):
    bits = lax.bitcast_convert_type(score + 0.0, jnp.int32)
    return bits ^ ((bits >> 31) & jnp.int32(0x7FFFFFFF))


def _kth_largest(count_ge, shape, k, bits_per_step=1):
    def body(it, ans):
        unit = jnp.left_shift(jnp.int32(1), 32 - bits_per_step * (it + 1))
        passed = jnp.zeros(shape, jnp.int32)
        for m in range(1, 2 ** bits_per_step):
            passed = passed + jnp.where(count_ge(ans + m * unit) >= k, 1, 0)
        return ans + passed * unit
    return lax.fori_loop(0, 32 // bits_per_step, body, jnp.full(shape, INT_MIN, jnp.int32))


def _tie_cut(count_eq_below, need, shape, n_bits):
    def body(it, a):
        cand = a + jnp.left_shift(jnp.int32(1), n_bits - 1 - it)
        return jnp.where(count_eq_below(cand) < need, cand, a)
    return lax.fori_loop(0, n_bits, body, jnp.zeros(shape, jnp.int32))


DSA_BLOCK_HEADS = 6


def _dsa_kernel(qi_ref, kiwi_ref, ki_ref, q_ref, k_ref, v_ref, o_ref,
                key_scr, m_scr, l_scr, acc_scr, *, n_keep, n_bits):
    t = SLAB
    i = pl.program_id(1)
    n_slab = i + 1
    qpos = i * t + lax.broadcasted_iota(jnp.int32, (1, t), 1)
    kiota = lax.broadcasted_iota(jnp.int32, (t, 1), 0)
    wi_t = kiwi_ref[...].T

    def scores(j, _):
        kij = ki_ref[pl.ds(pl.multiple_of(j * t, t), t), :][:, :IDX_DIM]
        score = jnp.zeros((t, t), F32)
        for h in range(IDX_HEADS):
            sc = _dot(kij, qi_ref[h * IDX_DIM:(h + 1) * IDX_DIM, :])
            score = score + wi_t[IDX_DIM + h:IDX_DIM + h + 1, :] * jnp.maximum(sc, 0.0)
        causal = (j * t + kiota) <= qpos
        key_scr[j] = _sort_key(jnp.where(causal, score, -jnp.inf))
        return 0

    lax.fori_loop(0, n_slab, scores, 0)

    def count(pred):
        def body(j, acc):
            ones = jnp.where(pred(key_scr[j], j * t + kiota), 1.0, 0.0)
            return acc + jnp.sum(ones.reshape(t // 8, 8, t), axis=0)
        acc = lax.fori_loop(0, n_slab, body, jnp.zeros((8, t), F32))
        return jnp.sum(acc, axis=0, keepdims=True)

    thr = _kth_largest(lambda cand: count(lambda key, kp: key >= cand), (1, t), float(n_keep))
    need = float(n_keep) - count(lambda key, kp: key > thr)
    n_eq = count(lambda key, kp: key == thr)
    cut = lax.cond(
        jnp.max(n_eq - need) > 0.0,
        lambda: _tie_cut(lambda cand: count(lambda key, kp: (key == thr) & (kp < cand)),
                         need, (1, t), n_bits),
        lambda: jnp.full((1, t), 2 ** n_bits, jnp.int32))

    m_scr[...] = jnp.full(m_scr.shape, NEG, F32)
    l_scr[...] = jnp.zeros(l_scr.shape, F32)
    acc_scr[...] = jnp.zeros(acc_scr.shape, F32)
    qm = []
    for pr in range(DSA_BLOCK_HEADS // 2):
        qm += list(_head_masks(q_ref[pr * PAIR:(pr + 1) * PAIR, :]))

    def attend(j, _):
        key = key_scr[j]
        kp = j * t + kiota
        sel = ((key > thr) | ((key == thr) & (kp <= cut))) & (kp <= qpos)
        start = pl.multiple_of(j * t, t)
        vj = v_ref[j]
        kj = [k_ref[pl.ds(start, t), pr * PAIR:(pr + 1) * PAIR] for pr in range((H_DSA + 1) // 2)]
        z = [_dot(kj[h // 2], qm[h]) for h in range(H_DSA)]
        for h in range(H_DSA):
            state = (m_scr[h], l_scr[h], acc_scr[h])
            m, l, acc = _online_update(state, [z[h]], [sel], vj[h * HEAD_DIM:(h + 1) * HEAD_DIM])
            m_scr[h] = m
            l_scr[h] = l
            acc_scr[h] = acc
        return 0

    lax.fori_loop(0, n_slab, attend, 0)
    for h in range(H_DSA):
        o_ref[h * HEAD_DIM:(h + 1) * HEAD_DIM, :] = (acc_scr[h] / l_scr[h]).astype(o_ref.dtype)
    o_ref[H_DSA * HEAD_DIM:, :] = jnp.zeros((HEAD_DIM, t), o_ref.dtype)


def _dsa_attention(proj_t, k_tok, kiwi_f32, *, n_b, seq):
    nq = seq // SLAB
    wide = DSA_BLOCK_HEADS * HEAD_DIM
    n_keep = min(DSA_TOPK, seq // 4)
    n_bits = max(1, (seq - 1).bit_length())
    kiwi_blk = N_OFF_KI // LANES
    return pl.pallas_call(
        functools.partial(_dsa_kernel, n_keep=n_keep, n_bits=n_bits),
        grid=(n_b, nq),
        in_specs=[
            pl.BlockSpec((None, QI_W, SLAB), lambda b, i: (b * nq + i, T_OFF_QI // QI_W, 0)),
            pl.BlockSpec((SLAB, LANES), lambda b, i: (b * nq + i, kiwi_blk)),
            pl.BlockSpec((seq, LANES), lambda b, i: (b, kiwi_blk)),
            pl.BlockSpec((None, wide, SLAB), lambda b, i: (b * nq + i, 1, 0)),
            pl.BlockSpec((seq, wide), lambda b, i: (b, 1)),
            pl.BlockSpec((nq, wide, SLAB), lambda b, i: (b, T_OFF_V // wide + 1, 0)),
        ],
        out_specs=pl.BlockSpec((None, wide, SLAB), lambda b, i: (b * nq + i, 0, 0)),
        out_shape=jax.ShapeDtypeStruct((n_b * nq, wide, SLAB), BF16),
        scratch_shapes=[pltpu.VMEM((nq, SLAB, SLAB), jnp.int32),
                        pltpu.VMEM((H_DSA, 1, SLAB), F32), pltpu.VMEM((H_DSA, 1, SLAB), F32),
                        pltpu.VMEM((H_DSA, HEAD_DIM, SLAB), F32)],
        compiler_params=_params(("parallel", "arbitrary")),
        name="dsa_attention",
    )(proj_t, kiwi_f32, k_tok, proj_t, k_tok, proj_t)


def _block_mean_kernel(k_ref, o_ref):
    o_ref[0] = jnp.mean(k_ref[:, :QKV_W], axis=0, keepdims=True)


def _block_means(k_tok_f32, n_tokens):
    nb = n_tokens // MOBA_BLOCK
    return pl.pallas_call(
        _block_mean_kernel,
        grid=(nb,),
        in_specs=[pl.BlockSpec((MOBA_BLOCK, QKV_W), lambda n: (n, 0))],
        out_specs=pl.BlockSpec((1, 1, QKV_W), lambda n: (n, 0, 0)),
        out_shape=jax.ShapeDtypeStruct((nb, 1, QKV_W), F32),
        compiler_params=_params(("parallel",)),
        name="moba_block_means",
    )(k_tok_f32)


MOBA_FIRST_PAIR = (H_SB + H_DSA) // 2


def _moba_kernel(q_ref, k_ref, v_ref, km_ref, o_ref, sel_scr, *, nb):
    t = MOBA_BLOCK
    i = pl.program_id(2)
    qm = _head_masks(q_ref[...])
    km_hi, km_lo = _split_bf16(km_ref[...])
    n_iota = lax.broadcasted_iota(jnp.int32, (nb, t), 0)
    fi = n_iota.astype(F32)
    valid = n_iota < i
    for hh in range(2):
        gm = jnp.where(valid, _dot(km_hi, qm[hh]) + _dot(km_lo, qm[hh]), -jnp.inf)
        sel = jnp.zeros((nb, t), jnp.bool_)
        for _ in range(min(MOBA_TOPK, nb)):
            m = jnp.max(gm, axis=0, keepdims=True)
            first = jnp.min(jnp.where(gm == m, fi, float(nb)), axis=0, keepdims=True)
            pick = fi == first
            sel = sel | pick
            gm = jnp.where(pick, -jnp.inf, gm)
        sel_scr[hh] = jnp.where(sel & valid, 1.0, 0.0)

    def blocks(ns, carry, mask_of):
        ks = [k_ref[pl.ds(pl.multiple_of(n * t, t), t), :] for n in ns]
        vs = [v_ref[n] for n in ns]
        z = [[_dot(kj, qm[hh]) for kj in ks] for hh in range(2)]
        out = []
        for hh in range(2):
            v_cat = jnp.concatenate([v[hh * HEAD_DIM:(hh + 1) * HEAD_DIM] for v in vs], axis=1)
            out += list(_online_update(carry[3 * hh:3 * hh + 3], z[hh],
                                       [mask_of(hh, n) for n in ns], v_cat))
        return tuple(out)

    key_i = lax.broadcasted_iota(jnp.int32, (t, t), 0)
    qry_i = lax.broadcasted_iota(jnp.int32, (t, t), 1)
    chosen = lambda hh, n: sel_scr[hh, pl.ds(n, 1), :] > 0.0
    init = (jnp.full((1, t), NEG, F32), jnp.zeros((1, t), F32), jnp.zeros((HEAD_DIM, t), F32)) * 2
    carry = blocks([i], init, lambda hh, n: key_i <= qry_i)
    carry = lax.fori_loop(0, i // 2, lambda n, ca: blocks([2 * n, 2 * n + 1], ca, chosen), carry)
    carry = lax.cond(i % 2 == 1, lambda ca: blocks([i - 1], ca, chosen), lambda ca: ca, carry)
    o_ref[...] = jnp.concatenate([carry[2] / carry[1], carry[5] / carry[4]],
                                 axis=0).astype(o_ref.dtype)


def _moba_attention(proj_t, k_tok, kmean, *, n_b, seq):
    nq = seq // SLAB
    nb = seq // MOBA_BLOCK
    n_pairs = N_HEADS // 2 - MOBA_FIRST_PAIR
    p0 = MOBA_FIRST_PAIR
    v_blk0 = T_OFF_V // PAIR + p0
    return pl.pallas_call(
        functools.partial(_moba_kernel, nb=nb),
        grid=(n_b, n_pairs, nq),
        in_specs=[
            pl.BlockSpec((None, PAIR, SLAB), lambda b, p, i: (b * nq + i, p0 + p, 0)),
            pl.BlockSpec((seq, PAIR), lambda b, p, i: (b, p0 + p)),
            pl.BlockSpec((nq, PAIR, SLAB), lambda b, p, i: (b, v_blk0 + p, 0)),
            pl.BlockSpec((None, nb, PAIR), lambda b, p, i: (b, 0, p0 + p)),
        ],
        out_specs=pl.BlockSpec((None, PAIR, SLAB), lambda b, p, i: (b * nq + i, p, 0)),
        out_shape=jax.ShapeDtypeStruct((n_b * nq, n_pairs * PAIR, SLAB), BF16),
        scratch_shapes=[pltpu.VMEM((2, nb, SLAB), F32)],
        compiler_params=_params(("parallel", "parallel", "arbitrary")),
        name="moba_attention",
    )(proj_t, k_tok, proj_t, kmean)


ROWS_PER_HEAD = 8


def _top_mask(cols, n_top):
    out = []
    for n, g in enumerate(cols):
        rank = jnp.zeros_like(g)
        for m, o in enumerate(cols):
            if m == n:
                continue
            ahead = (o >= g) if m < n else (o > g)
            rank = rank + jnp.where(ahead, 1.0, 0.0)
        out.append(rank < float(n_top))
    return out


def _sample_mixers(s_scr, p_scr, ki_scr, qi_ref, wi_ref, *, past, n_keep, n_bits):
    lk_len = past + PAGE_SIZE
    r8 = ROWS_PER_HEAD
    kpos = lax.broadcasted_iota(jnp.int32, (1, lk_len), 1)

    def qpos(rows):
        return past + lax.broadcasted_iota(jnp.int32, (rows, 1), 0) % r8

    ra = H_SB * r8
    z = s_scr[:ra, :]
    before = kpos < qpos(ra)
    sp = _softplus(z)
    lk = jnp.where(before, -sp, 0.0)
    hi, lo = _split_bf16(lk)
    tri_r = lax.broadcasted_iota(jnp.int32, (LANES, LANES), 0)
    tri_c = lax.broadcasted_iota(jnp.int32, (LANES, LANES), 1)
    tri = (tri_r > tri_c).astype(BF16)
    c = jnp.zeros((ra, 1), F32)
    for blk in reversed(range(lk_len // LANES)):
        sl = slice(blk * LANES, (blk + 1) * LANES)
        later = _dot(hi[:, sl], tri) + _dot(lo[:, sl], tri) + c
        a = jnp.where(before[:, sl], jnp.exp(z[:, sl] - sp[:, sl] + later), 0.0)
        p_scr[:ra, sl] = a.astype(BF16)
        c = c + jnp.sum(lk[:, sl], axis=1, keepdims=True)

    sc = jnp.maximum(_dot(qi_ref[0], ki_scr[...]), 0.0) * wi_ref[0]
    score = jnp.sum(sc.reshape(r8, IDX_HEADS, lk_len), axis=1)
    causal8 = kpos <= qpos(r8)
    key = _sort_key(jnp.where(causal8, score, -jnp.inf))

    def count(pred):
        return jnp.sum(jnp.where(pred, 1.0, 0.0), axis=1, keepdims=True)

    thr = _kth_largest(lambda cand: count(key >= cand), (r8, 1), float(n_keep), bits_per_step=2)
    need = float(n_keep) - count(key > thr)
    n_eq = count(key == thr)
    cut = lax.cond(
        jnp.max(n_eq - need) > 0.0,
        lambda: _tie_cut(lambda cand: count((key == thr) & (kpos < cand)), need, (r8, 1), n_bits),
        lambda: jnp.full((r8, 1), 2 ** n_bits, jnp.int32))
    sel8 = ((key > thr) | ((key == thr) & (kpos <= cut))) & causal8
    rb = H_DSA * r8
    sel = jnp.concatenate([jnp.where(sel8, 1.0, 0.0)] * H_DSA, axis=0) > 0.0
    s = jnp.where(sel, s_scr[ra:ra + rb, :], NEG)
    p = jnp.where(sel, jnp.exp(s - jnp.max(s, axis=1, keepdims=True)), 0.0)
    l_b = jnp.sum(p, axis=1, keepdims=True)
    p_scr[ra:ra + rb, :] = p.astype(BF16)

    rc = H_MOBA * r8
    sm = s_scr[ra + rb:, :]
    n_blk = past // MOBA_BLOCK
    gate = [jnp.sum(sm[:, n * MOBA_BLOCK:(n + 1) * MOBA_BLOCK], axis=1, keepdims=True)
            for n in range(n_blk)]
    chosen = _top_mask(gate, min(MOBA_TOPK, n_blk + 1))
    own_ok = kpos[:, past:] <= qpos(rc)
    pieces = [jnp.where(chosen[n], sm[:, n * MOBA_BLOCK:(n + 1) * MOBA_BLOCK], NEG)
              for n in range(n_blk)]
    pieces.append(jnp.where(own_ok, sm[:, past:], NEG))
    m_c = pieces[0].max(axis=1, keepdims=True)
    for pc in pieces[1:]:
        m_c = jnp.maximum(m_c, pc.max(axis=1, keepdims=True))
    l_c = jnp.zeros((rc, 1), F32)
    off = 0
    for pc in pieces:
        e = jnp.where(pc > 0.5 * NEG, jnp.exp(pc - m_c), 0.0)
        l_c = l_c + jnp.sum(e, axis=1, keepdims=True)
        p_scr[ra + rb:, off:off + pc.shape[1]] = e.astype(BF16)
        off += pc.shape[1]

    return jnp.concatenate([jnp.ones((ra, 1), F32), 1.0 / l_b, 1.0 / l_c], axis=0)


def _sample_kernel(pt_ref, q_ref, qi_ref, wi_ref, knew_ref, vnew_ref, kinew_ref, *rest,
                   n_pages, n_keep, n_bits):
    del pt_ref
    k_pages, v_pages = rest[:n_pages], rest[n_pages:2 * n_pages]
    i_pages = rest[2 * n_pages:3 * n_pages]
    o_ref = rest[3 * n_pages]
    k_scr, v_scr, ki_scr, s_scr, p_scr = rest[3 * n_pages + 1:]
    past = n_pages * PAGE_SIZE
    r8 = ROWS_PER_HEAD

    for pg in range(n_pages):
        cols = slice(pg * PAGE_SIZE, (pg + 1) * PAGE_SIZE)
        k_scr[:, :, cols] = k_pages[pg][...].astype(BF16)
        v_scr[:, :, cols] = v_pages[pg][...].astype(BF16)
        ki_scr[:, cols] = i_pages[pg][...].astype(BF16)
    k_scr[:, :, past:] = knew_ref[0]
    v_scr[:, :, past:] = vnew_ref[0]
    ki_scr[:, past:] = kinew_ref[0]
    for h in range(N_HEADS):
        s_scr[h * r8:(h + 1) * r8, :] = _dot(q_ref[0, h], k_scr[h])
    inv = _sample_mixers(s_scr, p_scr, ki_scr, qi_ref, wi_ref, past=past, n_keep=n_keep,
                         n_bits=n_bits)
    for h in range(N_HEADS):
        rows = slice(h * r8, (h + 1) * r8)
        o_ref[0, :, h * HEAD_DIM:(h + 1) * HEAD_DIM] = _nt(p_scr[rows, :], v_scr[h]) * inv[rows]


def _sample_attention(layer, page_table, q8, qi8, wi8, knew, vnew, kinew,
                      cache_k, cache_v, cache_ik, *, n_new):
    n_seq, n_pages = page_table.shape
    past = n_pages * PAGE_SIZE
    lk_len = past + PAGE_SIZE
    n_keep = min(DSA_TOPK, (past + n_new) // 4)
    n_bits = max(1, (lk_len - 1).bit_length())
    rows = N_HEADS * ROWS_PER_HEAD

    def page_spec(j, *dims):
        return pl.BlockSpec((None, None) + dims + (PAGE_SIZE,),
                            lambda b, pt: (layer, pt[b, j]) + (0,) * (len(dims) + 1))

    def seq_spec(*dims):
        return pl.BlockSpec((1,) + dims, lambda b, pt: (b,) + (0,) * len(dims))

    in_specs = [seq_spec(N_HEADS, ROWS_PER_HEAD, HEAD_DIM),
                seq_spec(ROWS_PER_HEAD * IDX_HEADS, IDX_DIM),
                seq_spec(ROWS_PER_HEAD * IDX_HEADS, 1),
                seq_spec(N_HEADS, HEAD_DIM, PAGE_SIZE), seq_spec(N_HEADS, HEAD_DIM, PAGE_SIZE),
                seq_spec(IDX_DIM, PAGE_SIZE)]
    in_specs += [page_spec(j, N_HEADS, HEAD_DIM) for j in range(n_pages)]
    in_specs += [page_spec(j, N_HEADS, HEAD_DIM) for j in range(n_pages)]
    in_specs += [page_spec(j, IDX_DIM) for j in range(n_pages)]
    grid_spec = pltpu.PrefetchScalarGridSpec(
        num_scalar_prefetch=1,
        grid=(n_seq,),
        in_specs=in_specs,
        out_specs=pl.BlockSpec((1, ROWS_PER_HEAD, QKV_W), lambda b, pt: (b, 0, 0)),
        scratch_shapes=[pltpu.VMEM((N_HEADS, HEAD_DIM, lk_len), BF16),
                        pltpu.VMEM((N_HEADS, HEAD_DIM, lk_len), BF16),
                        pltpu.VMEM((IDX_DIM, lk_len), BF16),
                        pltpu.VMEM((rows, lk_len), F32), pltpu.VMEM((rows, lk_len), BF16)],
    )
    out = pl.pallas_call(
        functools.partial(_sample_kernel, n_pages=n_pages, n_keep=n_keep, n_bits=n_bits),
        grid_spec=grid_spec,
        out_shape=jax.ShapeDtypeStruct((n_seq, ROWS_PER_HEAD, QKV_W), F32),
        compiler_params=_params(("arbitrary",)),
        name="sample_attention",
    )(page_table, q8, qi8, wi8, knew, vnew, kinew,
      *([cache_k] * n_pages), *([cache_v] * n_pages), *([cache_ik] * n_pages))
    return out[:, :n_new]


def _layer_norm(h, g, b):
    mu = jnp.mean(h, axis=-1, keepdims=True)
    d = h - mu
    var = jnp.mean(d * d, axis=-1, keepdims=True)
    return d * lax.rsqrt(var + LN_EPS) * g + b


def _merge_kernel(ya_ref, yb_ref, yc_ref, g_ref, bg_ref, wa_ref, wb_ref, wc_ref, wo_ref,
                  x_ref, lg_ref, lb_ref, o_ref, ob_ref):
    dm = x_ref.shape[1]
    merged = jnp.zeros(x_ref.shape, F32)
    for n, (y_ref, w_ref) in enumerate(((ya_ref, wa_ref), (yb_ref, wb_ref), (yc_ref, wc_ref))):
        sl = slice(n * dm, (n + 1) * dm)
        gate = jax.nn.sigmoid(g_ref[:, sl].astype(F32) + bg_ref[:, sl])
        merged = merged + gate * _dot(y_ref[...], w_ref[...])
    mix = _dot(merged.astype(BF16), wo_ref[...])
    y = _layer_norm(ALPHA * x_ref[...] + mix, lg_ref[...], lb_ref[...])
    o_ref[...] = y
    ob_ref[...] = y.astype(BF16)


def _merge(ya, yb, yc, g, bg, wa, wb, wc, wo, x, lg, lb, *, tm):
    n, dm = x.shape
    row = lambda w: pl.BlockSpec((tm, w), lambda i: (i, 0))
    full = lambda a: pl.BlockSpec(a.shape, lambda i: (0, 0))
    return pl.pallas_call(
        _merge_kernel,
        grid=(n // tm,),
        in_specs=[row(ya.shape[1]), row(yb.shape[1]), row(yc.shape[1]), row(g.shape[1]),
                  full(bg), full(wa), full(wb), full(wc), full(wo), row(dm), full(lg), full(lb)],
        out_specs=[row(dm), row(dm)],
        out_shape=[jax.ShapeDtypeStruct((n, dm), F32), jax.ShapeDtypeStruct((n, dm), BF16)],
        compiler_params=_params(("parallel",)),
        name="merge_out_ln",
    )(ya, yb, yc, g, bg, wa, wb, wc, wo, x, lg, lb)


def _route(x, xb, wr_ref, rb_ref, comb_ref):
    w_hi, w_lo = _split_bf16(wr_ref[...])
    x_lo = (x - xb.astype(F32)).astype(BF16)
    logits = _dot(xb, w_hi) + _dot(x_lo, w_hi) + _dot(xb, w_lo)
    s = jax.nn.sigmoid(logits)
    sb = s + rb_ref[...]
    s_col = [s[:, e:e + 1] for e in range(N_EXPERTS)]
    b_col = [sb[:, e:e + 1] for e in range(N_EXPERTS)]
    best_score = None
    for gi in range(N_GROUPS):
        a, b, c, d = b_col[gi * EXPERTS_PER_GROUP:(gi + 1) * EXPERTS_PER_GROUP]
        m1, n1 = jnp.maximum(a, b), jnp.minimum(a, b)
        m2, n2 = jnp.maximum(c, d), jnp.minimum(c, d)
        score = jnp.maximum(m1, m2) + jnp.maximum(jnp.minimum(m1, m2), jnp.maximum(n1, n2))
        if gi == 0:
            best_score, best = score, jnp.zeros_like(score)
        else:
            better = score > best_score
            best = jnp.where(better, float(gi), best)
            best_score = jnp.where(better, score, best_score)
    picked = []
    for gi in range(N_GROUPS):
        cols = b_col[gi * EXPERTS_PER_GROUP:(gi + 1) * EXPERTS_PER_GROUP]
        top = _top_mask(cols, 2)
        in_group = best == float(gi)
        for j in range(EXPERTS_PER_GROUP):
            picked.append(jnp.where(in_group & top[j], s_col[gi * EXPERTS_PER_GROUP + j], 0.0))
    total = picked[0]
    for c in picked[1:]:
        total = total + c
    for e in range(N_EXPERTS):
        comb_ref[e] = picked[e] / total


def _moe_kernel(x_ref, xb_ref, wr_ref, rb_ref, wg_ref, wu_ref, wd_ref, lg_ref, lb_ref,
                o_ref, ob_ref, comb_ref, acc_ref):
    e = pl.program_id(1)

    @pl.when(e == 0)
    def _():
        _route(x_ref[...], xb_ref[...], wr_ref, rb_ref, comb_ref)
        acc_ref[...] = jnp.zeros_like(acc_ref)

    xb = xb_ref[...]
    gate = _dot(xb, wg_ref[0])
    h = gate * jax.nn.sigmoid(gate) * _dot(xb, wu_ref[0]) * comb_ref[e]
    acc_ref[...] += _dot(h.astype(BF16), wd_ref[0])

    @pl.when(e == pl.num_programs(1) - 1)
    def _():
        y = _layer_norm(ALPHA * x_ref[...] + acc_ref[...], lg_ref[...], lb_ref[...])
        o_ref[...] = y
        ob_ref[...] = y.astype(BF16)


def _moe(x, xb, wr, rb, wg, wu, wd, lg, lb, *, tm):
    n, dm = x.shape
    n_e, _, de = wg.shape
    row = pl.BlockSpec((tm, dm), lambda i, e: (i, 0))
    full = lambda a: pl.BlockSpec(a.shape, lambda i, e: (0, 0))
    return pl.pallas_call(
        _moe_kernel,
        grid=(n // tm, n_e),
        in_specs=[row, row, full(wr), full(rb),
                  pl.BlockSpec((1, dm, de), lambda i, e: (e, 0, 0)),
                  pl.BlockSpec((1, dm, de), lambda i, e: (e, 0, 0)),
                  pl.BlockSpec((1, de, dm), lambda i, e: (e, 0, 0)),
                  full(lg), full(lb)],
        out_specs=[row, row],
        out_shape=[jax.ShapeDtypeStruct((n, dm), F32), jax.ShapeDtypeStruct((n, dm), BF16)],
        scratch_shapes=[pltpu.VMEM((n_e, tm, 1), F32), pltpu.VMEM((tm, dm), F32)],
        compiler_params=_params(("parallel", "arbitrary")),
        name="moe_ln",
    )(x, xb, wr, rb, wg, wu, wd, lg, lb)


def _pad_cols(a, width):
    return jnp.pad(a, ((0, 0), (0, width - a.shape[1])))


def _layer_weights(w_in_l):
    scale = HEAD_DIM ** -0.5
    cuts = [QKV_W, 2 * QKV_W, 3 * QKV_W, 3 * QKV_W + QI_W, 3 * QKV_W + QI_W + IDX_DIM,
            3 * QKV_W + QI_W + IDX_DIM + IDX_HEADS]
    wq, wk, wv, wqi, wki, wwi, wg = jnp.split(w_in_l, cuts, axis=1)
    w_tok = _pad_cols(jnp.concatenate([wk, wv, wki, wwi], axis=1), N_COLS)
    col = jnp.arange(N_COLS)
    flag_tok = ((col < QKV_W) & (col // HEAD_DIM >= H_SB)) | ((col >= N_OFF_KI) & (col < N_OFF_WI))
    w_feat = jnp.concatenate([wq * scale, wqi, wv], axis=1).T
    grp = jnp.arange(T_ROWS // HEAD_DIM)
    flag_feat = ((grp >= H_SB) & (grp < N_HEADS)) | ((grp >= T_OFF_QI // HEAD_DIM)
                                                     & (grp < T_OFF_V // HEAD_DIM))
    w_sq = jnp.concatenate([wq * scale, wqi], axis=1)
    col_s = jnp.arange(QKV_W + QI_W)
    flag_sq = (col_s // HEAD_DIM >= H_SB)
    f = lambda a: a.astype(F32)[None, :]
    return dict(w_tok=w_tok.astype(BF16), flag_tok=f(flag_tok),
                w_feat=w_feat.astype(BF16), flag_feat=flag_feat.astype(jnp.int32),
                w_sq=w_sq.astype(BF16), flag_sq=f(flag_sq), w_gate=wg.astype(BF16))


def _rope_tables(pos):
    half = HEAD_DIM // 2
    inv = ROPE_THETA ** (-jnp.arange(half, dtype=F32) / half)
    ang = pos.astype(F32)[:, None] * inv[None, :]
    cos, sin = jnp.cos(ang), jnp.sin(ang)
    return jnp.concatenate([cos, cos], axis=1), jnp.concatenate([-sin, sin], axis=1)


def kernel(x_prompt, x_sample, cache_k, cache_v, cache_idx_k, page_table, w_in, w_br_a, w_br_b,
           w_br_c, b_gate, w_out, ln1_g, ln1_b, ln2_g, ln2_b, w_router, router_bias,
           w_exp_gate, w_exp_up, w_exp_down):
    n_b, seq, dm = x_prompt.shape
    n_seq, n_new, _ = x_sample.shape
    depth = w_in.shape[0]
    n_p = n_b * seq
    n_s = n_seq * n_new
    n_pages = page_table.shape[1]
    past = n_pages * PAGE_SIZE
    assert seq % SLAB == 0 and past % MOBA_BLOCK == 0 and n_new <= ROWS_PER_HEAD
    ck_t = cache_k.transpose(0, 1, 3, 4, 2)
    cv_t = cache_v.transpose(0, 1, 3, 4, 2)
    cik_t = cache_idx_k.transpose(0, 1, 3, 2)

    pos = jnp.concatenate([jnp.tile(jnp.arange(seq, dtype=jnp.int32), n_b),
                           jnp.tile(past + jnp.arange(n_new, dtype=jnp.int32), n_seq)])
    cos64, sin64 = _rope_tables(pos)
    cos = jnp.concatenate([cos64, cos64], axis=1)
    sin = jnp.concatenate([sin64, sin64], axis=1)
    cos_t, sin_t = cos64[:n_p].T, sin64[:n_p].T
    x = jnp.concatenate([x_prompt.reshape(n_p, dm), x_sample.reshape(n_s, dm)], axis=0)
    xb = x.astype(BF16)

    pad_rows = lambda a, n: jnp.pad(a, ((0, 0), (0, n - a.shape[1])) + ((0, 0),) * (a.ndim - 2))
    zeros_b = jnp.zeros((HEAD_DIM, dm), BF16)
    rows_p, rows_s = [], []
    for l in range(depth):
        w = _layer_weights(w_in[l])
        tok_f32, tok_bf = _proj(xb, w["w_tok"], cos, sin, w["flag_tok"], rope=True,
                                tm=TILE_TOKENS, tn=TILE_COLS, name="in_proj_token_major")
        gates = _proj(xb, w["w_gate"], cos, sin, w["flag_tok"], rope=False,
                      tm=TILE_TOKENS, tn=TILE_COLS, name="in_proj_gates")
        proj_t = _proj_t(w["flag_feat"], w["w_feat"], xb[:n_p], cos_t, sin_t,
                         tm=TILE_TOKENS, tn=TILE_ROWS_T)
        sq, _ = _proj(xb[n_p:], w["w_sq"], cos[n_p:], sin[n_p:], w["flag_sq"], rope=True,
                      tm=n_s, tn=TILE_COLS, name="in_proj_sample_q")

        k_all, v_all = tok_f32[:, :QKV_W], tok_f32[:, QKV_W:2 * QKV_W]
        ki_all = tok_f32[:, N_OFF_KI:N_OFF_WI]
        rows_p.append((k_all[:n_p].reshape(n_b, seq, N_HEADS, HEAD_DIM),
                       v_all[:n_p].reshape(n_b, seq, N_HEADS, HEAD_DIM),
                       ki_all[:n_p].reshape(n_b, seq, IDX_DIM)))
        rows_s.append((k_all[n_p:].reshape(n_seq, n_new, N_HEADS, HEAD_DIM),
                       v_all[n_p:].reshape(n_seq, n_new, N_HEADS, HEAD_DIM),
                       ki_all[n_p:].reshape(n_seq, n_new, IDX_DIM)))

        y_a = _sb_attention(proj_t, tok_bf, n_b=n_b, seq=seq)
        y_b = _dsa_attention(proj_t, tok_bf, tok_f32, n_b=n_b, seq=seq)
        kmean = _block_means(tok_f32, n_p).reshape(n_b, seq // MOBA_BLOCK, QKV_W)
        y_c = _moba_attention(proj_t, tok_bf, kmean, n_b=n_b, seq=seq)
        tok = lambda y: y.transpose(0, 2, 1).reshape(n_p, -1)

        q_s = sq[:, :QKV_W].reshape(n_seq, n_new, N_HEADS, HEAD_DIM)
        q8 = pad_rows(q_s, ROWS_PER_HEAD).transpose(0, 2, 1, 3).astype(BF16)
        qi8 = pad_rows(sq[:, QKV_W:].reshape(n_seq, n_new, QI_W), ROWS_PER_HEAD).reshape(
            n_seq, ROWS_PER_HEAD * IDX_HEADS, IDX_DIM).astype(BF16)
        tok_s = tok_f32[n_p:].reshape(n_seq, n_new, -1)
        wi8 = pad_rows(tok_s[..., N_OFF_WI:N_OFF_WI + IDX_HEADS], ROWS_PER_HEAD).reshape(
            n_seq, ROWS_PER_HEAD * IDX_HEADS, 1)
        new_heads = lambda a: pad_rows(a.reshape(n_seq, n_new, N_HEADS, HEAD_DIM),
                                       PAGE_SIZE).transpose(0, 2, 3, 1).astype(BF16)
        y_s = _sample_attention(
            l, page_table, q8, qi8, wi8, new_heads(tok_s[..., :QKV_W]),
            new_heads(tok_s[..., QKV_W:2 * QKV_W]),
            pad_rows(tok_s[..., N_OFF_KI:N_OFF_WI], PAGE_SIZE).transpose(0, 2, 1).astype(BF16),
            ck_t, cv_t, cik_t, n_new=n_new)
        y_s = y_s.reshape(n_s, QKV_W).astype(BF16)

        wa_cols, wb_cols = H_SB * HEAD_DIM, (H_SB + H_DSA) * HEAD_DIM
        blk = DSA_BLOCK_HEADS * HEAD_DIM
        ya = jnp.concatenate([tok(y_a), y_s[:, :wa_cols]], axis=0)
        yb = jnp.concatenate([tok(y_b), y_s[:, wa_cols:wa_cols + blk]], axis=0)
        yc = jnp.concatenate([tok(y_c), y_s[:, QKV_W - blk:]], axis=0)
        w_b = jnp.concatenate([w_br_b[l].astype(BF16), zeros_b], axis=0)
        w_c = jnp.concatenate([zeros_b, w_br_c[l].astype(BF16)], axis=0)
        del wb_cols
        x, xb = _merge(ya, yb, yc, gates, b_gate[l][None, :], w_br_a[l].astype(BF16), w_b, w_c,
                       w_out[l].astype(BF16), x, ln1_g[l][None, :], ln1_b[l][None, :],
                       tm=TILE_TOKENS)
        x, xb = _moe(x, xb, w_router, router_bias[None, :], w_exp_gate[l].astype(BF16),
                     w_exp_up[l].astype(BF16), w_exp_down[l].astype(BF16),
                     ln2_g[l][None, :], ln2_b[l][None, :], tm=TILE_TOKENS)

    stack = lambda rows, i: jnp.stack([r[i] for r in rows], 0)
    return (x[:n_p].reshape(n_b, seq, dm), x[n_p:].reshape(n_seq, n_new, dm),
            stack(rows_p, 0), stack(rows_p, 1), stack(rows_p, 2),
            stack(rows_s, 0), stack(rows_s, 1), stack(rows_s, 2))
```

```python
import functools

import jax
import jax.numpy as jnp
from jax import lax
from jax.experimental import pallas as pl
from jax.experimental.pallas import tpu as pltpu

F32 = jnp.float32
BF16 = jnp.bfloat16

HEAD_DIM = 64
H_SB, H_DSA, H_MOBA = 6, 5, 5
N_HEADS = H_SB + H_DSA + H_MOBA
IDX_HEADS = 8
IDX_DIM = 64
DSA_TOPK = 256
MOBA_BLOCK = 256
MOBA_TOPK = 3
N_EXPERTS = 16
EXPERTS_PER_GROUP = 4
N_GROUPS = N_EXPERTS // EXPERTS_PER_GROUP
ROPE_THETA = 10000.0
LN_EPS = 1e-5
DEPTH = 2
ALPHA = (2 * DEPTH) ** 0.25
PAGE_SIZE = 128

LANES = 128
SLAB = 256
PAIR = 2 * HEAD_DIM
NEG = -1e30
INT_MIN = -2 ** 31
VMEM_LIMIT = 56 * 1024 * 1024

QKV_W = N_HEADS * HEAD_DIM
QI_W = IDX_HEADS * IDX_DIM
T_OFF_QI = QKV_W
T_OFF_V = QKV_W + QI_W
T_ROWS = 2 * QKV_W + QI_W
N_OFF_KI = QKV_W
N_OFF_WI = N_OFF_KI + IDX_DIM
N_COLS = QKV_W + LANES
TILE_TOKENS = 512
TILE_COLS = 768
TILE_COLS_TOK = N_COLS // 3
TILE_ROWS_T = 640

_NT = (((1,), (1,)), ((), ()))


def _nt(a, b):
    return lax.dot_general(a, b, _NT, preferred_element_type=F32)


def _dot(a, b):
    return jnp.dot(a, b, preferred_element_type=F32)


def _split_bf16(x):
    hi = x.astype(BF16)
    lo = (x - hi.astype(F32)).astype(BF16)
    return hi, lo


def _softplus(z):
    return jnp.maximum(z, 0.0) + jnp.log(1.0 + jnp.exp(-jnp.abs(z)))


def _params(sem):
    return pltpu.CompilerParams(dimension_semantics=sem, vmem_limit_bytes=VMEM_LIMIT)


def _proj_kernel(x_ref, w_ref, cos_ref, sin_ref, flag_ref, o_ref, ob_ref, *, rope):
    acc = _dot(x_ref[...], w_ref[...])
    if not rope:
        ob_ref[...] = acc.astype(ob_ref.dtype)
        return
    cos = cos_ref[...]
    sin = sin_ref[...]
    lane = lax.broadcasted_iota(jnp.int32, (1, LANES), 1)
    first_half = (lane % HEAD_DIM) < (HEAD_DIM // 2)
    for c in range(acc.shape[1] // LANES):
        sl = slice(c * LANES, (c + 1) * LANES)
        a = acc[:, sl]
        sw = jnp.where(first_half, pltpu.roll(a, LANES - HEAD_DIM // 2, 1),
                       pltpu.roll(a, HEAD_DIM // 2, 1))
        r = jnp.where(flag_ref[:, sl] > 0.0, a * cos + sw * sin, a)
        o_ref[:, sl] = r
        ob_ref[:, sl] = r.astype(BF16)


def _proj(x, w, cos, sin, flag, *, rope, tm, tn, name):
    n, d = x.shape
    width = w.shape[1]
    tile = pl.BlockSpec((tm, tn), lambda j, i: (i, j))
    if rope:
        out_specs = [tile, tile]
        out_shape = [jax.ShapeDtypeStruct((n, width), F32), jax.ShapeDtypeStruct((n, width), BF16)]
        body = functools.partial(_proj_kernel, rope=True)
    else:
        out_specs = tile
        out_shape = jax.ShapeDtypeStruct((n, width), BF16)
        flag = jnp.zeros((1, width), F32)
        body = lambda x_r, w_r, c_r, s_r, f_r, ob_r: _proj_kernel(
            x_r, w_r, c_r, s_r, f_r, None, ob_r, rope=False)
    return pl.pallas_call(
        body,
        grid=(width // tn, n // tm),
        in_specs=[
            pl.BlockSpec((tm, d), lambda j, i: (i, 0)),
            pl.BlockSpec((d, tn), lambda j, i: (0, j)),
            pl.BlockSpec((tm, LANES), lambda j, i: (i, 0)),
            pl.BlockSpec((tm, LANES), lambda j, i: (i, 0)),
            pl.BlockSpec((1, tn), lambda j, i: (0, j)),
        ],
        out_specs=out_specs,
        out_shape=out_shape,
        compiler_params=_params(("parallel", "parallel")),
        name=name,
    )(x, w, cos, sin, flag)


def _proj_t_kernel(flag_ref, w_ref, x_ref, cos_ref, sin_ref, o_ref, *, tn, tm, slabs):
    j = pl.program_id(1)
    acc = _nt(w_ref[...], x_ref[...])
    cos = cos_ref[...]
    sin = sin_ref[...]
    half = HEAD_DIM // 2
    for g in range(tn // HEAD_DIM):
        rows = slice(g * HEAD_DIM, (g + 1) * HEAD_DIM)
        a = acc[rows]
        sw = jnp.concatenate([a[half:], a[:half]], axis=0)
        r = jnp.where(flag_ref[j * (tn // HEAD_DIM) + g] > 0, a * cos + sw * sin, a)
        if slabs:
            for s in range(tm // SLAB):
                o_ref[s, rows, :] = r[:, s * SLAB:(s + 1) * SLAB].astype(o_ref.dtype)
        else:
            o_ref[rows, :] = r


def _proj_t(flags, w_t, x, cos_t, sin_t, *, tm, tn, seq=None):
    n, d = x.shape
    rows = w_t.shape[0]
    if seq is None:
        out_spec = pl.BlockSpec((tm // SLAB, tn, SLAB), lambda i, j, f: (i, j, 0))
        out_shape = jax.ShapeDtypeStruct((n // SLAB, rows, SLAB), BF16)
    else:
        per_seq = seq // tm
        out_spec = pl.BlockSpec((None, tn, tm), lambda i, j, f: (i // per_seq, j, i % per_seq))
        out_shape = jax.ShapeDtypeStruct((n // seq, rows, seq), F32)
    grid_spec = pltpu.PrefetchScalarGridSpec(
        num_scalar_prefetch=1,
        grid=(n // tm, rows // tn),
        in_specs=[
            pl.BlockSpec((tn, d), lambda i, j, f: (j, 0)),
            pl.BlockSpec((tm, d), lambda i, j, f: (i, 0)),
            pl.BlockSpec((HEAD_DIM, tm), lambda i, j, f: (0, i)),
            pl.BlockSpec((HEAD_DIM, tm), lambda i, j, f: (0, i)),
        ],
        out_specs=out_spec,
    )
    return pl.pallas_call(
        functools.partial(_proj_t_kernel, tn=tn, tm=tm, slabs=seq is None),
        grid_spec=grid_spec,
        out_shape=out_shape,
        compiler_params=_params(("parallel", "arbitrary")),
        name="in_proj_feature_major" if seq is None else "in_proj_output_rows",
    )(flags, w_t, x, cos_t, sin_t)


def _head_masks(q_pair):
    first = lax.broadcasted_iota(jnp.int32, (PAIR, 1), 0) < HEAD_DIM
    zero = jnp.zeros_like(q_pair)
    return jnp.where(first, q_pair, zero), jnp.where(first, zero, q_pair)


def _online_update(state, scores, masks, v_t):
    m, l, acc = state
    scores = [jnp.where(mk, s, NEG) for s, mk in zip(scores, masks)]
    m_new = m
    for s in scores:
        m_new = jnp.maximum(m_new, jnp.max(s, axis=0, keepdims=True))
    alpha = jnp.exp(m - m_new)
    probs = [jnp.exp(s - m_new) for s in scores]
    l = alpha * l
    for p in probs:
        l = l + jnp.sum(p, axis=0, keepdims=True)
    p_cat = probs[0] if len(probs) == 1 else jnp.concatenate(probs, axis=0)
    acc = alpha * acc + _dot(v_t, p_cat.astype(BF16))
    return m_new, l, acc


def _sb_kernel(q_ref, k_ref, v_ref, tri_ref, o_ref):
    t = SLAB
    i = pl.program_id(2)
    qm = _head_masks(q_ref[...])
    tri = tri_ref[...]
    key_i = lax.broadcasted_iota(jnp.int32, (t, t), 0)
    qry_i = lax.broadcasted_iota(jnp.int32, (t, t), 1)
    before = key_i < qry_i

    def tiles(js, carry, diag):
        chains = [(n, hh) for n in range(len(js)) for hh in range(2)]
        ks = [k_ref[pl.ds(pl.multiple_of(j * t, t), t), :] for j in js]
        vs = [v_ref[j] for j in js]
        z = {ch: _dot(ks[ch[0]], qm[ch[1]]) for ch in chains}
        sp = {ch: _softplus(z[ch]) for ch in chains}
        lk = {ch: -sp[ch] for ch in chains}
        if diag:
            lk = {ch: jnp.where(before, lk[ch], 0.0) for ch in chains}
        parts = {ch: _split_bf16(lk[ch]) for ch in chains}
        later = {ch: _dot(tri, parts[ch][0]) + _dot(tri, parts[ch][1]) for ch in chains}
        out = []
        for hh in range(2):
            c, acc = carry[2 * hh:2 * hh + 2]
            a_all = []
            for n in range(len(js)):
                ch = (n, hh)
                a = jnp.exp(z[ch] - sp[ch] + later[ch] + c)
                if diag:
                    a = jnp.where(before, a, 0.0)
                a_all.append(a.astype(BF16))
                c = c + jnp.sum(lk[ch], axis=0, keepdims=True)
            v_cat = jnp.concatenate([v[hh * HEAD_DIM:(hh + 1) * HEAD_DIM] for v in vs], axis=1)
            acc = acc + _dot(v_cat, jnp.concatenate(a_all, axis=0))
            out += [c, acc]
        return tuple(out)

    init = (jnp.zeros((1, t), F32), jnp.zeros((HEAD_DIM, t), F32)) * 2
    carry = tiles([i], init, True)
    carry = lax.fori_loop(
        0, i // 2, lambda n, ca: tiles([i - 1 - 2 * n, i - 2 - 2 * n], ca, False), carry)
    carry = lax.cond(i % 2 == 1, lambda ca: tiles([0], ca, False), lambda ca: ca, carry)
    o_ref[...] = jnp.concatenate([carry[1], carry[3]], axis=0).astype(o_ref.dtype)


def _tri_upper(n):
    r = lax.broadcasted_iota(jnp.int32, (n, n), 0)
    c = lax.broadcasted_iota(jnp.int32, (n, n), 1)
    return (c > r).astype(BF16)


def _sb_attention(proj_t, k_tok, *, n_b, seq):
    nq = seq // SLAB
    n_pairs = H_SB // 2
    v_blk0 = T_OFF_V // PAIR
    return pl.pallas_call(
        _sb_kernel,
        grid=(n_b, n_pairs, nq),
        in_specs=[
            pl.BlockSpec((None, PAIR, SLAB), lambda b, p, i: (b * nq + i, p, 0)),
            pl.BlockSpec((seq, PAIR), lambda b, p, i: (b, p)),
            pl.BlockSpec((nq, PAIR, SLAB), lambda b, p, i: (b, v_blk0 + p, 0)),
            pl.BlockSpec((SLAB, SLAB), lambda b, p, i: (0, 0)),
        ],
        out_specs=pl.BlockSpec((None, PAIR, SLAB), lambda b, p, i: (b * nq + i, p, 0)),
        out_shape=jax.ShapeDtypeStruct((n_b * nq, n_pairs * PAIR, SLAB), BF16),
        compiler_params=_params(("parallel", "parallel", "arbitrary")),
        name="sb_attention",
    )(proj_t, k_tok, proj_t, _tri_upper(SLAB))


def _sort_key(score):
    bits = lax.bitcast_convert_type(score + 0.0, jnp.int32)
    return bits ^ ((bits >> 31) & jnp.int32(0x7FFFFFFF))


_NEG_INF_KEY = -2139095041


def _kth_largest(count_ge, shape, k, bits_per_step=1):
    def body(it, ans):
        unit = jnp.left_shift(jnp.int32(1), 32 - bits_per_step * (it + 1))
        passed = jnp.zeros(shape, jnp.int32)
        for m in range(1, 2 ** bits_per_step):
            passed = passed + jnp.where(count_ge(ans + m * unit) >= k, 1, 0)
        return ans + passed * unit
    return lax.fori_loop(0, 32 // bits_per_step, body, jnp.full(shape, INT_MIN, jnp.int32))


def _tie_cut(count_eq_below, need, shape, n_bits):
    def body(it, a):
        cand = a + jnp.left_shift(jnp.int32(1), n_bits - 1 - it)
        return jnp.where(count_eq_below(cand) < need, cand, a)
    return lax.fori_loop(0, n_bits, body, jnp.zeros(shape, jnp.int32))


DSA_BLOCK_HEADS = 6


def _dsa_kernel(qi_ref, kiwi_ref, ki_ref, q_ref, k_ref, v_ref, o_ref,
                key_scr, m_scr, l_scr, acc_scr, *, nq, n_keep, n_bits):
    t = SLAB
    i = pl.program_id(1)
    n_slab = i + 1
    qpos = i * t + lax.broadcasted_iota(jnp.int32, (1, t), 1)
    kiota = lax.broadcasted_iota(jnp.int32, (t, 1), 0)
    wi_t = kiwi_ref[...].T

    def scores(j, _):
        kij = ki_ref[pl.ds(pl.multiple_of(j * t, t), t), :][:, :IDX_DIM]
        score = jnp.zeros((t, t), F32)
        for h in range(IDX_HEADS):
            sc = _dot(kij, qi_ref[h * IDX_DIM:(h + 1) * IDX_DIM, :])
            score = score + wi_t[IDX_DIM + h:IDX_DIM + h + 1, :] * jnp.maximum(sc, 0.0)
        causal = (j * t + kiota) <= qpos
        key_scr[j] = _sort_key(jnp.where(causal, score, -jnp.inf))
        return 0

    lax.fori_loop(0, n_slab, scores, 0)

    n_pair = (n_slab + 1) // 2

    @pl.when(n_slab % 2 == 1)
    def _():
        key_scr[n_slab] = jnp.full((t, t), _NEG_INF_KEY, jnp.int32)

    def count(pred):
        def body(n, acc):
            for j in (2 * n, 2 * n + 1):
                ones = jnp.where(pred(key_scr[j], j * t + kiota), 1.0, 0.0)
                acc = acc + jnp.sum(ones.reshape(t // 8, 8, t), axis=0)
            return acc
        acc = lax.fori_loop(0, n_pair, body, jnp.zeros((8, t), F32))
        return jnp.sum(acc, axis=0, keepdims=True)

    thr = _kth_largest(lambda cand: count(lambda key, kp: key >= cand), (1, t), float(n_keep))
    need = float(n_keep) - count(lambda key, kp: key > thr)
    n_eq = count(lambda key, kp: key == thr)
    cut = lax.cond(
        jnp.max(n_eq - need) > 0.0,
        lambda: _tie_cut(lambda cand: count(lambda key, kp: (key == thr) & (kp < cand)),
                         need, (1, t), n_bits),
        lambda: jnp.full((1, t), 2 ** n_bits, jnp.int32))

    m_scr[...] = jnp.full(m_scr.shape, NEG, F32)
    l_scr[...] = jnp.zeros(l_scr.shape, F32)
    acc_scr[...] = jnp.zeros(acc_scr.shape, F32)
    qm = []
    for pr in range(DSA_BLOCK_HEADS // 2):
        qm += list(_head_masks(q_ref[pr * PAIR:(pr + 1) * PAIR, :]))

    def attend(n, _):
        sel, z, vs = [], [], []
        for j in (2 * n, 2 * n + 1):
            key = key_scr[j]
            kp = j * t + kiota
            sel.append(((key > thr) | ((key == thr) & (kp <= cut))) & (kp <= qpos))
            jk = jnp.minimum(j, nq - 1)
            start = pl.multiple_of(jk * t, t)
            vs.append(v_ref[jk])
            kj = [k_ref[pl.ds(start, t), pr * PAIR:(pr + 1) * PAIR]
                  for pr in range((H_DSA + 1) // 2)]
            z.append([_dot(kj[h // 2], qm[h]) for h in range(H_DSA)])
        for h in range(H_DSA):
            state = (m_scr[h], l_scr[h], acc_scr[h])
            v_cat = jnp.concatenate([v[h * HEAD_DIM:(h + 1) * HEAD_DIM] for v in vs], axis=1)
            m, l, acc = _online_update(state, [z[0][h], z[1][h]], sel, v_cat)
            m_scr[h] = m
            l_scr[h] = l
            acc_scr[h] = acc
        return 0

    lax.fori_loop(0, n_pair, attend, 0)
    for h in range(H_DSA):
        o_ref[h * HEAD_DIM:(h + 1) * HEAD_DIM, :] = (acc_scr[h] / l_scr[h]).astype(o_ref.dtype)
    o_ref[H_DSA * HEAD_DIM:, :] = jnp.zeros((HEAD_DIM, t), o_ref.dtype)


def _dsa_attention(proj_t, k_tok, kiwi_f32, *, n_b, seq):
    nq = seq // SLAB
    wide = DSA_BLOCK_HEADS * HEAD_DIM
    n_keep = min(DSA_TOPK, seq // 4)
    n_bits = max(1, (seq - 1).bit_length())
    kiwi_blk = N_OFF_KI // LANES
    return pl.pallas_call(
        functools.partial(_dsa_kernel, nq=nq, n_keep=n_keep, n_bits=n_bits),
        grid=(n_b, nq),
        in_specs=[
            pl.BlockSpec((None, QI_W, SLAB), lambda b, i: (b * nq + i, T_OFF_QI // QI_W, 0)),
            pl.BlockSpec((SLAB, LANES), lambda b, i: (b * nq + i, kiwi_blk)),
            pl.BlockSpec((seq, LANES), lambda b, i: (b, kiwi_blk)),
            pl.BlockSpec((None, wide, SLAB), lambda b, i: (b * nq + i, 1, 0)),
            pl.BlockSpec((seq, wide), lambda b, i: (b, 1)),
            pl.BlockSpec((nq, wide, SLAB), lambda b, i: (b, T_OFF_V // wide + 1, 0)),
        ],
        out_specs=pl.BlockSpec((None, wide, SLAB), lambda b, i: (b * nq + i, 0, 0)),
        out_shape=jax.ShapeDtypeStruct((n_b * nq, wide, SLAB), BF16),
        scratch_shapes=[pltpu.VMEM((nq + 1, SLAB, SLAB), jnp.int32),
                        pltpu.VMEM((H_DSA, 1, SLAB), F32), pltpu.VMEM((H_DSA, 1, SLAB), F32),
                        pltpu.VMEM((H_DSA, HEAD_DIM, SLAB), F32)],
        compiler_params=_params(("parallel", "arbitrary")),
        name="dsa_attention",
    )(proj_t, kiwi_f32, k_tok, proj_t, k_tok, proj_t)


def _block_mean_kernel(k_ref, o_ref):
    o_ref[0] = jnp.mean(k_ref[:, :QKV_W], axis=0, keepdims=True)


def _block_means(k_tok_f32, n_tokens):
    nb = n_tokens // MOBA_BLOCK
    return pl.pallas_call(
        _block_mean_kernel,
        grid=(nb,),
        in_specs=[pl.BlockSpec((MOBA_BLOCK, QKV_W), lambda n: (n, 0))],
        out_specs=pl.BlockSpec((1, 1, QKV_W), lambda n: (n, 0, 0)),
        out_shape=jax.ShapeDtypeStruct((nb, 1, QKV_W), F32),
        compiler_params=_params(("parallel",)),
        name="moba_block_means",
    )(k_tok_f32)


MOBA_FIRST_PAIR = (H_SB + H_DSA) // 2


def _moba_kernel(q_ref, k_ref, v_ref, km_ref, o_ref, sel_scr, *, nb):
    t = MOBA_BLOCK
    i = pl.program_id(2)
    qm = _head_masks(q_ref[...])
    km_hi, km_lo = _split_bf16(km_ref[...])
    n_iota = lax.broadcasted_iota(jnp.int32, (nb, t), 0)
    fi = n_iota.astype(F32)
    valid = n_iota < i
    for hh in range(2):
        gm = jnp.where(valid, _dot(km_hi, qm[hh]) + _dot(km_lo, qm[hh]), -jnp.inf)
        sel = jnp.zeros((nb, t), jnp.bool_)
        for _ in range(min(MOBA_TOPK, nb)):
            m = jnp.max(gm, axis=0, keepdims=True)
            first = jnp.min(jnp.where(gm == m, fi, float(nb)), axis=0, keepdims=True)
            pick = fi == first
            sel = sel | pick
            gm = jnp.where(pick, -jnp.inf, gm)
        sel_scr[hh] = jnp.where(sel & valid, 1.0, 0.0)

    def blocks(ns, carry, mask_of):
        ks = [k_ref[pl.ds(pl.multiple_of(n * t, t), t), :] for n in ns]
        vs = [v_ref[n] for n in ns]
        z = [[_dot(kj, qm[hh]) for kj in ks] for hh in range(2)]
        out = []
        for hh in range(2):
            v_cat = jnp.concatenate([v[hh * HEAD_DIM:(hh + 1) * HEAD_DIM] for v in vs], axis=1)
            out += list(_online_update(carry[3 * hh:3 * hh + 3], z[hh],
                                       [mask_of(hh, n) for n in ns], v_cat))
        return tuple(out)

    key_i = lax.broadcasted_iota(jnp.int32, (t, t), 0)
    qry_i = lax.broadcasted_iota(jnp.int32, (t, t), 1)
    chosen = lambda hh, n: sel_scr[hh, pl.ds(n, 1), :] > 0.0
    init = (jnp.full((1, t), NEG, F32), jnp.zeros((1, t), F32), jnp.zeros((HEAD_DIM, t), F32)) * 2
    carry = blocks([i], init, lambda hh, n: key_i <= qry_i)
    carry = lax.fori_loop(0, i // 2, lambda n, ca: blocks([2 * n, 2 * n + 1], ca, chosen), carry)
    carry = lax.cond(i % 2 == 1, lambda ca: blocks([i - 1], ca, chosen), lambda ca: ca, carry)
    o_ref[...] = jnp.concatenate([carry[2] / carry[1], carry[5] / carry[4]],
                                 axis=0).astype(o_ref.dtype)


def _moba_attention(proj_t, k_tok, kmean, *, n_b, seq):
    nq = seq // SLAB
    nb = seq // MOBA_BLOCK
    n_pairs = N_HEADS // 2 - MOBA_FIRST_PAIR
    p0 = MOBA_FIRST_PAIR
    v_blk0 = T_OFF_V // PAIR + p0
    return pl.pallas_call(
        functools.partial(_moba_kernel, nb=nb),
        grid=(n_b, n_pairs, nq),
        in_specs=[
            pl.BlockSpec((None, PAIR, SLAB), lambda b, p, i: (b * nq + i, p0 + p, 0)),
            pl.BlockSpec((seq, PAIR), lambda b, p, i: (b, p0 + p)),
            pl.BlockSpec((nq, PAIR, SLAB), lambda b, p, i: (b, v_blk0 + p, 0)),
            pl.BlockSpec((None, nb, PAIR), lambda b, p, i: (b, 0, p0 + p)),
        ],
        out_specs=pl.BlockSpec((None, PAIR, SLAB), lambda b, p, i: (b * nq + i, p, 0)),
        out_shape=jax.ShapeDtypeStruct((n_b * nq, n_pairs * PAIR, SLAB), BF16),
        scratch_shapes=[pltpu.VMEM((2, nb, SLAB), F32)],
        compiler_params=_params(("parallel", "parallel", "arbitrary")),
        name="moba_attention",
    )(proj_t, k_tok, proj_t, kmean)


ROWS_PER_HEAD = 8


def _top_mask(cols, n_top):
    out = []
    for n, g in enumerate(cols):
        rank = jnp.zeros_like(g)
        for m, o in enumerate(cols):
            if m == n:
                continue
            ahead = (o >= g) if m < n else (o > g)
            rank = rank + jnp.where(ahead, 1.0, 0.0)
        out.append(rank < float(n_top))
    return out


def _sample_mixers(s_scr, p_scr, ki_scr, qi_ref, wi_ref, *, past, n_keep, n_bits):
    lk_len = past + PAGE_SIZE
    r8 = ROWS_PER_HEAD
    kpos = lax.broadcasted_iota(jnp.int32, (1, lk_len), 1)

    def qpos(rows):
        return past + lax.broadcasted_iota(jnp.int32, (rows, 1), 0) % r8

    ra = H_SB * r8
    z = s_scr[:ra, :]
    before = kpos < qpos(ra)
    sp = _softplus(z)
    lk = jnp.where(before, -sp, 0.0)
    hi, lo = _split_bf16(lk)
    tri_r = lax.broadcasted_iota(jnp.int32, (LANES, LANES), 0)
    tri_c = lax.broadcasted_iota(jnp.int32, (LANES, LANES), 1)
    tri = (tri_r > tri_c).astype(BF16)
    c = jnp.zeros((ra, 1), F32)
    for blk in reversed(range(lk_len // LANES)):
        sl = slice(blk * LANES, (blk + 1) * LANES)
        later = _dot(hi[:, sl], tri) + _dot(lo[:, sl], tri) + c
        a = jnp.where(before[:, sl], jnp.exp(z[:, sl] - sp[:, sl] + later), 0.0)
        p_scr[:ra, sl] = a.astype(BF16)
        c = c + jnp.sum(lk[:, sl], axis=1, keepdims=True)

    sc = jnp.maximum(_dot(qi_ref[0], ki_scr[...]), 0.0) * wi_ref[0]
    score = jnp.sum(sc.reshape(r8, IDX_HEADS, lk_len), axis=1)
    causal8 = kpos <= qpos(r8)
    key = _sort_key(jnp.where(causal8, score, -jnp.inf))

    def count(pred):
        return jnp.sum(jnp.where(pred, 1.0, 0.0), axis=1, keepdims=True)

    thr = _kth_largest(lambda cand: count(key >= cand), (r8, 1), float(n_keep), bits_per_step=2)
    need = float(n_keep) - count(key > thr)
    n_eq = count(key == thr)
    cut = lax.cond(
        jnp.max(n_eq - need) > 0.0,
        lambda: _tie_cut(lambda cand: count((key == thr) & (kpos < cand)), need, (r8, 1), n_bits),
        lambda: jnp.full((r8, 1), 2 ** n_bits, jnp.int32))
    sel8 = ((key > thr) | ((key == thr) & (kpos <= cut))) & causal8
    rb = H_DSA * r8
    sel = jnp.concatenate([jnp.where(sel8, 1.0, 0.0)] * H_DSA, axis=0) > 0.0
    s = jnp.where(sel, s_scr[ra:ra + rb, :], NEG)
    p = jnp.where(sel, jnp.exp(s - jnp.max(s, axis=1, keepdims=True)), 0.0)
    l_b = jnp.sum(p, axis=1, keepdims=True)
    p_scr[ra:ra + rb, :] = p.astype(BF16)

    rc = H_MOBA * r8
    sm = s_scr[ra + rb:, :]
    n_blk = past // MOBA_BLOCK
    gate = [jnp.sum(sm[:, n * MOBA_BLOCK:(n + 1) * MOBA_BLOCK], axis=1, keepdims=True)
            for n in range(n_blk)]
    chosen = _top_mask(gate, min(MOBA_TOPK, n_blk + 1))
    own_ok = kpos[:, past:] <= qpos(rc)
    pieces = [jnp.where(chosen[n], sm[:, n * MOBA_BLOCK:(n + 1) * MOBA_BLOCK], NEG)
              for n in range(n_blk)]
    pieces.append(jnp.where(own_ok, sm[:, past:], NEG))
    m_c = pieces[0].max(axis=1, keepdims=True)
    for pc in pieces[1:]:
        m_c = jnp.maximum(m_c, pc.max(axis=1, keepdims=True))
    l_c = jnp.zeros((rc, 1), F32)
    off = 0
    for pc in pieces:
        e = jnp.where(pc > 0.5 * NEG, jnp.exp(pc - m_c), 0.0)
        l_c = l_c + jnp.sum(e, axis=1, keepdims=True)
        p_scr[ra + rb:, off:off + pc.shape[1]] = e.astype(BF16)
        off += pc.shape[1]

    return jnp.concatenate([jnp.ones((ra, 1), F32), 1.0 / l_b, 1.0 / l_c], axis=0)


def _sample_kernel(pt_ref, q_ref, qi_ref, wi_ref, knew_ref, vnew_ref, kinew_ref, *rest,
                   n_pages, n_keep, n_bits):
    del pt_ref
    k_pages, v_pages = rest[:n_pages], rest[n_pages:2 * n_pages]
    i_pages = rest[2 * n_pages:3 * n_pages]
    o_ref = rest[3 * n_pages]
    k_scr, v_scr, ki_scr, s_scr, p_scr = rest[3 * n_pages + 1:]
    past = n_pages * PAGE_SIZE
    r8 = ROWS_PER_HEAD

    for pg in range(n_pages):
        cols = slice(pg * PAGE_SIZE, (pg + 1) * PAGE_SIZE)
        k_scr[:, :, cols] = k_pages[pg][...].astype(BF16)
        v_scr[:, :, cols] = v_pages[pg][...].astype(BF16)
        ki_scr[:, cols] = i_pages[pg][...].astype(BF16)
    k_scr[:, :, past:] = knew_ref[0]
    v_scr[:, :, past:] = vnew_ref[0]
    ki_scr[:, past:] = kinew_ref[0]
    for h in range(N_HEADS):
        s_scr[h * r8:(h + 1) * r8, :] = _dot(q_ref[0, h], k_scr[h])
    inv = _sample_mixers(s_scr, p_scr, ki_scr, qi_ref, wi_ref, past=past, n_keep=n_keep,
                         n_bits=n_bits)
    for h in range(N_HEADS):
        rows = slice(h * r8, (h + 1) * r8)
        o_ref[0, :, h * HEAD_DIM:(h + 1) * HEAD_DIM] = _nt(p_scr[rows, :], v_scr[h]) * inv[rows]


def _sample_attention(layer, page_table, q8, qi8, wi8, knew, vnew, kinew,
                      cache_k, cache_v, cache_ik, *, n_new):
    n_seq, n_pages = page_table.shape
    past = n_pages * PAGE_SIZE
    lk_len = past + PAGE_SIZE
    n_keep = min(DSA_TOPK, (past + n_new) // 4)
    n_bits = max(1, (lk_len - 1).bit_length())
    rows = N_HEADS * ROWS_PER_HEAD

    def page_spec(j, *dims):
        return pl.BlockSpec((None, None) + dims + (PAGE_SIZE,),
                            lambda b, pt: (layer, pt[b, j]) + (0,) * (len(dims) + 1))

    def seq_spec(*dims):
        return pl.BlockSpec((1,) + dims, lambda b, pt: (b,) + (0,) * len(dims))

    in_specs = [seq_spec(N_HEADS, ROWS_PER_HEAD, HEAD_DIM),
                seq_spec(ROWS_PER_HEAD * IDX_HEADS, IDX_DIM),
                seq_spec(ROWS_PER_HEAD * IDX_HEADS, 1),
                seq_spec(N_HEADS, HEAD_DIM, PAGE_SIZE), seq_spec(N_HEADS, HEAD_DIM, PAGE_SIZE),
                seq_spec(IDX_DIM, PAGE_SIZE)]
    in_specs += [page_spec(j, N_HEADS, HEAD_DIM) for j in range(n_pages)]
    in_specs += [page_spec(j, N_HEADS, HEAD_DIM) for j in range(n_pages)]
    in_specs += [page_spec(j, IDX_DIM) for j in range(n_pages)]
    grid_spec = pltpu.PrefetchScalarGridSpec(
        num_scalar_prefetch=1,
        grid=(n_seq,),
        in_specs=in_specs,
        out_specs=pl.BlockSpec((1, ROWS_PER_HEAD, QKV_W), lambda b, pt: (b, 0, 0)),
        scratch_shapes=[pltpu.VMEM((N_HEADS, HEAD_DIM, lk_len), BF16),
                        pltpu.VMEM((N_HEADS, HEAD_DIM, lk_len), BF16),
                        pltpu.VMEM((IDX_DIM, lk_len), BF16),
                        pltpu.VMEM((rows, lk_len), F32), pltpu.VMEM((rows, lk_len), BF16)],
    )
    out = pl.pallas_call(
        functools.partial(_sample_kernel, n_pages=n_pages, n_keep=n_keep, n_bits=n_bits),
        grid_spec=grid_spec,
        out_shape=jax.ShapeDtypeStruct((n_seq, ROWS_PER_HEAD, QKV_W), F32),
        compiler_params=_params(("arbitrary",)),
        name="sample_attention",
    )(page_table, q8, qi8, wi8, knew, vnew, kinew,
      *([cache_k] * n_pages), *([cache_v] * n_pages), *([cache_ik] * n_pages))
    return out[:, :n_new]


def _layer_norm(h, g, b):
    mu = jnp.mean(h, axis=-1, keepdims=True)
    d = h - mu
    var = jnp.mean(d * d, axis=-1, keepdims=True)
    return d * lax.rsqrt(var + LN_EPS) * g + b


def _merge_kernel(ya_ref, yb_ref, yc_ref, g_ref, bg_ref, wa_ref, wb_ref, wc_ref, wo_ref,
                  x_ref, lg_ref, lb_ref, o_ref, ob_ref):
    dm = x_ref.shape[1]
    merged = jnp.zeros(x_ref.shape, F32)
    for n, (y_ref, w_ref) in enumerate(((ya_ref, wa_ref), (yb_ref, wb_ref), (yc_ref, wc_ref))):
        sl = slice(n * dm, (n + 1) * dm)
        gate = jax.nn.sigmoid(g_ref[:, sl].astype(F32) + bg_ref[:, sl])
        merged = merged + gate * _dot(y_ref[...], w_ref[...])
    mix = _dot(merged.astype(BF16), wo_ref[...])
    y = _layer_norm(ALPHA * x_ref[...] + mix, lg_ref[...], lb_ref[...])
    o_ref[...] = y
    ob_ref[...] = y.astype(BF16)


def _merge(ya, yb, yc, g, bg, wa, wb, wc, wo, x, lg, lb, *, tm):
    n, dm = x.shape
    row = lambda w: pl.BlockSpec((tm, w), lambda i: (i, 0))
    full = lambda a: pl.BlockSpec(a.shape, lambda i: (0, 0))
    return pl.pallas_call(
        _merge_kernel,
        grid=(n // tm,),
        in_specs=[row(ya.shape[1]), row(yb.shape[1]), row(yc.shape[1]), row(g.shape[1]),
                  full(bg), full(wa), full(wb), full(wc), full(wo), row(dm), full(lg), full(lb)],
        out_specs=[row(dm), row(dm)],
        out_shape=[jax.ShapeDtypeStruct((n, dm), F32), jax.ShapeDtypeStruct((n, dm), BF16)],
        compiler_params=_params(("parallel",)),
        name="merge_out_ln",
    )(ya, yb, yc, g, bg, wa, wb, wc, wo, x, lg, lb)


def _route(x, xb, wr_ref, rb_ref, comb_ref):
    w_hi, w_lo = _split_bf16(wr_ref[...])
    x_lo = (x - xb.astype(F32)).astype(BF16)
    logits = _nt(w_hi, xb) + _nt(w_hi, x_lo) + _nt(w_lo, xb)
    s = jax.nn.sigmoid(logits)
    sb = s + rb_ref[...]
    s_col = [s[e:e + 1, :] for e in range(N_EXPERTS)]
    b_col = [sb[e:e + 1, :] for e in range(N_EXPERTS)]
    best_score = None
    for gi in range(N_GROUPS):
        a, b, c, d = b_col[gi * EXPERTS_PER_GROUP:(gi + 1) * EXPERTS_PER_GROUP]
        m1, n1 = jnp.maximum(a, b), jnp.minimum(a, b)
        m2, n2 = jnp.maximum(c, d), jnp.minimum(c, d)
        score = jnp.maximum(m1, m2) + jnp.maximum(jnp.minimum(m1, m2), jnp.maximum(n1, n2))
        if gi == 0:
            best_score, best = score, jnp.zeros_like(score)
        else:
            better = score > best_score
            best = jnp.where(better, float(gi), best)
            best_score = jnp.where(better, score, best_score)
    picked = []
    for gi in range(N_GROUPS):
        cols = b_col[gi * EXPERTS_PER_GROUP:(gi + 1) * EXPERTS_PER_GROUP]
        top = _top_mask(cols, 2)
        in_group = best == float(gi)
        for j in range(EXPERTS_PER_GROUP):
            picked.append(jnp.where(in_group & top[j], s_col[gi * EXPERTS_PER_GROUP + j], 0.0))
    total = picked[0]
    for c in picked[1:]:
        total = total + c
    pad = jnp.zeros((LANES - N_EXPERTS, total.shape[1]), F32)
    comb_ref[...] = jnp.concatenate([p / total for p in picked] + [pad], axis=0).T


def _moe_kernel(x_ref, xb_ref, wr_ref, rb_ref, wg_ref, wu_ref, wd_ref, lg_ref, lb_ref,
                o_ref, ob_ref, comb_ref, acc_ref):
    e = pl.program_id(1)

    @pl.when(e == 0)
    def _():
        _route(x_ref[...], xb_ref[...], wr_ref, rb_ref, comb_ref)
        acc_ref[...] = jnp.zeros_like(acc_ref)

    xb = xb_ref[...]
    gate = _dot(xb, wg_ref[0])
    lane = lax.broadcasted_iota(jnp.int32, comb_ref.shape, 1)
    weight = jnp.sum(jnp.where(lane == e, comb_ref[...], 0.0), axis=1, keepdims=True)
    h = gate * jax.nn.sigmoid(gate) * _dot(xb, wu_ref[0]) * weight
    acc_ref[...] += _dot(h.astype(BF16), wd_ref[0])

    @pl.when(e == pl.num_programs(1) - 1)
    def _():
        y = _layer_norm(ALPHA * x_ref[...] + acc_ref[...], lg_ref[...], lb_ref[...])
        o_ref[...] = y
        ob_ref[...] = y.astype(BF16)


def _moe(x, xb, wr, rb, wg, wu, wd, lg, lb, *, tm):
    n, dm = x.shape
    n_e, _, de = wg.shape
    row = pl.BlockSpec((tm, dm), lambda i, e: (i, 0))
    full = lambda a: pl.BlockSpec(a.shape, lambda i, e: (0, 0))
    return pl.pallas_call(
        _moe_kernel,
        grid=(n // tm, n_e),
        in_specs=[row, row, full(wr), full(rb),
                  pl.BlockSpec((1, dm, de), lambda i, e: (e, 0, 0)),
                  pl.BlockSpec((1, dm, de), lambda i, e: (e, 0, 0)),
                  pl.BlockSpec((1, de, dm), lambda i, e: (e, 0, 0)),
                  full(lg), full(lb)],
        out_specs=[row, row],
        out_shape=[jax.ShapeDtypeStruct((n, dm), F32), jax.ShapeDtypeStruct((n, dm), BF16)],
        scratch_shapes=[pltpu.VMEM((tm, LANES), F32), pltpu.VMEM((tm, dm), F32)],
        compiler_params=_params(("parallel", "arbitrary")),
        name="moe_ln",
    )(x, xb, wr, rb, wg, wu, wd, lg, lb)


def _pad_cols(a, width):
    return jnp.pad(a, ((0, 0), (0, width - a.shape[1])))


def _layer_weights(w_in_l):
    scale = HEAD_DIM ** -0.5
    cuts = [QKV_W, 2 * QKV_W, 3 * QKV_W, 3 * QKV_W + QI_W, 3 * QKV_W + QI_W + IDX_DIM,
            3 * QKV_W + QI_W + IDX_DIM + IDX_HEADS]
    wq, wk, wv, wqi, wki, wwi, wg = jnp.split(w_in_l, cuts, axis=1)
    w_tok = _pad_cols(jnp.concatenate([wk, wki, wwi], axis=1), N_COLS)
    col = jnp.arange(N_COLS)
    flag_tok = ((col < QKV_W) & (col // HEAD_DIM >= H_SB)) | ((col >= N_OFF_KI) & (col < N_OFF_WI))
    w_feat = jnp.concatenate([wq * scale, wqi, wv], axis=1).T
    grp = jnp.arange(T_ROWS // HEAD_DIM)
    flag_feat = ((grp >= H_SB) & (grp < N_HEADS)) | ((grp >= T_OFF_QI // HEAD_DIM)
                                                     & (grp < T_OFF_V // HEAD_DIM))
    w_sq = jnp.concatenate([wq * scale, wqi, wv], axis=1)
    col_s = jnp.arange(T_ROWS)
    flag_sq = (col_s // HEAD_DIM >= H_SB) & (col_s < T_OFF_V)
    head = jnp.arange(N_HEADS)
    f = lambda a: a.astype(F32)[None, :]
    i32 = lambda a: a.astype(jnp.int32)
    return dict(w_tok=w_tok.astype(BF16), flag_tok=f(flag_tok),
                w_feat=w_feat.astype(BF16), flag_feat=i32(flag_feat),
                w_sq=w_sq.astype(BF16), flag_sq=f(flag_sq), w_gate=wg.astype(BF16),
                w_k=wk.T.astype(BF16), flag_k=i32(head >= H_SB),
                w_v=wv.T.astype(BF16), flag_v=i32(head < 0),
                w_ki=wki.T.astype(BF16), flag_ki=jnp.ones((1,), jnp.int32))


def _rope_tables(pos):
    half = HEAD_DIM // 2
    inv = ROPE_THETA ** (-jnp.arange(half, dtype=F32) / half)
    ang = pos.astype(F32)[:, None] * inv[None, :]
    cos, sin = jnp.cos(ang), jnp.sin(ang)
    return jnp.concatenate([cos, cos], axis=1), jnp.concatenate([-sin, sin], axis=1)


def kernel(x_prompt, x_sample, cache_k, cache_v, cache_idx_k, page_table, w_in, w_br_a, w_br_b,
           w_br_c, b_gate, w_out, ln1_g, ln1_b, ln2_g, ln2_b, w_router, router_bias,
           w_exp_gate, w_exp_up, w_exp_down):
    n_b, seq, dm = x_prompt.shape
    n_seq, n_new, _ = x_sample.shape
    depth = w_in.shape[0]
    n_p = n_b * seq
    n_s = n_seq * n_new
    n_pages = page_table.shape[1]
    past = n_pages * PAGE_SIZE
    assert seq % SLAB == 0 and past % MOBA_BLOCK == 0 and n_new <= ROWS_PER_HEAD
    ck_t = cache_k.transpose(0, 1, 3, 4, 2)
    cv_t = cache_v.transpose(0, 1, 3, 4, 2)
    cik_t = cache_idx_k.transpose(0, 1, 3, 2)

    pos = jnp.concatenate([jnp.tile(jnp.arange(seq, dtype=jnp.int32), n_b),
                           jnp.tile(past + jnp.arange(n_new, dtype=jnp.int32), n_seq)])
    cos64, sin64 = _rope_tables(pos)
    cos = jnp.concatenate([cos64, cos64], axis=1)
    sin = jnp.concatenate([sin64, sin64], axis=1)
    cos_t, sin_t = cos64[:n_p].T, sin64[:n_p].T
    x = jnp.concatenate([x_prompt.reshape(n_p, dm), x_sample.reshape(n_s, dm)], axis=0)
    xb = x.astype(BF16)
    proj_tm = max(m * TILE_TOKENS for m in (1, 2, 3) if (n_p + n_s) % (m * TILE_TOKENS) == 0)

    pad_rows = lambda a, n: jnp.pad(a, ((0, 0), (0, n - a.shape[1])) + ((0, 0),) * (a.ndim - 2))
    zeros_b = jnp.zeros((HEAD_DIM, dm), BF16)
    rows_p, rows_s = [], []
    for l in range(depth):
        w = _layer_weights(w_in[l])
        tok_f32, tok_bf = _proj(xb, w["w_tok"], cos, sin, w["flag_tok"], rope=True,
                                tm=proj_tm, tn=TILE_COLS_TOK, name="in_proj_token_major")
        gates = _proj(xb, w["w_gate"], cos, sin, w["flag_tok"], rope=False,
                      tm=proj_tm, tn=TILE_COLS, name="in_proj_gates")
        xb_p = xb[:n_p]
        proj_t = _proj_t(w["flag_feat"], w["w_feat"], xb_p, cos_t, sin_t,
                         tm=2 * TILE_TOKENS, tn=TILE_ROWS_T)
        sq, _ = _proj(xb[n_p:], w["w_sq"], cos[n_p:], sin[n_p:], w["flag_sq"], rope=True,
                      tm=n_s, tn=TILE_ROWS_T, name="in_proj_sample_q")

        out_rows = lambda name, tn: _proj_t(w["flag_" + name], w["w_" + name], xb_p, cos_t, sin_t,
                                            tm=2 * TILE_TOKENS, tn=tn, seq=seq)
        heads_last = lambda a: a.reshape(n_b, N_HEADS, HEAD_DIM, seq).transpose(0, 3, 1, 2)
        rows_p.append((heads_last(out_rows("k", TILE_TOKENS)), heads_last(out_rows("v", TILE_TOKENS)),
                       out_rows("ki", IDX_DIM).transpose(0, 2, 1)))
        tok_s = tok_f32[n_p:].reshape(n_seq, n_new, -1)
        k_s, ki_s = tok_s[..., :QKV_W], tok_s[..., N_OFF_KI:N_OFF_WI]
        v_s = sq[:, T_OFF_V:].reshape(n_seq, n_new, QKV_W)
        rows_s.append((k_s.reshape(n_seq, n_new, N_HEADS, HEAD_DIM),
                       v_s.reshape(n_seq, n_new, N_HEADS, HEAD_DIM), ki_s))

        y_a = _sb_attention(proj_t, tok_bf, n_b=n_b, seq=seq)
        y_b = _dsa_attention(proj_t, tok_bf, tok_f32, n_b=n_b, seq=seq)
        kmean = _block_means(tok_f32, n_p).reshape(n_b, seq // MOBA_BLOCK, QKV_W)
        y_c = _moba_attention(proj_t, tok_bf, kmean, n_b=n_b, seq=seq)
        tok = lambda y: y.transpose(0, 2, 1).reshape(n_p, -1)

        q_s = sq[:, :QKV_W].reshape(n_seq, n_new, N_HEADS, HEAD_DIM)
        q8 = pad_rows(q_s, ROWS_PER_HEAD).transpose(0, 2, 1, 3).astype(BF16)
        qi8 = pad_rows(sq[:, T_OFF_QI:T_OFF_V].reshape(n_seq, n_new, QI_W), ROWS_PER_HEAD).reshape(
            n_seq, ROWS_PER_HEAD * IDX_HEADS, IDX_DIM).astype(BF16)
        wi8 = pad_rows(tok_s[..., N_OFF_WI:N_OFF_WI + IDX_HEADS], ROWS_PER_HEAD).reshape(
            n_seq, ROWS_PER_HEAD * IDX_HEADS, 1)
        new_heads = lambda a: pad_rows(a.reshape(n_seq, n_new, N_HEADS, HEAD_DIM),
                                       PAGE_SIZE).transpose(0, 2, 3, 1).astype(BF16)
        y_s = _sample_attention(
            l, page_table, q8, qi8, wi8, new_heads(k_s), new_heads(v_s),
            pad_rows(ki_s, PAGE_SIZE).transpose(0, 2, 1).astype(BF16),
            ck_t, cv_t, cik_t, n_new=n_new)
        y_s = y_s.reshape(n_s, QKV_W).astype(BF16)

        wa_cols, wb_cols = H_SB * HEAD_DIM, (H_SB + H_DSA) * HEAD_DIM
        blk = DSA_BLOCK_HEADS * HEAD_DIM
        ya = jnp.concatenate([tok(y_a), y_s[:, :wa_cols]], axis=0)
        yb = jnp.concatenate([tok(y_b), y_s[:, wa_cols:wa_cols + blk]], axis=0)
        yc = jnp.concatenate([tok(y_c), y_s[:, QKV_W - blk:]], axis=0)
        w_b = jnp.concatenate([w_br_b[l].astype(BF16), zeros_b], axis=0)
        w_c = jnp.concatenate([zeros_b, w_br_c[l].astype(BF16)], axis=0)
        del wb_cols
        x, xb = _merge(ya, yb, yc, gates, b_gate[l][None, :], w_br_a[l].astype(BF16), w_b, w_c,
                       w_out[l].astype(BF16), x, ln1_g[l][None, :], ln1_b[l][None, :],
                       tm=TILE_TOKENS)
        x, xb = _moe(x, xb, w_router.T, router_bias[:, None], w_exp_gate[l].astype(BF16),
                     w_exp_up[l].astype(BF16), w_exp_down[l].astype(BF16),
                     ln2_g[l][None, :], ln2_b[l][None, :], tm=TILE_TOKENS)

    stack = lambda rows, i: jnp.stack([r[i] for r in rows], 0)
    return (x[:n_p].reshape(n_b, seq, dm), x[n_p:].reshape(n_seq, n_new, dm),
            stack(rows_p, 0), stack(rows_p, 1), stack(rows_p, 2),
            stack(rows_s, 0), stack(rows_s, 1), stack(rows_s, 2))
```

```python
import functools

import jax
import jax.numpy as jnp
from jax import lax
from jax.experimental import pallas as pl
from jax.experimental.pallas import tpu as pltpu

F32 = jnp.float32
BF16 = jnp.bfloat16

HEAD_DIM = 64
H_SB, H_DSA, H_MOBA = 6, 5, 5
N_HEADS = H_SB + H_DSA + H_MOBA
IDX_HEADS = 8
IDX_DIM = 64
DSA_TOPK = 256
MOBA_BLOCK = 256
MOBA_TOPK = 3
N_EXPERTS = 16
EXPERTS_PER_GROUP = 4
N_GROUPS = N_EXPERTS // EXPERTS_PER_GROUP
ROPE_THETA = 10000.0
LN_EPS = 1e-5
DEPTH = 2
ALPHA = (2 * DEPTH) ** 0.25
PAGE_SIZE = 128

LANES = 128
SLAB = 256
PAIR = 2 * HEAD_DIM
NEG = -1e30
INT_MIN = -2 ** 31
VMEM_LIMIT = 56 * 1024 * 1024

QKV_W = N_HEADS * HEAD_DIM
QI_W = IDX_HEADS * IDX_DIM
T_OFF_QI = QKV_W
T_OFF_V = QKV_W + QI_W
T_ROWS = 2 * QKV_W + QI_W
N_OFF_KI = QKV_W
N_OFF_WI = N_OFF_KI + IDX_DIM
N_COLS = QKV_W + LANES
TILE_TOKENS = 512
TILE_COLS = 768
TILE_COLS_TOK = N_COLS // 3
TILE_ROWS_T = 640

_NT = (((1,), (1,)), ((), ()))


def _nt(a, b):
    return lax.dot_general(a, b, _NT, preferred_element_type=F32)


def _dot(a, b):
    return jnp.dot(a, b, preferred_element_type=F32)


def _split_bf16(x):
    hi = x.astype(BF16)
    lo = (x - hi.astype(F32)).astype(BF16)
    return hi, lo


LOG2E = 1.4426950408889634


def _softplus2(z2):
    return jnp.maximum(z2, 0.0) + jnp.log(1.0 + jnp.exp2(-jnp.abs(z2))) * LOG2E


def _params(sem):
    return pltpu.CompilerParams(dimension_semantics=sem, vmem_limit_bytes=VMEM_LIMIT)


def _proj_kernel(x_ref, w_ref, cos_ref, sin_ref, flag_ref, o_ref, ob_ref, *, rope):
    acc = _dot(x_ref[...], w_ref[...])
    if not rope:
        ob_ref[...] = acc.astype(ob_ref.dtype)
        return
    cos = cos_ref[...]
    sin = sin_ref[...]
    lane = lax.broadcasted_iota(jnp.int32, (1, LANES), 1)
    first_half = (lane % HEAD_DIM) < (HEAD_DIM // 2)
    for c in range(acc.shape[1] // LANES):
        sl = slice(c * LANES, (c + 1) * LANES)
        a = acc[:, sl]
        sw = jnp.where(first_half, pltpu.roll(a, LANES - HEAD_DIM // 2, 1),
                       pltpu.roll(a, HEAD_DIM // 2, 1))
        r = jnp.where(flag_ref[:, sl] > 0.0, a * cos + sw * sin, a)
        o_ref[:, sl] = r
        ob_ref[:, sl] = r.astype(BF16)


def _proj(x, w, cos, sin, flag, *, rope, tm, tn, name):
    n, d = x.shape
    width = w.shape[1]
    tile = pl.BlockSpec((tm, tn), lambda j, i: (i, j))
    if rope:
        out_specs = [tile, tile]
        out_shape = [jax.ShapeDtypeStruct((n, width), F32), jax.ShapeDtypeStruct((n, width), BF16)]
        body = functools.partial(_proj_kernel, rope=True)
    else:
        out_specs = tile
        out_shape = jax.ShapeDtypeStruct((n, width), BF16)
        flag = jnp.zeros((1, width), F32)
        body = lambda x_r, w_r, c_r, s_r, f_r, ob_r: _proj_kernel(
            x_r, w_r, c_r, s_r, f_r, None, ob_r, rope=False)
    return pl.pallas_call(
        body,
        grid=(width // tn, n // tm),
        in_specs=[
            pl.BlockSpec((tm, d), lambda j, i: (i, 0)),
            pl.BlockSpec((d, tn), lambda j, i: (0, j)),
            pl.BlockSpec((tm, LANES), lambda j, i: (i, 0)),
            pl.BlockSpec((tm, LANES), lambda j, i: (i, 0)),
            pl.BlockSpec((1, tn), lambda j, i: (0, j)),
        ],
        out_specs=out_specs,
        out_shape=out_shape,
        compiler_params=_params(("parallel", "parallel")),
        name=name,
    )(x, w, cos, sin, flag)


def _proj_t_kernel(flag_ref, w_ref, x_ref, cos_ref, sin_ref, o_ref, *, tn, tm, slabs):
    j = pl.program_id(1)
    acc = _nt(w_ref[...], x_ref[...])
    cos = cos_ref[...]
    sin = sin_ref[...]
    half = HEAD_DIM // 2
    for g in range(tn // HEAD_DIM):
        rows = slice(g * HEAD_DIM, (g + 1) * HEAD_DIM)
        a = acc[rows]
        sw = jnp.concatenate([a[half:], a[:half]], axis=0)
        r = jnp.where(flag_ref[j * (tn // HEAD_DIM) + g] > 0, a * cos + sw * sin, a)
        if slabs:
            for s in range(tm // SLAB):
                o_ref[s, rows, :] = r[:, s * SLAB:(s + 1) * SLAB].astype(o_ref.dtype)
        else:
            o_ref[rows, :] = r


def _proj_t(flags, w_t, x, cos_t, sin_t, *, tm, tn, seq=None):
    n, d = x.shape
    rows = w_t.shape[0]
    if seq is None:
        out_spec = pl.BlockSpec((tm // SLAB, tn, SLAB), lambda i, j, f: (i, j, 0))
        out_shape = jax.ShapeDtypeStruct((n // SLAB, rows, SLAB), BF16)
    else:
        per_seq = seq // tm
        out_spec = pl.BlockSpec((None, tn, tm), lambda i, j, f: (i // per_seq, j, i % per_seq))
        out_shape = jax.ShapeDtypeStruct((n // seq, rows, seq), F32)
    grid_spec = pltpu.PrefetchScalarGridSpec(
        num_scalar_prefetch=1,
        grid=(n // tm, rows // tn),
        in_specs=[
            pl.BlockSpec((tn, d), lambda i, j, f: (j, 0)),
            pl.BlockSpec((tm, d), lambda i, j, f: (i, 0)),
            pl.BlockSpec((HEAD_DIM, tm), lambda i, j, f: (0, i)),
            pl.BlockSpec((HEAD_DIM, tm), lambda i, j, f: (0, i)),
        ],
        out_specs=out_spec,
    )
    return pl.pallas_call(
        functools.partial(_proj_t_kernel, tn=tn, tm=tm, slabs=seq is None),
        grid_spec=grid_spec,
        out_shape=out_shape,
        compiler_params=_params(("parallel", "arbitrary")),
        name="in_proj_feature_major" if seq is None else "in_proj_output_rows",
    )(flags, w_t, x, cos_t, sin_t)


def _head_masks(q_pair):
    first = lax.broadcasted_iota(jnp.int32, (PAIR, 1), 0) < HEAD_DIM
    zero = jnp.zeros_like(q_pair)
    return jnp.where(first, q_pair, zero), jnp.where(first, zero, q_pair)


def _online_update(state, scores, masks, v_t):
    m, l, acc = state
    scores = [jnp.where(mk, s, NEG) for s, mk in zip(scores, masks)]
    m_new = m
    for s in scores:
        m_new = jnp.maximum(m_new, jnp.max(s, axis=0, keepdims=True))
    alpha = jnp.exp2(m - m_new)
    probs = [jnp.exp2(s - m_new) for s in scores]
    l = alpha * l
    for p in probs:
        l = l + jnp.sum(p, axis=0, keepdims=True)
    p_cat = probs[0] if len(probs) == 1 else jnp.concatenate(probs, axis=0)
    acc = alpha * acc + _dot(v_t, p_cat.astype(BF16))
    return m_new, l, acc


def _sb_kernel(q_ref, k_ref, v_ref, tri_ref, o_ref):
    t = SLAB
    i = pl.program_id(2)
    qm = _head_masks(q_ref[...])
    tri2 = tri_ref[...]
    key_i = lax.broadcasted_iota(jnp.int32, (t, t), 0)
    qry_i = lax.broadcasted_iota(jnp.int32, (t, t), 1)
    before = key_i < qry_i

    def tiles(js, carry, diag):
        chains = [(n, hh) for n in range(len(js)) for hh in range(2)]
        ks = [k_ref[pl.ds(pl.multiple_of(j * t, t), t), :] for j in js]
        vs = [v_ref[j] for j in js]
        z = {ch: _dot(ks[ch[0]], qm[ch[1]]) for ch in chains}
        sp = {ch: _softplus2(z[ch]) for ch in chains}
        drop = sp if not diag else {ch: jnp.where(before, sp[ch], 0.0) for ch in chains}
        later = {ch: _dot(tri2, jnp.concatenate(_split_bf16(drop[ch]), axis=0)) for ch in chains}
        out = []
        for hh in range(2):
            c, acc = carry[2 * hh:2 * hh + 2]
            a_all = []
            for n in range(len(js)):
                ch = (n, hh)
                a = jnp.exp2(z[ch] - sp[ch] - later[ch] - c)
                if diag:
                    a = jnp.where(before, a, 0.0)
                a_all.append(a.astype(BF16))
                c = c + jnp.sum(drop[ch], axis=0, keepdims=True)
            v_cat = jnp.concatenate([v[hh * HEAD_DIM:(hh + 1) * HEAD_DIM] for v in vs], axis=1)
            acc = acc + _dot(v_cat, jnp.concatenate(a_all, axis=0))
            out += [c, acc]
        return tuple(out)

    init = (jnp.zeros((1, t), F32), jnp.zeros((HEAD_DIM, t), F32)) * 2
    carry = tiles([i], init, True)
    carry = lax.fori_loop(
        0, i // 2, lambda n, ca: tiles([i - 1 - 2 * n, i - 2 - 2 * n], ca, False), carry)
    carry = lax.cond(i % 2 == 1, lambda ca: tiles([0], ca, False), lambda ca: ca, carry)
    o_ref[...] = jnp.concatenate([carry[1], carry[3]], axis=0).astype(o_ref.dtype)


def _tri_upper(n):
    r = lax.broadcasted_iota(jnp.int32, (n, n), 0)
    c = lax.broadcasted_iota(jnp.int32, (n, n), 1)
    return (c > r).astype(BF16)


def _sb_attention(proj_t, k_tok, *, n_b, seq):
    nq = seq // SLAB
    n_pairs = H_SB // 2
    v_blk0 = T_OFF_V // PAIR
    return pl.pallas_call(
        _sb_kernel,
        grid=(n_b, n_pairs, nq),
        in_specs=[
            pl.BlockSpec((None, PAIR, SLAB), lambda b, p, i: (b * nq + i, p, 0)),
            pl.BlockSpec((seq, PAIR), lambda b, p, i: (b, p)),
            pl.BlockSpec((nq, PAIR, SLAB), lambda b, p, i: (b, v_blk0 + p, 0)),
            pl.BlockSpec((SLAB, 2 * SLAB), lambda b, p, i: (0, 0)),
        ],
        out_specs=pl.BlockSpec((None, PAIR, SLAB), lambda b, p, i: (b * nq + i, p, 0)),
        out_shape=jax.ShapeDtypeStruct((n_b * nq, n_pairs * PAIR, SLAB), BF16),
        compiler_params=_params(("parallel", "parallel", "arbitrary")),
        name="sb_attention",
    )(proj_t, k_tok, proj_t, jnp.concatenate([_tri_upper(SLAB)] * 2, axis=1))


def _sort_key(score):
    bits = lax.bitcast_convert_type(score + 0.0, jnp.int32)
    return bits ^ ((bits >> 31) & jnp.int32(0x7FFFFFFF))


_NEG_INF_KEY = -2139095041


def _kth_largest(count_ge, shape, k, bits_per_step=1):
    def body(it, ans):
        unit = jnp.left_shift(jnp.int32(1), 32 - bits_per_step * (it + 1))
        passed = jnp.zeros(shape, jnp.int32)
        for m in range(1, 2 ** bits_per_step):
            passed = passed + jnp.where(count_ge(ans + m * unit) >= k, 1, 0)
        return ans + passed * unit
    return lax.fori_loop(0, 32 // bits_per_step, body, jnp.full(shape, INT_MIN, jnp.int32))


def _tie_cut(count_eq_below, need, shape, n_bits):
    def body(it, a):
        cand = a + jnp.left_shift(jnp.int32(1), n_bits - 1 - it)
        return jnp.where(count_eq_below(cand) < need, cand, a)
    return lax.fori_loop(0, n_bits, body, jnp.zeros(shape, jnp.int32))


DSA_BLOCK_HEADS = 6


def _dsa_kernel(qi_ref, kiwi_ref, ki_ref, q_ref, k_ref, v_ref, o_ref,
                key_scr, m_scr, l_scr, acc_scr, *, nq, n_keep, n_bits):
    t = SLAB
    i = pl.program_id(1)
    n_slab = i + 1
    qpos = i * t + lax.broadcasted_iota(jnp.int32, (1, t), 1)
    kiota = lax.broadcasted_iota(jnp.int32, (t, 1), 0)
    wi_t = kiwi_ref[...].T

    def scores(j, _):
        kij = ki_ref[pl.ds(pl.multiple_of(j * t, t), t), :][:, :IDX_DIM]
        score = jnp.zeros((t, t), F32)
        for h in range(IDX_HEADS):
            sc = _dot(kij, qi_ref[h * IDX_DIM:(h + 1) * IDX_DIM, :])
            score = score + wi_t[IDX_DIM + h:IDX_DIM + h + 1, :] * jnp.maximum(sc, 0.0)
        causal = (j * t + kiota) <= qpos
        key_scr[j] = _sort_key(jnp.where(causal, score, -jnp.inf))
        return 0

    lax.fori_loop(0, n_slab, scores, 0)

    n_pair = (n_slab + 1) // 2

    @pl.when(n_slab % 2 == 1)
    def _():
        key_scr[n_slab] = jnp.full((t, t), _NEG_INF_KEY, jnp.int32)

    def count(pred):
        def body(n, acc):
            for j in (2 * n, 2 * n + 1):
                ones = jnp.where(pred(key_scr[j], j * t + kiota), 1.0, 0.0)
                acc = acc + jnp.sum(ones.reshape(t // 8, 8, t), axis=0)
            return acc
        acc = lax.fori_loop(0, n_pair, body, jnp.zeros((8, t), F32))
        return jnp.sum(acc, axis=0, keepdims=True)

    thr = _kth_largest(lambda cand: count(lambda key, kp: key >= cand), (1, t), float(n_keep))
    need = float(n_keep) - count(lambda key, kp: key > thr)
    n_eq = count(lambda key, kp: key == thr)
    cut = lax.cond(
        jnp.max(n_eq - need) > 0.0,
        lambda: _tie_cut(lambda cand: count(lambda key, kp: (key == thr) & (kp < cand)),
                         need, (1, t), n_bits),
        lambda: jnp.full((1, t), 2 ** n_bits, jnp.int32))

    m_scr[...] = jnp.full(m_scr.shape, NEG, F32)
    l_scr[...] = jnp.zeros(l_scr.shape, F32)
    acc_scr[...] = jnp.zeros(acc_scr.shape, F32)
    qm = []
    for pr in range(DSA_BLOCK_HEADS // 2):
        qm += list(_head_masks(q_ref[pr * PAIR:(pr + 1) * PAIR, :]))

    def attend(n, _):
        sel, z, vs = [], [], []
        for j in (2 * n, 2 * n + 1):
            key = key_scr[j]
            kp = j * t + kiota
            sel.append(((key > thr) | ((key == thr) & (kp <= cut))) & (kp <= qpos))
            jk = jnp.minimum(j, nq - 1)
            start = pl.multiple_of(jk * t, t)
            vs.append(v_ref[jk])
            kj = [k_ref[pl.ds(start, t), pr * PAIR:(pr + 1) * PAIR]
                  for pr in range((H_DSA + 1) // 2)]
            z.append([_dot(kj[h // 2], qm[h]) for h in range(H_DSA)])
        for h in range(H_DSA):
            state = (m_scr[h], l_scr[h], acc_scr[h])
            v_cat = jnp.concatenate([v[h * HEAD_DIM:(h + 1) * HEAD_DIM] for v in vs], axis=1)
            m, l, acc = _online_update(state, [z[0][h], z[1][h]], sel, v_cat)
            m_scr[h] = m
            l_scr[h] = l
            acc_scr[h] = acc
        return 0

    lax.fori_loop(0, n_pair, attend, 0)
    for h in range(H_DSA):
        o_ref[h * HEAD_DIM:(h + 1) * HEAD_DIM, :] = (acc_scr[h] / l_scr[h]).astype(o_ref.dtype)
    o_ref[H_DSA * HEAD_DIM:, :] = jnp.zeros((HEAD_DIM, t), o_ref.dtype)


def _dsa_attention(proj_t, k_tok, kiwi_f32, *, n_b, seq):
    nq = seq // SLAB
    wide = DSA_BLOCK_HEADS * HEAD_DIM
    n_keep = min(DSA_TOPK, seq // 4)
    n_bits = max(1, (seq - 1).bit_length())
    kiwi_blk = N_OFF_KI // LANES
    return pl.pallas_call(
        functools.partial(_dsa_kernel, nq=nq, n_keep=n_keep, n_bits=n_bits),
        grid=(n_b, nq),
        in_specs=[
            pl.BlockSpec((None, QI_W, SLAB), lambda b, i: (b * nq + i, T_OFF_QI // QI_W, 0)),
            pl.BlockSpec((SLAB, LANES), lambda b, i: (b * nq + i, kiwi_blk)),
            pl.BlockSpec((seq, LANES), lambda b, i: (b, kiwi_blk)),
            pl.BlockSpec((None, wide, SLAB), lambda b, i: (b * nq + i, 1, 0)),
            pl.BlockSpec((seq, wide), lambda b, i: (b, 1)),
            pl.BlockSpec((nq, wide, SLAB), lambda b, i: (b, T_OFF_V // wide + 1, 0)),
        ],
        out_specs=pl.BlockSpec((None, wide, SLAB), lambda b, i: (b * nq + i, 0, 0)),
        out_shape=jax.ShapeDtypeStruct((n_b * nq, wide, SLAB), BF16),
        scratch_shapes=[pltpu.VMEM((nq + 1, SLAB, SLAB), jnp.int32),
                        pltpu.VMEM((H_DSA, 1, SLAB), F32), pltpu.VMEM((H_DSA, 1, SLAB), F32),
                        pltpu.VMEM((H_DSA, HEAD_DIM, SLAB), F32)],
        compiler_params=_params(("parallel", "arbitrary")),
        name="dsa_attention",
    )(proj_t, kiwi_f32, k_tok, proj_t, k_tok, proj_t)


def _block_mean_kernel(k_ref, o_ref):
    o_ref[0] = jnp.mean(k_ref[:, :QKV_W], axis=0, keepdims=True)


def _block_means(k_tok_f32, n_tokens):
    nb = n_tokens // MOBA_BLOCK
    return pl.pallas_call(
        _block_mean_kernel,
        grid=(nb,),
        in_specs=[pl.BlockSpec((MOBA_BLOCK, QKV_W), lambda n: (n, 0))],
        out_specs=pl.BlockSpec((1, 1, QKV_W), lambda n: (n, 0, 0)),
        out_shape=jax.ShapeDtypeStruct((nb, 1, QKV_W), F32),
        compiler_params=_params(("parallel",)),
        name="moba_block_means",
    )(k_tok_f32)


MOBA_FIRST_PAIR = (H_SB + H_DSA) // 2


def _moba_kernel(q_ref, k_ref, v_ref, km_ref, o_ref, sel_scr, *, nb):
    t = MOBA_BLOCK
    i = pl.program_id(2)
    qm = _head_masks(q_ref[...])
    km_hi, km_lo = _split_bf16(km_ref[...])
    n_iota = lax.broadcasted_iota(jnp.int32, (nb, t), 0)
    fi = n_iota.astype(F32)
    valid = n_iota < i
    for hh in range(2):
        gm = jnp.where(valid, _dot(km_hi, qm[hh]) + _dot(km_lo, qm[hh]), -jnp.inf)
        sel = jnp.zeros((nb, t), jnp.bool_)
        for _ in range(min(MOBA_TOPK, nb)):
            m = jnp.max(gm, axis=0, keepdims=True)
            first = jnp.min(jnp.where(gm == m, fi, float(nb)), axis=0, keepdims=True)
            pick = fi == first
            sel = sel | pick
            gm = jnp.where(pick, -jnp.inf, gm)
        sel_scr[hh] = jnp.where(sel & valid, 1.0, 0.0)

    def blocks(ns, carry, mask_of):
        ks = [k_ref[pl.ds(pl.multiple_of(n * t, t), t), :] for n in ns]
        vs = [v_ref[n] for n in ns]
        z = [[_dot(kj, qm[hh]) for kj in ks] for hh in range(2)]
        out = []
        for hh in range(2):
            v_cat = jnp.concatenate([v[hh * HEAD_DIM:(hh + 1) * HEAD_DIM] for v in vs], axis=1)
            out += list(_online_update(carry[3 * hh:3 * hh + 3], z[hh],
                                       [mask_of(hh, n) for n in ns], v_cat))
        return tuple(out)

    key_i = lax.broadcasted_iota(jnp.int32, (t, t), 0)
    qry_i = lax.broadcasted_iota(jnp.int32, (t, t), 1)
    chosen = lambda hh, n: sel_scr[hh, pl.ds(n, 1), :] > 0.0
    init = (jnp.full((1, t), NEG, F32), jnp.zeros((1, t), F32), jnp.zeros((HEAD_DIM, t), F32)) * 2
    carry = blocks([i], init, lambda hh, n: key_i <= qry_i)
    carry = lax.fori_loop(0, i // 2, lambda n, ca: blocks([2 * n, 2 * n + 1], ca, chosen), carry)
    carry = lax.cond(i % 2 == 1, lambda ca: blocks([i - 1], ca, chosen), lambda ca: ca, carry)
    o_ref[...] = jnp.concatenate([carry[2] / carry[1], carry[5] / carry[4]],
                                 axis=0).astype(o_ref.dtype)


def _moba_attention(proj_t, k_tok, kmean, *, n_b, seq):
    nq = seq // SLAB
    nb = seq // MOBA_BLOCK
    n_pairs = N_HEADS // 2 - MOBA_FIRST_PAIR
    p0 = MOBA_FIRST_PAIR
    v_blk0 = T_OFF_V // PAIR + p0
    return pl.pallas_call(
        functools.partial(_moba_kernel, nb=nb),
        grid=(n_b, n_pairs, nq),
        in_specs=[
            pl.BlockSpec((None, PAIR, SLAB), lambda b, p, i: (b * nq + i, p0 + p, 0)),
            pl.BlockSpec((seq, PAIR), lambda b, p, i: (b, p0 + p)),
            pl.BlockSpec((nq, PAIR, SLAB), lambda b, p, i: (b, v_blk0 + p, 0)),
            pl.BlockSpec((None, nb, PAIR), lambda b, p, i: (b, 0, p0 + p)),
        ],
        out_specs=pl.BlockSpec((None, PAIR, SLAB), lambda b, p, i: (b * nq + i, p, 0)),
        out_shape=jax.ShapeDtypeStruct((n_b * nq, n_pairs * PAIR, SLAB), BF16),
        scratch_shapes=[pltpu.VMEM((2, nb, SLAB), F32)],
        compiler_params=_params(("parallel", "parallel", "arbitrary")),
        name="moba_attention",
    )(proj_t, k_tok, proj_t, kmean)


ROWS_PER_HEAD = 8


def _top_mask(cols, n_top):
    out = []
    for n, g in enumerate(cols):
        rank = jnp.zeros_like(g)
        for m, o in enumerate(cols):
            if m == n:
                continue
            ahead = (o >= g) if m < n else (o > g)
            rank = rank + jnp.where(ahead, 1.0, 0.0)
        out.append(rank < float(n_top))
    return out


def _sample_mixers(s_scr, p_scr, ki_scr, qi_ref, wi_ref, *, past, n_keep, n_bits):
    lk_len = past + PAGE_SIZE
    r8 = ROWS_PER_HEAD
    kpos = lax.broadcasted_iota(jnp.int32, (1, lk_len), 1)

    def qpos(rows):
        return past + lax.broadcasted_iota(jnp.int32, (rows, 1), 0) % r8

    ra = H_SB * r8
    z = s_scr[:ra, :]
    before = kpos < qpos(ra)
    sp = _softplus2(z)
    drop = jnp.where(before, sp, 0.0)
    hi, lo = _split_bf16(drop)
    tri_r = lax.broadcasted_iota(jnp.int32, (2 * LANES, LANES), 0) % LANES
    tri_c = lax.broadcasted_iota(jnp.int32, (2 * LANES, LANES), 1)
    tri2 = (tri_r > tri_c).astype(BF16)
    c = jnp.zeros((ra, 1), F32)
    for blk in reversed(range(lk_len // LANES)):
        sl = slice(blk * LANES, (blk + 1) * LANES)
        later = _dot(jnp.concatenate([hi[:, sl], lo[:, sl]], axis=1), tri2) + c
        a = jnp.where(before[:, sl], jnp.exp2(z[:, sl] - sp[:, sl] - later), 0.0)
        p_scr[:ra, sl] = a.astype(BF16)
        c = c + jnp.sum(drop[:, sl], axis=1, keepdims=True)

    sc = jnp.maximum(_dot(qi_ref[0], ki_scr[...]), 0.0) * wi_ref[0]
    score = jnp.sum(sc.reshape(r8, IDX_HEADS, lk_len), axis=1)
    causal8 = kpos <= qpos(r8)
    key = _sort_key(jnp.where(causal8, score, -jnp.inf))

    def count(pred):
        return jnp.sum(jnp.where(pred, 1.0, 0.0), axis=1, keepdims=True)

    thr = _kth_largest(lambda cand: count(key >= cand), (r8, 1), float(n_keep), bits_per_step=2)
    need = float(n_keep) - count(key > thr)
    n_eq = count(key == thr)
    cut = lax.cond(
        jnp.max(n_eq - need) > 0.0,
        lambda: _tie_cut(lambda cand: count((key == thr) & (kpos < cand)), need, (r8, 1), n_bits),
        lambda: jnp.full((r8, 1), 2 ** n_bits, jnp.int32))
    sel8 = ((key > thr) | ((key == thr) & (kpos <= cut))) & causal8
    rb = H_DSA * r8
    sel = jnp.concatenate([jnp.where(sel8, 1.0, 0.0)] * H_DSA, axis=0) > 0.0
    s = jnp.where(sel, s_scr[ra:ra + rb, :], NEG)
    p = jnp.where(sel, jnp.exp2(s - jnp.max(s, axis=1, keepdims=True)), 0.0)
    l_b = jnp.sum(p, axis=1, keepdims=True)
    p_scr[ra:ra + rb, :] = p.astype(BF16)

    rc = H_MOBA * r8
    sm = s_scr[ra + rb:, :]
    n_blk = past // MOBA_BLOCK
    gate = [jnp.sum(sm[:, n * MOBA_BLOCK:(n + 1) * MOBA_BLOCK], axis=1, keepdims=True)
            for n in range(n_blk)]
    chosen = _top_mask(gate, min(MOBA_TOPK, n_blk + 1))
    own_ok = kpos[:, past:] <= qpos(rc)
    pieces = [jnp.where(chosen[n], sm[:, n * MOBA_BLOCK:(n + 1) * MOBA_BLOCK], NEG)
              for n in range(n_blk)]
    pieces.append(jnp.where(own_ok, sm[:, past:], NEG))
    m_c = pieces[0].max(axis=1, keepdims=True)
    for pc in pieces[1:]:
        m_c = jnp.maximum(m_c, pc.max(axis=1, keepdims=True))
    l_c = jnp.zeros((rc, 1), F32)
    off = 0
    for pc in pieces:
        e = jnp.where(pc > 0.5 * NEG, jnp.exp2(pc - m_c), 0.0)
        l_c = l_c + jnp.sum(e, axis=1, keepdims=True)
        p_scr[ra + rb:, off:off + pc.shape[1]] = e.astype(BF16)
        off += pc.shape[1]

    return jnp.concatenate([jnp.ones((ra, 1), F32), 1.0 / l_b, 1.0 / l_c], axis=0)


def _sample_kernel(pt_ref, q_ref, qi_ref, wi_ref, knew_ref, vnew_ref, kinew_ref, *rest,
                   n_pages, n_keep, n_bits):
    del pt_ref
    k_pages, v_pages = rest[:n_pages], rest[n_pages:2 * n_pages]
    i_pages = rest[2 * n_pages:3 * n_pages]
    o_ref = rest[3 * n_pages]
    k_scr, v_scr, ki_scr, s_scr, p_scr = rest[3 * n_pages + 1:]
    past = n_pages * PAGE_SIZE
    r8 = ROWS_PER_HEAD

    for pg in range(n_pages):
        cols = slice(pg * PAGE_SIZE, (pg + 1) * PAGE_SIZE)
        k_scr[:, :, cols] = k_pages[pg][...].astype(BF16)
        v_scr[:, :, cols] = v_pages[pg][...].astype(BF16)
        ki_scr[:, cols] = i_pages[pg][...].astype(BF16)
    k_scr[:, :, past:] = knew_ref[0]
    v_scr[:, :, past:] = vnew_ref[0]
    ki_scr[:, past:] = kinew_ref[0]
    for h in range(N_HEADS):
        s_scr[h * r8:(h + 1) * r8, :] = _dot(q_ref[0, h], k_scr[h])
    inv = _sample_mixers(s_scr, p_scr, ki_scr, qi_ref, wi_ref, past=past, n_keep=n_keep,
                         n_bits=n_bits)
    for h in range(N_HEADS):
        rows = slice(h * r8, (h + 1) * r8)
        o_ref[0, :, h * HEAD_DIM:(h + 1) * HEAD_DIM] = _nt(p_scr[rows, :], v_scr[h]) * inv[rows]


def _sample_attention(layer, page_table, q8, qi8, wi8, knew, vnew, kinew,
                      cache_k, cache_v, cache_ik, *, n_new):
    n_seq, n_pages = page_table.shape
    past = n_pages * PAGE_SIZE
    lk_len = past + PAGE_SIZE
    n_keep = min(DSA_TOPK, (past + n_new) // 4)
    n_bits = max(1, (lk_len - 1).bit_length())
    rows = N_HEADS * ROWS_PER_HEAD

    def page_spec(j, *dims):
        return pl.BlockSpec((None, None) + dims + (PAGE_SIZE,),
                            lambda b, pt: (layer, pt[b, j]) + (0,) * (len(dims) + 1))

    def seq_spec(*dims):
        return pl.BlockSpec((1,) + dims, lambda b, pt: (b,) + (0,) * len(dims))

    in_specs = [seq_spec(N_HEADS, ROWS_PER_HEAD, HEAD_DIM),
                seq_spec(ROWS_PER_HEAD * IDX_HEADS, IDX_DIM),
                seq_spec(ROWS_PER_HEAD * IDX_HEADS, 1),
                seq_spec(N_HEADS, HEAD_DIM, PAGE_SIZE), seq_spec(N_HEADS, HEAD_DIM, PAGE_SIZE),
                seq_spec(IDX_DIM, PAGE_SIZE)]
    in_specs += [page_spec(j, N_HEADS, HEAD_DIM) for j in range(n_pages)]
    in_specs += [page_spec(j, N_HEADS, HEAD_DIM) for j in range(n_pages)]
    in_specs += [page_spec(j, IDX_DIM) for j in range(n_pages)]
    grid_spec = pltpu.PrefetchScalarGridSpec(
        num_scalar_prefetch=1,
        grid=(n_seq,),
        in_specs=in_specs,
        out_specs=pl.BlockSpec((1, ROWS_PER_HEAD, QKV_W), lambda b, pt: (b, 0, 0)),
        scratch_shapes=[pltpu.VMEM((N_HEADS, HEAD_DIM, lk_len), BF16),
                        pltpu.VMEM((N_HEADS, HEAD_DIM, lk_len), BF16),
                        pltpu.VMEM((IDX_DIM, lk_len), BF16),
                        pltpu.VMEM((rows, lk_len), F32), pltpu.VMEM((rows, lk_len), BF16)],
    )
    out = pl.pallas_call(
        functools.partial(_sample_kernel, n_pages=n_pages, n_keep=n_keep, n_bits=n_bits),
        grid_spec=grid_spec,
        out_shape=jax.ShapeDtypeStruct((n_seq, ROWS_PER_HEAD, QKV_W), F32),
        compiler_params=_params(("arbitrary",)),
        name="sample_attention",
    )(page_table, q8, qi8, wi8, knew, vnew, kinew,
      *([cache_k] * n_pages), *([cache_v] * n_pages), *([cache_ik] * n_pages))
    return out[:, :n_new]


def _layer_norm(h, g, b):
    mu = jnp.mean(h, axis=-1, keepdims=True)
    d = h - mu
    var = jnp.mean(d * d, axis=-1, keepdims=True)
    return d * lax.rsqrt(var + LN_EPS) * g + b


def _merge_kernel(ya_ref, yb_ref, yc_ref, g_ref, bg_ref, wa_ref, wb_ref, wc_ref, wo_ref,
                  x_ref, lg_ref, lb_ref, o_ref, ob_ref):
    dm = x_ref.shape[1]
    merged = jnp.zeros(x_ref.shape, F32)
    for n, (y_ref, w_ref) in enumerate(((ya_ref, wa_ref), (yb_ref, wb_ref), (yc_ref, wc_ref))):
        sl = slice(n * dm, (n + 1) * dm)
        gate = jax.nn.sigmoid(g_ref[:, sl].astype(F32) + bg_ref[:, sl])
        merged = merged + gate * _dot(y_ref[...], w_ref[...])
    mix = _dot(merged.astype(BF16), wo_ref[...])
    y = _layer_norm(ALPHA * x_ref[...] + mix, lg_ref[...], lb_ref[...])
    o_ref[...] = y
    ob_ref[...] = y.astype(BF16)


def _merge(ya, yb, yc, g, bg, wa, wb, wc, wo, x, lg, lb, *, tm):
    n, dm = x.shape
    row = lambda w: pl.BlockSpec((tm, w), lambda i: (i, 0))
    full = lambda a: pl.BlockSpec(a.shape, lambda i: (0, 0))
    return pl.pallas_call(
        _merge_kernel,
        grid=(n // tm,),
        in_specs=[row(ya.shape[1]), row(yb.shape[1]), row(yc.shape[1]), row(g.shape[1]),
                  full(bg), full(wa), full(wb), full(wc), full(wo), row(dm), full(lg), full(lb)],
        out_specs=[row(dm), row(dm)],
        out_shape=[jax.ShapeDtypeStruct((n, dm), F32), jax.ShapeDtypeStruct((n, dm), BF16)],
        compiler_params=_params(("parallel",)),
        name="merge_out_ln",
    )(ya, yb, yc, g, bg, wa, wb, wc, wo, x, lg, lb)


def _route(x, xb, wr_ref, rb_ref, comb_ref):
    w_hi, w_lo = _split_bf16(wr_ref[...])
    x_lo = (x - xb.astype(F32)).astype(BF16)
    logits = _nt(w_hi, xb) + _nt(w_hi, x_lo) + _nt(w_lo, xb)
    s = jax.nn.sigmoid(logits)
    sb = s + rb_ref[...]
    s_col = [s[e:e + 1, :] for e in range(N_EXPERTS)]
    b_col = [sb[e:e + 1, :] for e in range(N_EXPERTS)]
    best_score = None
    for gi in range(N_GROUPS):
        a, b, c, d = b_col[gi * EXPERTS_PER_GROUP:(gi + 1) * EXPERTS_PER_GROUP]
        m1, n1 = jnp.maximum(a, b), jnp.minimum(a, b)
        m2, n2 = jnp.maximum(c, d), jnp.minimum(c, d)
        score = jnp.maximum(m1, m2) + jnp.maximum(jnp.minimum(m1, m2), jnp.maximum(n1, n2))
        if gi == 0:
            best_score, best = score, jnp.zeros_like(score)
        else:
            better = score > best_score
            best = jnp.where(better, float(gi), best)
            best_score = jnp.where(better, score, best_score)
    picked = []
    for gi in range(N_GROUPS):
        cols = b_col[gi * EXPERTS_PER_GROUP:(gi + 1) * EXPERTS_PER_GROUP]
        top = _top_mask(cols, 2)
        in_group = best == float(gi)
        for j in range(EXPERTS_PER_GROUP):
            picked.append(jnp.where(in_group & top[j], s_col[gi * EXPERTS_PER_GROUP + j], 0.0))
    total = picked[0]
    for c in picked[1:]:
        total = total + c
    pad = jnp.zeros((LANES - N_EXPERTS, total.shape[1]), F32)
    comb_ref[...] = jnp.concatenate([p / total for p in picked] + [pad], axis=0).T


def _moe_kernel(x_ref, xb_ref, wr_ref, rb_ref, wg_ref, wu_ref, wd_ref, lg_ref, lb_ref,
                o_ref, ob_ref, comb_ref, acc_ref):
    e = pl.program_id(1)

    @pl.when(e == 0)
    def _():
        _route(x_ref[...], xb_ref[...], wr_ref, rb_ref, comb_ref)
        acc_ref[...] = jnp.zeros_like(acc_ref)

    xb = xb_ref[...]
    gate = _dot(xb, wg_ref[0])
    lane = lax.broadcasted_iota(jnp.int32, comb_ref.shape, 1)
    weight = jnp.sum(jnp.where(lane == e, comb_ref[...], 0.0), axis=1, keepdims=True)
    h = gate * jax.nn.sigmoid(gate) * _dot(xb, wu_ref[0]) * weight
    acc_ref[...] += _dot(h.astype(BF16), wd_ref[0])

    @pl.when(e == pl.num_programs(1) - 1)
    def _():
        y = _layer_norm(ALPHA * x_ref[...] + acc_ref[...], lg_ref[...], lb_ref[...])
        o_ref[...] = y
        ob_ref[...] = y.astype(BF16)


def _moe(x, xb, wr, rb, wg, wu, wd, lg, lb, *, tm):
    n, dm = x.shape
    n_e, _, de = wg.shape
    row = pl.BlockSpec((tm, dm), lambda i, e: (i, 0))
    full = lambda a: pl.BlockSpec(a.shape, lambda i, e: (0, 0))
    return pl.pallas_call(
        _moe_kernel,
        grid=(n // tm, n_e),
        in_specs=[row, row, full(wr), full(rb),
                  pl.BlockSpec((1, dm, de), lambda i, e: (e, 0, 0)),
                  pl.BlockSpec((1, dm, de), lambda i, e: (e, 0, 0)),
                  pl.BlockSpec((1, de, dm), lambda i, e: (e, 0, 0)),
                  full(lg), full(lb)],
        out_specs=[row, row],
        out_shape=[jax.ShapeDtypeStruct((n, dm), F32), jax.ShapeDtypeStruct((n, dm), BF16)],
        scratch_shapes=[pltpu.VMEM((tm, LANES), F32), pltpu.VMEM((tm, dm), F32)],
        compiler_params=_params(("parallel", "arbitrary")),
        name="moe_ln",
    )(x, xb, wr, rb, wg, wu, wd, lg, lb)


def _pad_cols(a, width):
    return jnp.pad(a, ((0, 0), (0, width - a.shape[1])))


def _layer_weights(w_in_l):
    scale = HEAD_DIM ** -0.5 * LOG2E
    cuts = [QKV_W, 2 * QKV_W, 3 * QKV_W, 3 * QKV_W + QI_W, 3 * QKV_W + QI_W + IDX_DIM,
            3 * QKV_W + QI_W + IDX_DIM + IDX_HEADS]
    wq, wk, wv, wqi, wki, wwi, wg = jnp.split(w_in_l, cuts, axis=1)
    w_tok = _pad_cols(jnp.concatenate([wk, wki, wwi], axis=1), N_COLS)
    col = jnp.arange(N_COLS)
    flag_tok = ((col < QKV_W) & (col // HEAD_DIM >= H_SB)) | ((col >= N_OFF_KI) & (col < N_OFF_WI))
    w_feat = jnp.concatenate([wq * scale, wqi, wv], axis=1).T
    grp = jnp.arange(T_ROWS // HEAD_DIM)
    flag_feat = ((grp >= H_SB) & (grp < N_HEADS)) | ((grp >= T_OFF_QI // HEAD_DIM)
                                                     & (grp < T_OFF_V // HEAD_DIM))
    w_sq = jnp.concatenate([wq * scale, wqi, wv], axis=1)
    col_s = jnp.arange(T_ROWS)
    flag_sq = (col_s // HEAD_DIM >= H_SB) & (col_s < T_OFF_V)
    head = jnp.arange(N_HEADS)
    f = lambda a: a.astype(F32)[None, :]
    i32 = lambda a: a.astype(jnp.int32)
    return dict(w_tok=w_tok.astype(BF16), flag_tok=f(flag_tok),
                w_feat=w_feat.astype(BF16), flag_feat=i32(flag_feat),
                w_sq=w_sq.astype(BF16), flag_sq=f(flag_sq), w_gate=wg.astype(BF16),
                w_k=wk.T.astype(BF16), flag_k=i32(head >= H_SB),
                w_v=wv.T.astype(BF16), flag_v=i32(head < 0),
                w_ki=wki.T.astype(BF16), flag_ki=jnp.ones((1,), jnp.int32))


def _rope_tables(pos):
    half = HEAD_DIM // 2
    inv = ROPE_THETA ** (-jnp.arange(half, dtype=F32) / half)
    ang = pos.astype(F32)[:, None] * inv[None, :]
    cos, sin = jnp.cos(ang), jnp.sin(ang)
    return jnp.concatenate([cos, cos], axis=1), jnp.concatenate([-sin, sin], axis=1)


def kernel(x_prompt, x_sample, cache_k, cache_v, cache_idx_k, page_table, w_in, w_br_a, w_br_b,
           w_br_c, b_gate, w_out, ln1_g, ln1_b, ln2_g, ln2_b, w_router, router_bias,
           w_exp_gate, w_exp_up, w_exp_down):
    n_b, seq, dm = x_prompt.shape
    n_seq, n_new, _ = x_sample.shape
    depth = w_in.shape[0]
    n_p = n_b * seq
    n_s = n_seq * n_new
    n_pages = page_table.shape[1]
    past = n_pages * PAGE_SIZE
    assert seq % SLAB == 0 and past % MOBA_BLOCK == 0 and n_new <= ROWS_PER_HEAD
    ck_t = cache_k.transpose(0, 1, 3, 4, 2)
    cv_t = cache_v.transpose(0, 1, 3, 4, 2)
    cik_t = cache_idx_k.transpose(0, 1, 3, 2)

    pos = jnp.concatenate([jnp.tile(jnp.arange(seq, dtype=jnp.int32), n_b),
                           jnp.tile(past + jnp.arange(n_new, dtype=jnp.int32), n_seq)])
    cos64, sin64 = _rope_tables(pos)
    cos = jnp.concatenate([cos64, cos64], axis=1)
    sin = jnp.concatenate([sin64, sin64], axis=1)
    cos_t, sin_t = cos64[:n_p].T, sin64[:n_p].T
    x = jnp.concatenate([x_prompt.reshape(n_p, dm), x_sample.reshape(n_s, dm)], axis=0)
    xb = x.astype(BF16)
    proj_tm = max(m * TILE_TOKENS for m in (1, 2, 3) if (n_p + n_s) % (m * TILE_TOKENS) == 0)

    pad_rows = lambda a, n: jnp.pad(a, ((0, 0), (0, n - a.shape[1])) + ((0, 0),) * (a.ndim - 2))
    zeros_b = jnp.zeros((HEAD_DIM, dm), BF16)
    rows_p, rows_s = [], []
    for l in range(depth):
        w = _layer_weights(w_in[l])
        tok_f32, tok_bf = _proj(xb, w["w_tok"], cos, sin, w["flag_tok"], rope=True,
                                tm=proj_tm, tn=TILE_COLS_TOK, name="in_proj_token_major")
        gates = _proj(xb, w["w_gate"], cos, sin, w["flag_tok"], rope=False,
                      tm=proj_tm, tn=TILE_COLS, name="in_proj_gates")
        xb_p = xb[:n_p]
        proj_t = _proj_t(w["flag_feat"], w["w_feat"], xb_p, cos_t, sin_t,
                         tm=2 * TILE_TOKENS, tn=TILE_ROWS_T)
        sq, _ = _proj(xb[n_p:], w["w_sq"], cos[n_p:], sin[n_p:], w["flag_sq"], rope=True,
                      tm=n_s, tn=TILE_ROWS_T, name="in_proj_sample_q")

        out_rows = lambda name, tn: _proj_t(w["flag_" + name], w["w_" + name], xb_p, cos_t, sin_t,
                                            tm=2 * TILE_TOKENS, tn=tn, seq=seq)
        heads_last = lambda a: a.reshape(n_b, N_HEADS, HEAD_DIM, seq).transpose(0, 3, 1, 2)
        rows_p.append((heads_last(out_rows("k", TILE_TOKENS)), heads_last(out_rows("v", TILE_TOKENS)),
                       out_rows("ki", IDX_DIM).transpose(0, 2, 1)))
        tok_s = tok_f32[n_p:].reshape(n_seq, n_new, -1)
        k_s, ki_s = tok_s[..., :QKV_W], tok_s[..., N_OFF_KI:N_OFF_WI]
        v_s = sq[:, T_OFF_V:].reshape(n_seq, n_new, QKV_W)
        rows_s.append((k_s.reshape(n_seq, n_new, N_HEADS, HEAD_DIM),
                       v_s.reshape(n_seq, n_new, N_HEADS, HEAD_DIM), ki_s))

        y_a = _sb_attention(proj_t, tok_bf, n_b=n_b, seq=seq)
        y_b = _dsa_attention(proj_t, tok_bf, tok_f32, n_b=n_b, seq=seq)
        kmean = _block_means(tok_f32, n_p).reshape(n_b, seq // MOBA_BLOCK, QKV_W)
        y_c = _moba_attention(proj_t, tok_bf, kmean, n_b=n_b, seq=seq)
        tok = lambda y: y.transpose(0, 2, 1).reshape(n_p, -1)

        q_s = sq[:, :QKV_W].reshape(n_seq, n_new, N_HEADS, HEAD_DIM)
        q8 = pad_rows(q_s, ROWS_PER_HEAD).transpose(0, 2, 1, 3).astype(BF16)
        qi8 = pad_rows(sq[:, T_OFF_QI:T_OFF_V].reshape(n_seq, n_new, QI_W), ROWS_PER_HEAD).reshape(
            n_seq, ROWS_PER_HEAD * IDX_HEADS, IDX_DIM).astype(BF16)
        wi8 = pad_rows(tok_s[..., N_OFF_WI:N_OFF_WI + IDX_HEADS], ROWS_PER_HEAD).reshape(
            n_seq, ROWS_PER_HEAD * IDX_HEADS, 1)
        new_heads = lambda a: pad_rows(a.reshape(n_seq, n_new, N_HEADS, HEAD_DIM),
                                       PAGE_SIZE).transpose(0, 2, 3, 1).astype(BF16)
        y_s = _sample_attention(
            l, page_table, q8, qi8, wi8, new_heads(k_s), new_heads(v_s),
            pad_rows(ki_s, PAGE_SIZE).transpose(0, 2, 1).astype(BF16),
            ck_t, cv_t, cik_t, n_new=n_new)
        y_s = y_s.reshape(n_s, QKV_W).astype(BF16)

        wa_cols, wb_cols = H_SB * HEAD_DIM, (H_SB + H_DSA) * HEAD_DIM
        blk = DSA_BLOCK_HEADS * HEAD_DIM
        ya = jnp.concatenate([tok(y_a), y_s[:, :wa_cols]], axis=0)
        yb = jnp.concatenate([tok(y_b), y_s[:, wa_cols:wa_cols + blk]], axis=0)
        yc = jnp.concatenate([tok(y_c), y_s[:, QKV_W - blk:]], axis=0)
        w_b = jnp.concatenate([w_br_b[l].astype(BF16), zeros_b], axis=0)
        w_c = jnp.concatenate([zeros_b, w_br_c[l].astype(BF16)], axis=0)
        del wb_cols
        x, xb = _merge(ya, yb, yc, gates, b_gate[l][None, :], w_br_a[l].astype(BF16), w_b, w_c,
                       w_out[l].astype(BF16), x, ln1_g[l][None, :], ln1_b[l][None, :],
                       tm=TILE_TOKENS)
        x, xb = _moe(x, xb, w_router.T, router_bias[:, None], w_exp_gate[l].astype(BF16),
                     w_exp_up[l].astype(BF16), w_exp_down[l].astype(BF16),
                     ln2_g[l][None, :], ln2_b[l][None, :], tm=TILE_TOKENS)

    stack = lambda rows, i: jnp.stack([r[i] for r in rows], 0)
    return (x[:n_p].reshape(n_b, seq, dm), x[n_p:].reshape(n_seq, n_new, dm),
            stack(rows_p, 0), stack(rows_p, 1), stack(rows_p, 2),
            stack(rows_s, 0), stack(rows_s, 1), stack(rows_s, 2))
```

```python
import functools

import jax
import jax.numpy as jnp
from jax import lax
from jax.experimental import pallas as pl
from jax.experimental.pallas import tpu as pltpu

F32 = jnp.float32
BF16 = jnp.bfloat16

HEAD_DIM = 64
H_SB, H_DSA, H_MOBA = 6, 5, 5
N_HEADS = H_SB + H_DSA + H_MOBA
IDX_HEADS = 8
IDX_DIM = 64
DSA_TOPK = 256
MOBA_BLOCK = 256
MOBA_TOPK = 3
N_EXPERTS = 16
EXPERTS_PER_GROUP = 4
N_GROUPS = N_EXPERTS // EXPERTS_PER_GROUP
ROPE_THETA = 10000.0
LN_EPS = 1e-5
DEPTH = 2
ALPHA = (2 * DEPTH) ** 0.25
PAGE_SIZE = 128

LANES = 128
SLAB = 256
PAIR = 2 * HEAD_DIM
NEG = -1e30
INT_MIN = -2 ** 31
VMEM_LIMIT = 56 * 1024 * 1024

QKV_W = N_HEADS * HEAD_DIM
QI_W = IDX_HEADS * IDX_DIM
T_OFF_QI = QKV_W
T_OFF_V = QKV_W + QI_W
T_ROWS = 2 * QKV_W + QI_W
N_OFF_KI = QKV_W
N_OFF_WI = N_OFF_KI + IDX_DIM
N_COLS = QKV_W + LANES
TILE_TOKENS = 512
TILE_COLS = 768
TILE_COLS_TOK = N_COLS // 3
TILE_ROWS_T = 640

_NT = (((1,), (1,)), ((), ()))


def _nt(a, b):
    return lax.dot_general(a, b, _NT, preferred_element_type=F32)


def _dot(a, b):
    return jnp.dot(a, b, preferred_element_type=F32)


def _split_bf16(x):
    hi = x.astype(BF16)
    lo = (x - hi.astype(F32)).astype(BF16)
    return hi, lo


LOG2E = 1.4426950408889634


def _softplus2(z2):
    return jnp.maximum(z2, 0.0) + jnp.log(1.0 + jnp.exp2(-jnp.abs(z2))) * LOG2E


def _params(sem):
    return pltpu.CompilerParams(dimension_semantics=sem, vmem_limit_bytes=VMEM_LIMIT)


def _proj_kernel(x_ref, w_ref, cos_ref, sin_ref, flag_ref, o_ref, ob_ref, *, rope):
    acc = _dot(x_ref[...], w_ref[...])
    if not rope:
        ob_ref[...] = acc.astype(ob_ref.dtype)
        return
    cos = cos_ref[...]
    sin = sin_ref[...]
    lane = lax.broadcasted_iota(jnp.int32, (1, LANES), 1)
    first_half = (lane % HEAD_DIM) < (HEAD_DIM // 2)
    for c in range(acc.shape[1] // LANES):
        sl = slice(c * LANES, (c + 1) * LANES)
        a = acc[:, sl]
        sw = jnp.where(first_half, pltpu.roll(a, LANES - HEAD_DIM // 2, 1),
                       pltpu.roll(a, HEAD_DIM // 2, 1))
        r = jnp.where(flag_ref[:, sl] > 0.0, a * cos + sw * sin, a)
        o_ref[:, sl] = r
        ob_ref[:, sl] = r.astype(BF16)


def _proj(x, w, cos, sin, flag, *, rope, tm, tn, name):
    n, d = x.shape
    width = w.shape[1]
    tile = pl.BlockSpec((tm, tn), lambda j, i: (i, j))
    if rope:
        out_specs = [tile, tile]
        out_shape = [jax.ShapeDtypeStruct((n, width), F32), jax.ShapeDtypeStruct((n, width), BF16)]
        body = functools.partial(_proj_kernel, rope=True)
    else:
        out_specs = tile
        out_shape = jax.ShapeDtypeStruct((n, width), BF16)
        flag = jnp.zeros((1, width), F32)
        body = lambda x_r, w_r, c_r, s_r, f_r, ob_r: _proj_kernel(
            x_r, w_r, c_r, s_r, f_r, None, ob_r, rope=False)
    return pl.pallas_call(
        body,
        grid=(width // tn, n // tm),
        in_specs=[
            pl.BlockSpec((tm, d), lambda j, i: (i, 0)),
            pl.BlockSpec((d, tn), lambda j, i: (0, j)),
            pl.BlockSpec((tm, LANES), lambda j, i: (i, 0)),
            pl.BlockSpec((tm, LANES), lambda j, i: (i, 0)),
            pl.BlockSpec((1, tn), lambda j, i: (0, j)),
        ],
        out_specs=out_specs,
        out_shape=out_shape,
        compiler_params=_params(("parallel", "parallel")),
        name=name,
    )(x, w, cos, sin, flag)


def _proj_t_kernel(flag_ref, w_ref, x_ref, cos_ref, sin_ref, o_ref, *, tn, tm, slabs):
    j = pl.program_id(1)
    acc = _nt(w_ref[...], x_ref[...])
    cos = cos_ref[...]
    sin = sin_ref[...]
    half = HEAD_DIM // 2
    for g in range(tn // HEAD_DIM):
        rows = slice(g * HEAD_DIM, (g + 1) * HEAD_DIM)
        a = acc[rows]
        sw = jnp.concatenate([a[half:], a[:half]], axis=0)
        r = jnp.where(flag_ref[j * (tn // HEAD_DIM) + g] > 0, a * cos + sw * sin, a)
        if slabs:
            for s in range(tm // SLAB):
                o_ref[s, rows, :] = r[:, s * SLAB:(s + 1) * SLAB].astype(o_ref.dtype)
        else:
            o_ref[rows, :] = r


def _proj_t(flags, w_t, x, cos_t, sin_t, *, tm, tn, seq=None):
    n, d = x.shape
    rows = w_t.shape[0]
    if seq is None:
        out_spec = pl.BlockSpec((tm // SLAB, tn, SLAB), lambda i, j, f: (i, j, 0))
        out_shape = jax.ShapeDtypeStruct((n // SLAB, rows, SLAB), BF16)
    else:
        per_seq = seq // tm
        out_spec = pl.BlockSpec((None, tn, tm), lambda i, j, f: (i // per_seq, j, i % per_seq))
        out_shape = jax.ShapeDtypeStruct((n // seq, rows, seq), F32)
    grid_spec = pltpu.PrefetchScalarGridSpec(
        num_scalar_prefetch=1,
        grid=(n // tm, rows // tn),
        in_specs=[
            pl.BlockSpec((tn, d), lambda i, j, f: (j, 0)),
            pl.BlockSpec((tm, d), lambda i, j, f: (i, 0)),
            pl.BlockSpec((HEAD_DIM, tm), lambda i, j, f: (0, i)),
            pl.BlockSpec((HEAD_DIM, tm), lambda i, j, f: (0, i)),
        ],
        out_specs=out_spec,
    )
    return pl.pallas_call(
        functools.partial(_proj_t_kernel, tn=tn, tm=tm, slabs=seq is None),
        grid_spec=grid_spec,
        out_shape=out_shape,
        compiler_params=_params(("parallel", "arbitrary")),
        name="in_proj_feature_major" if seq is None else "in_proj_output_rows",
    )(flags, w_t, x, cos_t, sin_t)


def _head_masks(q_pair):
    first = lax.broadcasted_iota(jnp.int32, (PAIR, 1), 0) < HEAD_DIM
    zero = jnp.zeros_like(q_pair)
    return jnp.where(first, q_pair, zero), jnp.where(first, zero, q_pair)


def _online_update(state, scores, masks, v_t):
    m, l, acc = state
    scores = [jnp.where(mk, s, NEG) for s, mk in zip(scores, masks)]
    m_new = m
    for s in scores:
        m_new = jnp.maximum(m_new, jnp.max(s, axis=0, keepdims=True))
    alpha = jnp.exp2(m - m_new)
    probs = [jnp.exp2(s - m_new) for s in scores]
    l = alpha * l
    for p in probs:
        l = l + jnp.sum(p, axis=0, keepdims=True)
    p_cat = probs[0] if len(probs) == 1 else jnp.concatenate(probs, axis=0)
    acc = alpha * acc + _dot(v_t, p_cat.astype(BF16))
    return m_new, l, acc


def _sb_kernel(q_ref, k_ref, v_ref, tri_ref, o_ref):
    t = SLAB
    i = pl.program_id(2)
    qm = _head_masks(q_ref[...])
    tri2 = tri_ref[...]
    key_i = lax.broadcasted_iota(jnp.int32, (t, t), 0)
    qry_i = lax.broadcasted_iota(jnp.int32, (t, t), 1)
    before = key_i < qry_i

    def tiles(js, carry, diag):
        chains = [(n, hh) for n in range(len(js)) for hh in range(2)]
        ks = [k_ref[pl.ds(pl.multiple_of(j * t, t), t), :] for j in js]
        vs = [v_ref[j] for j in js]
        z = {ch: _dot(ks[ch[0]], qm[ch[1]]) for ch in chains}
        sp = {ch: _softplus2(z[ch]) for ch in chains}
        drop = sp if not diag else {ch: jnp.where(before, sp[ch], 0.0) for ch in chains}
        later = {ch: _dot(tri2, jnp.concatenate(_split_bf16(drop[ch]), axis=0)) for ch in chains}
        out = []
        for hh in range(2):
            c, acc = carry[2 * hh:2 * hh + 2]
            a_all = []
            for n in range(len(js)):
                ch = (n, hh)
                a = jnp.exp2(z[ch] - sp[ch] - later[ch] - c)
                if diag:
                    a = jnp.where(before, a, 0.0)
                a_all.append(a.astype(BF16))
                c = c + jnp.sum(drop[ch], axis=0, keepdims=True)
            v_cat = jnp.concatenate([v[hh * HEAD_DIM:(hh + 1) * HEAD_DIM] for v in vs], axis=1)
            acc = acc + _dot(v_cat, jnp.concatenate(a_all, axis=0))
            out += [c, acc]
        return tuple(out)

    init = (jnp.zeros((1, t), F32), jnp.zeros((HEAD_DIM, t), F32)) * 2
    carry = tiles([i], init, True)
    carry = lax.fori_loop(
        0, i // 2, lambda n, ca: tiles([i - 1 - 2 * n, i - 2 - 2 * n], ca, False), carry)
    carry = lax.cond(i % 2 == 1, lambda ca: tiles([0], ca, False), lambda ca: ca, carry)
    o_ref[...] = jnp.concatenate([carry[1], carry[3]], axis=0).astype(o_ref.dtype)


def _tri_upper(n):
    r = lax.broadcasted_iota(jnp.int32, (n, n), 0)
    c = lax.broadcasted_iota(jnp.int32, (n, n), 1)
    return (c > r).astype(BF16)


def _sb_attention(proj_t, k_tok, *, n_b, seq):
    nq = seq // SLAB
    n_pairs = H_SB // 2
    v_blk0 = T_OFF_V // PAIR
    return pl.pallas_call(
        _sb_kernel,
        grid=(n_b, n_pairs, nq),
        in_specs=[
            pl.BlockSpec((None, PAIR, SLAB), lambda b, p, i: (b * nq + i, p, 0)),
            pl.BlockSpec((seq, PAIR), lambda b, p, i: (b, p)),
            pl.BlockSpec((nq, PAIR, SLAB), lambda b, p, i: (b, v_blk0 + p, 0)),
            pl.BlockSpec((SLAB, 2 * SLAB), lambda b, p, i: (0, 0)),
        ],
        out_specs=pl.BlockSpec((None, PAIR, SLAB), lambda b, p, i: (b * nq + i, p, 0)),
        out_shape=jax.ShapeDtypeStruct((n_b * nq, n_pairs * PAIR, SLAB), BF16),
        compiler_params=_params(("parallel", "parallel", "arbitrary")),
        name="sb_attention",
    )(proj_t, k_tok, proj_t, jnp.concatenate([_tri_upper(SLAB)] * 2, axis=1))


def _sort_key(score):
    bits = lax.bitcast_convert_type(score + 0.0, jnp.int32)
    return bits ^ ((bits >> 31) & jnp.int32(0x7FFFFFFF))


_NEG_INF_KEY = -2139095041


def _kth_largest(count_ge, shape, k, bits_per_step=1):
    def body(it, ans):
        unit = jnp.left_shift(jnp.int32(1), 32 - bits_per_step * (it + 1))
        passed = jnp.zeros(shape, jnp.int32)
        for m in range(1, 2 ** bits_per_step):
            passed = passed + jnp.where(count_ge(ans + m * unit) >= k, 1, 0)
        return ans + passed * unit
    return lax.fori_loop(0, 32 // bits_per_step, body, jnp.full(shape, INT_MIN, jnp.int32))


def _tie_cut(count_eq_below, need, shape, n_bits):
    def body(it, a):
        cand = a + jnp.left_shift(jnp.int32(1), n_bits - 1 - it)
        return jnp.where(count_eq_below(cand) < need, cand, a)
    return lax.fori_loop(0, n_bits, body, jnp.zeros(shape, jnp.int32))


DSA_BLOCK_HEADS = 6


def _dsa_kernel(qi_ref, kiwi_ref, ki_ref, q_ref, k_ref, v_ref, o_ref,
                key_scr, m_scr, l_scr, acc_scr, *, nq, n_keep, n_bits):
    t = SLAB
    i = pl.program_id(1)
    n_slab = i + 1
    qpos = i * t + lax.broadcasted_iota(jnp.int32, (1, t), 1)
    kiota = lax.broadcasted_iota(jnp.int32, (t, 1), 0)
    wi_t = kiwi_ref[...].T

    def scores(j, _):
        kij = ki_ref[pl.ds(pl.multiple_of(j * t, t), t), :][:, :IDX_DIM]
        score = jnp.zeros((t, t), F32)
        for h in range(IDX_HEADS):
            sc = _dot(kij, qi_ref[h * IDX_DIM:(h + 1) * IDX_DIM, :])
            score = score + wi_t[IDX_DIM + h:IDX_DIM + h + 1, :] * jnp.maximum(sc, 0.0)
        causal = (j * t + kiota) <= qpos
        key_scr[j] = _sort_key(jnp.where(causal, score, -jnp.inf))
        return 0

    lax.fori_loop(0, n_slab, scores, 0)

    n_pair = (n_slab + 1) // 2

    @pl.when(n_slab % 2 == 1)
    def _():
        key_scr[n_slab] = jnp.full((t, t), _NEG_INF_KEY, jnp.int32)

    def count(pred):
        def body(n, acc):
            for j in (2 * n, 2 * n + 1):
                ones = jnp.where(pred(key_scr[j], j * t + kiota), 1.0, 0.0)
                acc = acc + jnp.sum(ones.reshape(t // 8, 8, t), axis=0)
            return acc
        acc = lax.fori_loop(0, n_pair, body, jnp.zeros((8, t), F32))
        return jnp.sum(acc, axis=0, keepdims=True)

    thr = _kth_largest(lambda cand: count(lambda key, kp: key >= cand), (1, t), float(n_keep))
    need = float(n_keep) - count(lambda key, kp: key > thr)
    n_eq = count(lambda key, kp: key == thr)
    cut = lax.cond(
        jnp.max(n_eq - need) > 0.0,
        lambda: _tie_cut(lambda cand: count(lambda key, kp: (key == thr) & (kp < cand)),
                         need, (1, t), n_bits),
        lambda: jnp.full((1, t), 2 ** n_bits, jnp.int32))

    m_scr[...] = jnp.full(m_scr.shape, NEG, F32)
    l_scr[...] = jnp.zeros(l_scr.shape, F32)
    acc_scr[...] = jnp.zeros(acc_scr.shape, F32)
    qm = []
    for pr in range(DSA_BLOCK_HEADS // 2):
        qm += list(_head_masks(q_ref[pr * PAIR:(pr + 1) * PAIR, :]))

    def attend(n, _):
        sel, z, vs = [], [], []
        for j in (2 * n, 2 * n + 1):
            key = key_scr[j]
            kp = j * t + kiota
            sel.append(((key > thr) | ((key == thr) & (kp <= cut))) & (kp <= qpos))
            jk = jnp.minimum(j, nq - 1)
            start = pl.multiple_of(jk * t, t)
            vs.append(v_ref[jk])
            kj = [k_ref[pl.ds(start, t), pr * PAIR:(pr + 1) * PAIR]
                  for pr in range((H_DSA + 1) // 2)]
            z.append([_dot(kj[h // 2], qm[h]) for h in range(H_DSA)])
        for h in range(H_DSA):
            state = (m_scr[h], l_scr[h], acc_scr[h])
            v_cat = jnp.concatenate([v[h * HEAD_DIM:(h + 1) * HEAD_DIM] for v in vs], axis=1)
            m, l, acc = _online_update(state, [z[0][h], z[1][h]], sel, v_cat)
            m_scr[h] = m
            l_scr[h] = l
            acc_scr[h] = acc
        return 0

    lax.fori_loop(0, n_pair, attend, 0)
    for h in range(H_DSA):
        o_ref[h * HEAD_DIM:(h + 1) * HEAD_DIM, :] = (acc_scr[h] / l_scr[h]).astype(o_ref.dtype)
    o_ref[H_DSA * HEAD_DIM:, :] = jnp.zeros((HEAD_DIM, t), o_ref.dtype)


def _dsa_attention(proj_t, k_tok, kiwi_f32, *, n_b, seq):
    nq = seq // SLAB
    wide = DSA_BLOCK_HEADS * HEAD_DIM
    n_keep = min(DSA_TOPK, seq // 4)
    n_bits = max(1, (seq - 1).bit_length())
    kiwi_blk = N_OFF_KI // LANES
    return pl.pallas_call(
        functools.partial(_dsa_kernel, nq=nq, n_keep=n_keep, n_bits=n_bits),
        grid=(n_b, nq),
        in_specs=[
            pl.BlockSpec((None, QI_W, SLAB), lambda b, i: (b * nq + i, T_OFF_QI // QI_W, 0)),
            pl.BlockSpec((SLAB, LANES), lambda b, i: (b * nq + i, kiwi_blk)),
            pl.BlockSpec((seq, LANES), lambda b, i: (b, kiwi_blk)),
            pl.BlockSpec((None, wide, SLAB), lambda b, i: (b * nq + i, 1, 0)),
            pl.BlockSpec((seq, wide), lambda b, i: (b, 1)),
            pl.BlockSpec((nq, wide, SLAB), lambda b, i: (b, T_OFF_V // wide + 1, 0)),
        ],
        out_specs=pl.BlockSpec((None, wide, SLAB), lambda b, i: (b * nq + i, 0, 0)),
        out_shape=jax.ShapeDtypeStruct((n_b * nq, wide, SLAB), BF16),
        scratch_shapes=[pltpu.VMEM((nq + 1, SLAB, SLAB), jnp.int32),
                        pltpu.VMEM((H_DSA, 1, SLAB), F32), pltpu.VMEM((H_DSA, 1, SLAB), F32),
                        pltpu.VMEM((H_DSA, HEAD_DIM, SLAB), F32)],
        compiler_params=_params(("parallel", "arbitrary")),
        name="dsa_attention",
    )(proj_t, kiwi_f32, k_tok, proj_t, k_tok, proj_t)


def _block_mean_kernel(k_ref, o_ref):
    o_ref[0] = jnp.mean(k_ref[:, :QKV_W], axis=0, keepdims=True)


def _block_means(k_tok_f32, n_tokens):
    nb = n_tokens // MOBA_BLOCK
    return pl.pallas_call(
        _block_mean_kernel,
        grid=(nb,),
        in_specs=[pl.BlockSpec((MOBA_BLOCK, QKV_W), lambda n: (n, 0))],
        out_specs=pl.BlockSpec((1, 1, QKV_W), lambda n: (n, 0, 0)),
        out_shape=jax.ShapeDtypeStruct((nb, 1, QKV_W), F32),
        compiler_params=_params(("parallel",)),
        name="moba_block_means",
    )(k_tok_f32)


MOBA_FIRST_PAIR = (H_SB + H_DSA) // 2


def _moba_kernel(q_ref, k_ref, v_ref, km_ref, o_ref, sel_scr, *, nb):
    t = MOBA_BLOCK
    i = pl.program_id(2)
    qm = _head_masks(q_ref[...])
    km_hi, km_lo = _split_bf16(km_ref[...])
    n_iota = lax.broadcasted_iota(jnp.int32, (nb, t), 0)
    fi = n_iota.astype(F32)
    valid = n_iota < i
    for hh in range(2):
        gm = jnp.where(valid, _dot(km_hi, qm[hh]) + _dot(km_lo, qm[hh]), -jnp.inf)
        sel = jnp.zeros((nb, t), jnp.bool_)
        for _ in range(min(MOBA_TOPK, nb)):
            m = jnp.max(gm, axis=0, keepdims=True)
            first = jnp.min(jnp.where(gm == m, fi, float(nb)), axis=0, keepdims=True)
            pick = fi == first
            sel = sel | pick
            gm = jnp.where(pick, -jnp.inf, gm)
        sel_scr[hh] = jnp.where(sel & valid, 1.0, 0.0)

    def blocks(ns, carry, mask_of):
        ks = [k_ref[pl.ds(pl.multiple_of(n * t, t), t), :] for n in ns]
        vs = [v_ref[n] for n in ns]
        z = [[_dot(kj, qm[hh]) for kj in ks] for hh in range(2)]
        out = []
        for hh in range(2):
            v_cat = jnp.concatenate([v[hh * HEAD_DIM:(hh + 1) * HEAD_DIM] for v in vs], axis=1)
            out += list(_online_update(carry[3 * hh:3 * hh + 3], z[hh],
                                       [mask_of(hh, n) for n in ns], v_cat))
        return tuple(out)

    key_i = lax.broadcasted_iota(jnp.int32, (t, t), 0)
    qry_i = lax.broadcasted_iota(jnp.int32, (t, t), 1)
    chosen = lambda hh, n: sel_scr[hh, pl.ds(n, 1), :] > 0.0
    init = (jnp.full((1, t), NEG, F32), jnp.zeros((1, t), F32), jnp.zeros((HEAD_DIM, t), F32)) * 2
    carry = blocks([i], init, lambda hh, n: key_i <= qry_i)
    carry = lax.fori_loop(0, i // 2, lambda n, ca: blocks([2 * n, 2 * n + 1], ca, chosen), carry)
    carry = lax.cond(i % 2 == 1, lambda ca: blocks([i - 1], ca, chosen), lambda ca: ca, carry)
    o_ref[...] = jnp.concatenate([carry[2] / carry[1], carry[5] / carry[4]],
                                 axis=0).astype(o_ref.dtype)


def _moba_attention(proj_t, k_tok, kmean, *, n_b, seq):
    nq = seq // SLAB
    nb = seq // MOBA_BLOCK
    n_pairs = N_HEADS // 2 - MOBA_FIRST_PAIR
    p0 = MOBA_FIRST_PAIR
    v_blk0 = T_OFF_V // PAIR + p0
    return pl.pallas_call(
        functools.partial(_moba_kernel, nb=nb),
        grid=(n_b, n_pairs, nq),
        in_specs=[
            pl.BlockSpec((None, PAIR, SLAB), lambda b, p, i: (b * nq + i, p0 + p, 0)),
            pl.BlockSpec((seq, PAIR), lambda b, p, i: (b, p0 + p)),
            pl.BlockSpec((nq, PAIR, SLAB), lambda b, p, i: (b, v_blk0 + p, 0)),
            pl.BlockSpec((None, nb, PAIR), lambda b, p, i: (b, 0, p0 + p)),
        ],
        out_specs=pl.BlockSpec((None, PAIR, SLAB), lambda b, p, i: (b * nq + i, p, 0)),
        out_shape=jax.ShapeDtypeStruct((n_b * nq, n_pairs * PAIR, SLAB), BF16),
        scratch_shapes=[pltpu.VMEM((2, nb, SLAB), F32)],
        compiler_params=_params(("parallel", "parallel", "arbitrary")),
        name="moba_attention",
    )(proj_t, k_tok, proj_t, kmean)


ROWS_PER_HEAD = 8


def _top_mask(cols, n_top):
    out = []
    for n, g in enumerate(cols):
        rank = jnp.zeros_like(g)
        for m, o in enumerate(cols):
            if m == n:
                continue
            ahead = (o >= g) if m < n else (o > g)
            rank = rank + jnp.where(ahead, 1.0, 0.0)
        out.append(rank < float(n_top))
    return out


def _sample_mixers(s_scr, p_scr, ki_scr, qi_ref, wi_ref, *, past, n_keep, n_bits):
    lk_len = past + PAGE_SIZE
    r8 = ROWS_PER_HEAD
    kpos = lax.broadcasted_iota(jnp.int32, (1, lk_len), 1)

    def qpos(rows):
        return past + lax.broadcasted_iota(jnp.int32, (rows, 1), 0) % r8

    ra = H_SB * r8
    z = s_scr[:ra, :]
    before = kpos < qpos(ra)
    sp = _softplus2(z)
    drop = jnp.where(before, sp, 0.0)
    hi, lo = _split_bf16(drop)
    tri_r = lax.broadcasted_iota(jnp.int32, (2 * LANES, LANES), 0) % LANES
    tri_c = lax.broadcasted_iota(jnp.int32, (2 * LANES, LANES), 1)
    tri2 = (tri_r > tri_c).astype(BF16)
    c = jnp.zeros((ra, 1), F32)
    for blk in reversed(range(lk_len // LANES)):
        sl = slice(blk * LANES, (blk + 1) * LANES)
        later = _dot(jnp.concatenate([hi[:, sl], lo[:, sl]], axis=1), tri2) + c
        a = jnp.where(before[:, sl], jnp.exp2(z[:, sl] - sp[:, sl] - later), 0.0)
        p_scr[:ra, sl] = a.astype(BF16)
        c = c + jnp.sum(drop[:, sl], axis=1, keepdims=True)

    sc = jnp.maximum(_dot(qi_ref[0], ki_scr[...]), 0.0) * wi_ref[0]
    score = jnp.sum(sc.reshape(r8, IDX_HEADS, lk_len), axis=1)
    causal8 = kpos <= qpos(r8)
    key = _sort_key(jnp.where(causal8, score, -jnp.inf))

    def count(pred):
        return jnp.sum(jnp.where(pred, 1.0, 0.0), axis=1, keepdims=True)

    thr = _kth_largest(lambda cand: count(key >= cand), (r8, 1), float(n_keep), bits_per_step=2)
    need = float(n_keep) - count(key > thr)
    n_eq = count(key == thr)
    cut = lax.cond(
        jnp.max(n_eq - need) > 0.0,
        lambda: _tie_cut(lambda cand: count((key == thr) & (kpos < cand)), need, (r8, 1), n_bits),
        lambda: jnp.full((r8, 1), 2 ** n_bits, jnp.int32))
    sel8 = ((key > thr) | ((key == thr) & (kpos <= cut))) & causal8
    rb = H_DSA * r8
    sel = jnp.concatenate([jnp.where(sel8, 1.0, 0.0)] * H_DSA, axis=0) > 0.0
    s = jnp.where(sel, s_scr[ra:ra + rb, :], NEG)
    p = jnp.where(sel, jnp.exp2(s - jnp.max(s, axis=1, keepdims=True)), 0.0)
    l_b = jnp.sum(p, axis=1, keepdims=True)
    p_scr[ra:ra + rb, :] = p.astype(BF16)

    rc = H_MOBA * r8
    sm = s_scr[ra + rb:, :]
    n_blk = past // MOBA_BLOCK
    gate = [jnp.sum(sm[:, n * MOBA_BLOCK:(n + 1) * MOBA_BLOCK], axis=1, keepdims=True)
            for n in range(n_blk)]
    chosen = _top_mask(gate, min(MOBA_TOPK, n_blk + 1))
    own_ok = kpos[:, past:] <= qpos(rc)
    pieces = [jnp.where(chosen[n], sm[:, n * MOBA_BLOCK:(n + 1) * MOBA_BLOCK], NEG)
              for n in range(n_blk)]
    pieces.append(jnp.where(own_ok, sm[:, past:], NEG))
    m_c = pieces[0].max(axis=1, keepdims=True)
    for pc in pieces[1:]:
        m_c = jnp.maximum(m_c, pc.max(axis=1, keepdims=True))
    l_c = jnp.zeros((rc, 1), F32)
    off = 0
    for pc in pieces:
        e = jnp.where(pc > 0.5 * NEG, jnp.exp2(pc - m_c), 0.0)
        l_c = l_c + jnp.sum(e, axis=1, keepdims=True)
        p_scr[ra + rb:, off:off + pc.shape[1]] = e.astype(BF16)
        off += pc.shape[1]

    return jnp.concatenate([jnp.ones((ra, 1), F32), 1.0 / l_b, 1.0 / l_c], axis=0)


def _sample_kernel(pt_ref, q_ref, qi_ref, wi_ref, knew_ref, vnew_ref, kinew_ref, *rest,
                   n_pages, n_keep, n_bits):
    del pt_ref
    k_pages, v_pages = rest[:n_pages], rest[n_pages:2 * n_pages]
    i_pages = rest[2 * n_pages:3 * n_pages]
    o_ref = rest[3 * n_pages]
    k_scr, v_scr, ki_scr, s_scr, p_scr = rest[3 * n_pages + 1:]
    past = n_pages * PAGE_SIZE
    r8 = ROWS_PER_HEAD

    for pg in range(n_pages):
        cols = slice(pg * PAGE_SIZE, (pg + 1) * PAGE_SIZE)
        k_scr[:, :, cols] = k_pages[pg][...].astype(BF16)
        v_scr[:, :, cols] = v_pages[pg][...].astype(BF16)
        ki_scr[:, cols] = i_pages[pg][...].astype(BF16)
    k_scr[:, :, past:] = knew_ref[0]
    v_scr[:, :, past:] = vnew_ref[0]
    ki_scr[:, past:] = kinew_ref[0]
    for h in range(N_HEADS):
        s_scr[h * r8:(h + 1) * r8, :] = _dot(q_ref[0, h], k_scr[h])
    inv = _sample_mixers(s_scr, p_scr, ki_scr, qi_ref, wi_ref, past=past, n_keep=n_keep,
                         n_bits=n_bits)
    for h in range(N_HEADS):
        rows = slice(h * r8, (h + 1) * r8)
        o_ref[0, :, h * HEAD_DIM:(h + 1) * HEAD_DIM] = _nt(p_scr[rows, :], v_scr[h]) * inv[rows]


def _sample_attention(layer, page_table, q8, qi8, wi8, knew, vnew, kinew,
                      cache_k, cache_v, cache_ik, *, n_new):
    n_seq, n_pages = page_table.shape
    past = n_pages * PAGE_SIZE
    lk_len = past + PAGE_SIZE
    n_keep = min(DSA_TOPK, (past + n_new) // 4)
    n_bits = max(1, (lk_len - 1).bit_length())
    rows = N_HEADS * ROWS_PER_HEAD

    def page_spec(j, *dims):
        return pl.BlockSpec((None, None) + dims + (PAGE_SIZE,),
                            lambda b, pt: (layer, pt[b, j]) + (0,) * (len(dims) + 1))

    def seq_spec(*dims):
        return pl.BlockSpec((1,) + dims, lambda b, pt: (b,) + (0,) * len(dims))

    in_specs = [seq_spec(N_HEADS, ROWS_PER_HEAD, HEAD_DIM),
                seq_spec(ROWS_PER_HEAD * IDX_HEADS, IDX_DIM),
                seq_spec(ROWS_PER_HEAD * IDX_HEADS, 1),
                seq_spec(N_HEADS, HEAD_DIM, PAGE_SIZE), seq_spec(N_HEADS, HEAD_DIM, PAGE_SIZE),
                seq_spec(IDX_DIM, PAGE_SIZE)]
    in_specs += [page_spec(j, N_HEADS, HEAD_DIM) for j in range(n_pages)]
    in_specs += [page_spec(j, N_HEADS, HEAD_DIM) for j in range(n_pages)]
    in_specs += [page_spec(j, IDX_DIM) for j in range(n_pages)]
    grid_spec = pltpu.PrefetchScalarGridSpec(
        num_scalar_prefetch=1,
        grid=(n_seq,),
        in_specs=in_specs,
        out_specs=pl.BlockSpec((1, ROWS_PER_HEAD, QKV_W), lambda b, pt: (b, 0, 0)),
        scratch_shapes=[pltpu.VMEM((N_HEADS, HEAD_DIM, lk_len), BF16),
                        pltpu.VMEM((N_HEADS, HEAD_DIM, lk_len), BF16),
                        pltpu.VMEM((IDX_DIM, lk_len), BF16),
                        pltpu.VMEM((rows, lk_len), F32), pltpu.VMEM((rows, lk_len), BF16)],
    )
    out = pl.pallas_call(
        functools.partial(_sample_kernel, n_pages=n_pages, n_keep=n_keep, n_bits=n_bits),
        grid_spec=grid_spec,
        out_shape=jax.ShapeDtypeStruct((n_seq, ROWS_PER_HEAD, QKV_W), F32),
        compiler_params=_params(("arbitrary",)),
        name="sample_attention",
    )(page_table, q8, qi8, wi8, knew, vnew, kinew,
      *([cache_k] * n_pages), *([cache_v] * n_pages), *([cache_ik] * n_pages))
    return out[:, :n_new]


def _layer_norm(h, g, b):
    mu = jnp.mean(h, axis=-1, keepdims=True)
    d = h - mu
    var = jnp.mean(d * d, axis=-1, keepdims=True)
    return d * lax.rsqrt(var + LN_EPS) * g + b


def _merge_kernel(ya_ref, yb_ref, yc_ref, g_ref, bg_ref, wa_ref, wb_ref, wc_ref, wo_ref,
                  x_ref, lg_ref, lb_ref, o_ref, ob_ref):
    dm = x_ref.shape[1]
    merged = jnp.zeros(x_ref.shape, F32)
    for n, (y_ref, w_ref) in enumerate(((ya_ref, wa_ref), (yb_ref, wb_ref), (yc_ref, wc_ref))):
        sl = slice(n * dm, (n + 1) * dm)
        gate = jax.nn.sigmoid(g_ref[:, sl].astype(F32) + bg_ref[:, sl])
        merged = merged + gate * _dot(y_ref[...], w_ref[...])
    mix = _dot(merged.astype(BF16), wo_ref[...])
    y = _layer_norm(ALPHA * x_ref[...] + mix, lg_ref[...], lb_ref[...])
    o_ref[...] = y
    ob_ref[...] = y.astype(BF16)


def _merge(ya, yb, yc, g, bg, wa, wb, wc, wo, x, lg, lb, *, tm):
    n, dm = x.shape
    row = lambda w: pl.BlockSpec((tm, w), lambda i: (i, 0))
    full = lambda a: pl.BlockSpec(a.shape, lambda i: (0, 0))
    return pl.pallas_call(
        _merge_kernel,
        grid=(n // tm,),
        in_specs=[row(ya.shape[1]), row(yb.shape[1]), row(yc.shape[1]), row(g.shape[1]),
                  full(bg), full(wa), full(wb), full(wc), full(wo), row(dm), full(lg), full(lb)],
        out_specs=[row(dm), row(dm)],
        out_shape=[jax.ShapeDtypeStruct((n, dm), F32), jax.ShapeDtypeStruct((n, dm), BF16)],
        compiler_params=_params(("parallel",)),
        name="merge_out_ln",
    )(ya, yb, yc, g, bg, wa, wb, wc, wo, x, lg, lb)


def _route(x, xb, wr_ref, rb_ref):
    w_hi, w_lo = _split_bf16(wr_ref[...])
    x_lo = (x - xb.astype(F32)).astype(BF16)
    logits = _nt(w_hi, xb) + _nt(w_hi, x_lo) + _nt(w_lo, xb)
    s = jax.nn.sigmoid(logits)
    sb = s + rb_ref[...]
    s_col = [s[e:e + 1, :] for e in range(N_EXPERTS)]
    b_col = [sb[e:e + 1, :] for e in range(N_EXPERTS)]
    best_score = None
    for gi in range(N_GROUPS):
        a, b, c, d = b_col[gi * EXPERTS_PER_GROUP:(gi + 1) * EXPERTS_PER_GROUP]
        m1, n1 = jnp.maximum(a, b), jnp.minimum(a, b)
        m2, n2 = jnp.maximum(c, d), jnp.minimum(c, d)
        score = jnp.maximum(m1, m2) + jnp.maximum(jnp.minimum(m1, m2), jnp.maximum(n1, n2))
        if gi == 0:
            best_score, best = score, jnp.zeros_like(score)
        else:
            better = score > best_score
            best = jnp.where(better, float(gi), best)
            best_score = jnp.where(better, score, best_score)
    picked = []
    for gi in range(N_GROUPS):
        cols = b_col[gi * EXPERTS_PER_GROUP:(gi + 1) * EXPERTS_PER_GROUP]
        top = _top_mask(cols, 2)
        in_group = best == float(gi)
        for j in range(EXPERTS_PER_GROUP):
            picked.append(jnp.where(in_group & top[j], s_col[gi * EXPERTS_PER_GROUP + j], 0.0))
    total = picked[0]
    for c in picked[1:]:
        total = total + c
    return jnp.concatenate([p / total for p in picked], axis=0)


MOE_CHUNK = 128


def _moe_kernel(x_ref, xb_ref, wr_ref, rb_ref, tri_ref, wg_ref, wu_ref, wd_ref, lg_ref, lb_ref,
                o_ref, ob_ref, comb_t_ref, pos_t_ref, comb_c_ref, pos_c_ref, cnt_ref, acc_ref):
    e = pl.program_id(1)
    tm = x_ref.shape[0]

    @pl.when(e == 0)
    def _():
        comb_t = _route(x_ref[...], xb_ref[...], wr_ref, rb_ref)
        routed = jnp.where(comb_t > 0.0, 1.0, 0.0)
        pos_t = _dot(routed.astype(BF16), tri_ref[...])
        comb_t_ref[...] = comb_t
        pos_t_ref[...] = pos_t
        pad = jnp.zeros((LANES - N_EXPERTS, tm), F32)
        comb_c_ref[...] = jnp.concatenate([comb_t, pad], axis=0).T
        pos_c_ref[...] = jnp.concatenate([pos_t, pad], axis=0).T
        for ex in range(N_EXPERTS):
            cnt_ref[ex] = jnp.sum(routed[ex:ex + 1, :]).astype(jnp.int32)
        acc_ref[...] = jnp.zeros_like(acc_ref)

    xb = xb_ref[...]
    comb_row = comb_t_ref[pl.ds(e, 1), :]
    pos_row = pos_t_ref[pl.ds(e, 1), :]
    lane = lax.broadcasted_iota(jnp.int32, comb_c_ref.shape, 1)
    column = lambda ref: jnp.sum(jnp.where(lane == e, ref[...], 0.0), axis=1, keepdims=True)
    comb_col, pos_col = column(comb_c_ref), column(pos_c_ref)
    row_iota = lax.broadcasted_iota(jnp.int32, (MOE_CHUNK, 1), 0).astype(F32)
    lane_iota = lax.broadcasted_iota(jnp.int32, (1, MOE_CHUNK), 1).astype(F32)

    def chunk(c, _):
        base = (c * MOE_CHUNK).astype(F32)
        pick = jnp.where((pos_row == row_iota + base) & (comb_row > 0.0), 1.0, 0.0)
        weight = jnp.sum(pick * comb_row, axis=1, keepdims=True)
        xg = _dot(pick.astype(BF16), xb).astype(BF16)
        gate = _dot(xg, wg_ref[0])
        h = gate * jax.nn.sigmoid(gate) * _dot(xg, wu_ref[0]) * weight
        out = _dot(h.astype(BF16), wd_ref[0]).astype(BF16)
        back = jnp.where((pos_col == lane_iota + base) & (comb_col > 0.0), 1.0, 0.0).astype(BF16)
        acc_ref[...] += _dot(back, out)
        return 0

    lax.fori_loop(0, (cnt_ref[e] + MOE_CHUNK - 1) // MOE_CHUNK, chunk, 0)

    @pl.when(e == pl.num_programs(1) - 1)
    def _():
        y = _layer_norm(ALPHA * x_ref[...] + acc_ref[...], lg_ref[...], lb_ref[...])
        o_ref[...] = y
        ob_ref[...] = y.astype(BF16)


def _moe(x, xb, wr, rb, wg, wu, wd, lg, lb, *, tm):
    n, dm = x.shape
    n_e, _, de = wg.shape
    row = pl.BlockSpec((tm, dm), lambda i, e: (i, 0))
    full = lambda a: pl.BlockSpec(a.shape, lambda i, e: (0, 0))
    r = lax.broadcasted_iota(jnp.int32, (tm, tm), 0)
    c = lax.broadcasted_iota(jnp.int32, (tm, tm), 1)
    tri = (r < c).astype(BF16)
    return pl.pallas_call(
        _moe_kernel,
        grid=(n // tm, n_e),
        in_specs=[row, row, full(wr), full(rb), full(tri),
                  pl.BlockSpec((1, dm, de), lambda i, e: (e, 0, 0)),
                  pl.BlockSpec((1, dm, de), lambda i, e: (e, 0, 0)),
                  pl.BlockSpec((1, de, dm), lambda i, e: (e, 0, 0)),
                  full(lg), full(lb)],
        out_specs=[row, row],
        out_shape=[jax.ShapeDtypeStruct((n, dm), F32), jax.ShapeDtypeStruct((n, dm), BF16)],
        scratch_shapes=[pltpu.VMEM((n_e, tm), F32), pltpu.VMEM((n_e, tm), F32),
                        pltpu.VMEM((tm, LANES), F32), pltpu.VMEM((tm, LANES), F32),
                        pltpu.SMEM((n_e,), jnp.int32), pltpu.VMEM((tm, dm), F32)],
        compiler_params=_params(("parallel", "arbitrary")),
        name="moe_ln",
    )(x, xb, wr, rb, tri, wg, wu, wd, lg, lb)


def _pad_cols(a, width):
    return jnp.pad(a, ((0, 0), (0, width - a.shape[1])))


def _layer_weights(w_in_l):
    scale = HEAD_DIM ** -0.5 * LOG2E
    cuts = [QKV_W, 2 * QKV_W, 3 * QKV_W, 3 * QKV_W + QI_W, 3 * QKV_W + QI_W + IDX_DIM,
            3 * QKV_W + QI_W + IDX_DIM + IDX_HEADS]
    wq, wk, wv, wqi, wki, wwi, wg = jnp.split(w_in_l, cuts, axis=1)
    w_tok = _pad_cols(jnp.concatenate([wk, wki, wwi], axis=1), N_COLS)
    col = jnp.arange(N_COLS)
    flag_tok = ((col < QKV_W) & (col // HEAD_DIM >= H_SB)) | ((col >= N_OFF_KI) & (col < N_OFF_WI))
    w_feat = jnp.concatenate([wq * scale, wqi, wv], axis=1).T
    grp = jnp.arange(T_ROWS // HEAD_DIM)
    flag_feat = ((grp >= H_SB) & (grp < N_HEADS)) | ((grp >= T_OFF_QI // HEAD_DIM)
                                                     & (grp < T_OFF_V // HEAD_DIM))
    w_sq = jnp.concatenate([wq * scale, wqi, wv], axis=1)
    col_s = jnp.arange(T_ROWS)
    flag_sq = (col_s // HEAD_DIM >= H_SB) & (col_s < T_OFF_V)
    head = jnp.arange(N_HEADS)
    f = lambda a: a.astype(F32)[None, :]
    i32 = lambda a: a.astype(jnp.int32)
    return dict(w_tok=w_tok.astype(BF16), flag_tok=f(flag_tok),
                w_feat=w_feat.astype(BF16), flag_feat=i32(flag_feat),
                w_sq=w_sq.astype(BF16), flag_sq=f(flag_sq), w_gate=wg.astype(BF16),
                w_k=wk.T.astype(BF16), flag_k=i32(head >= H_SB),
                w_v=wv.T.astype(BF16), flag_v=i32(head < 0),
                w_ki=wki.T.astype(BF16), flag_ki=jnp.ones((1,), jnp.int32))


def _rope_tables(pos):
    half = HEAD_DIM // 2
    inv = ROPE_THETA ** (-jnp.arange(half, dtype=F32) / half)
    ang = pos.astype(F32)[:, None] * inv[None, :]
    cos, sin = jnp.cos(ang), jnp.sin(ang)
    return jnp.concatenate([cos, cos], axis=1), jnp.concatenate([-sin, sin], axis=1)


def kernel(x_prompt, x_sample, cache_k, cache_v, cache_idx_k, page_table, w_in, w_br_a, w_br_b,
           w_br_c, b_gate, w_out, ln1_g, ln1_b, ln2_g, ln2_b, w_router, router_bias,
           w_exp_gate, w_exp_up, w_exp_down):
    n_b, seq, dm = x_prompt.shape
    n_seq, n_new, _ = x_sample.shape
    depth = w_in.shape[0]
    n_p = n_b * seq
    n_s = n_seq * n_new
    n_pages = page_table.shape[1]
    past = n_pages * PAGE_SIZE
    assert seq % SLAB == 0 and past % MOBA_BLOCK == 0 and n_new <= ROWS_PER_HEAD
    ck_t = cache_k.transpose(0, 1, 3, 4, 2)
    cv_t = cache_v.transpose(0, 1, 3, 4, 2)
    cik_t = cache_idx_k.transpose(0, 1, 3, 2)

    pos = jnp.concatenate([jnp.tile(jnp.arange(seq, dtype=jnp.int32), n_b),
                           jnp.tile(past + jnp.arange(n_new, dtype=jnp.int32), n_seq)])
    cos64, sin64 = _rope_tables(pos)
    cos = jnp.concatenate([cos64, cos64], axis=1)
    sin = jnp.concatenate([sin64, sin64], axis=1)
    cos_t, sin_t = cos64[:n_p].T, sin64[:n_p].T
    x = jnp.concatenate([x_prompt.reshape(n_p, dm), x_sample.reshape(n_s, dm)], axis=0)
    xb = x.astype(BF16)
    proj_tm = max(m * TILE_TOKENS for m in (1, 2, 3) if (n_p + n_s) % (m * TILE_TOKENS) == 0)
    moe_tm = 3 * TILE_TOKENS // 2 if (n_p + n_s) % (3 * TILE_TOKENS // 2) == 0 else TILE_TOKENS

    pad_rows = lambda a, n: jnp.pad(a, ((0, 0), (0, n - a.shape[1])) + ((0, 0),) * (a.ndim - 2))
    zeros_b = jnp.zeros((HEAD_DIM, dm), BF16)
    rows_p, rows_s = [], []
    for l in range(depth):
        w = _layer_weights(w_in[l])
        tok_f32, tok_bf = _proj(xb, w["w_tok"], cos, sin, w["flag_tok"], rope=True,
                                tm=proj_tm, tn=TILE_COLS_TOK, name="in_proj_token_major")
        gates = _proj(xb, w["w_gate"], cos, sin, w["flag_tok"], rope=False,
                      tm=proj_tm, tn=TILE_COLS, name="in_proj_gates")
        xb_p = xb[:n_p]
        proj_t = _proj_t(w["flag_feat"], w["w_feat"], xb_p, cos_t, sin_t,
                         tm=2 * TILE_TOKENS, tn=TILE_ROWS_T)
        sq, _ = _proj(xb[n_p:], w["w_sq"], cos[n_p:], sin[n_p:], w["flag_sq"], rope=True,
                      tm=n_s, tn=TILE_ROWS_T, name="in_proj_sample_q")

        out_rows = lambda name, tn: _proj_t(w["flag_" + name], w["w_" + name], xb_p, cos_t, sin_t,
                                            tm=2 * TILE_TOKENS, tn=tn, seq=seq)
        heads_last = lambda a: a.reshape(n_b, N_HEADS, HEAD_DIM, seq).transpose(0, 3, 1, 2)
        rows_p.append((heads_last(out_rows("k", TILE_TOKENS)), heads_last(out_rows("v", TILE_TOKENS)),
                       out_rows("ki", IDX_DIM).transpose(0, 2, 1)))
        tok_s = tok_f32[n_p:].reshape(n_seq, n_new, -1)
        k_s, ki_s = tok_s[..., :QKV_W], tok_s[..., N_OFF_KI:N_OFF_WI]
        v_s = sq[:, T_OFF_V:].reshape(n_seq, n_new, QKV_W)
        rows_s.append((k_s.reshape(n_seq, n_new, N_HEADS, HEAD_DIM),
                       v_s.reshape(n_seq, n_new, N_HEADS, HEAD_DIM), ki_s))

        y_a = _sb_attention(proj_t, tok_bf, n_b=n_b, seq=seq)
        y_b = _dsa_attention(proj_t, tok_bf, tok_f32, n_b=n_b, seq=seq)
        kmean = _block_means(tok_f32, n_p).reshape(n_b, seq // MOBA_BLOCK, QKV_W)
        y_c = _moba_attention(proj_t, tok_bf, kmean, n_b=n_b, seq=seq)
        tok = lambda y: y.transpose(0, 2, 1).reshape(n_p, -1)

        q_s = sq[:, :QKV_W].reshape(n_seq, n_new, N_HEADS, HEAD_DIM)
        q8 = pad_rows(q_s, ROWS_PER_HEAD).transpose(0, 2, 1, 3).astype(BF16)
        qi8 = pad_rows(sq[:, T_OFF_QI:T_OFF_V].reshape(n_seq, n_new, QI_W), ROWS_PER_HEAD).reshape(
            n_seq, ROWS_PER_HEAD * IDX_HEADS, IDX_DIM).astype(BF16)
        wi8 = pad_rows(tok_s[..., N_OFF_WI:N_OFF_WI + IDX_HEADS], ROWS_PER_HEAD).reshape(
            n_seq, ROWS_PER_HEAD * IDX_HEADS, 1)
        new_heads = lambda a: pad_rows(a.reshape(n_seq, n_new, N_HEADS, HEAD_DIM),
                                       PAGE_SIZE).transpose(0, 2, 3, 1).astype(BF16)
        y_s = _sample_attention(
            l, page_table, q8, qi8, wi8, new_heads(k_s), new_heads(v_s),
            pad_rows(ki_s, PAGE_SIZE).transpose(0, 2, 1).astype(BF16),
            ck_t, cv_t, cik_t, n_new=n_new)
        y_s = y_s.reshape(n_s, QKV_W).astype(BF16)

        wa_cols, wb_cols = H_SB * HEAD_DIM, (H_SB + H_DSA) * HEAD_DIM
        blk = DSA_BLOCK_HEADS * HEAD_DIM
        ya = jnp.concatenate([tok(y_a), y_s[:, :wa_cols]], axis=0)
        yb = jnp.concatenate([tok(y_b), y_s[:, wa_cols:wa_cols + blk]], axis=0)
        yc = jnp.concatenate([tok(y_c), y_s[:, QKV_W - blk:]], axis=0)
        w_b = jnp.concatenate([w_br_b[l].astype(BF16), zeros_b], axis=0)
        w_c = jnp.concatenate([zeros_b, w_br_c[l].astype(BF16)], axis=0)
        del wb_cols
        x, xb = _merge(ya, yb, yc, gates, b_gate[l][None, :], w_br_a[l].astype(BF16), w_b, w_c,
                       w_out[l].astype(BF16), x, ln1_g[l][None, :], ln1_b[l][None, :],
                       tm=TILE_TOKENS)
        x, xb = _moe(x, xb, w_router.T, router_bias[:, None], w_exp_gate[l].astype(BF16),
                     w_exp_up[l].astype(BF16), w_exp_down[l].astype(BF16),
                     ln2_g[l][None, :], ln2_b[l][None, :], tm=moe_tm)

    stack = lambda rows, i: jnp.stack([r[i] for r in rows], 0)
    return (x[:n_p].reshape(n_b, seq, dm), x[n_p:].reshape(n_seq, n_new, dm),
            stack(rows_p, 0), stack(rows_p, 1), stack(rows_p, 2),
            stack(rows_s, 0), stack(rows_s, 1), stack(rows_s, 2))
```

```python
import functools

import jax
import jax.numpy as jnp
from jax import lax
from jax.experimental import pallas as pl
from jax.experimental.pallas import tpu as pltpu

F32 = jnp.float32
BF16 = jnp.bfloat16

HEAD_DIM = 64
H_SB, H_DSA, H_MOBA = 6, 5, 5
N_HEADS = H_SB + H_DSA + H_MOBA
IDX_HEADS = 8
IDX_DIM = 64
DSA_TOPK = 256
MOBA_BLOCK = 256
MOBA_TOPK = 3
N_EXPERTS = 16
EXPERTS_PER_GROUP = 4
N_GROUPS = N_EXPERTS // EXPERTS_PER_GROUP
ROPE_THETA = 10000.0
LN_EPS = 1e-5
DEPTH = 2
ALPHA = (2 * DEPTH) ** 0.25
PAGE_SIZE = 128

LANES = 128
SLAB = 256
PAIR = 2 * HEAD_DIM
NEG = -1e30
INT_MIN = -2 ** 31
VMEM_LIMIT = 56 * 1024 * 1024

QKV_W = N_HEADS * HEAD_DIM
QI_W = IDX_HEADS * IDX_DIM
T_OFF_QI = QKV_W
T_OFF_V = QKV_W + QI_W
T_ROWS = 2 * QKV_W + QI_W
N_OFF_KI = QKV_W
N_OFF_WI = N_OFF_KI + IDX_DIM
N_COLS = QKV_W + LANES
TILE_TOKENS = 512
TILE_COLS = 768
TILE_COLS_TOK = N_COLS // 3
TILE_ROWS_T = 640

_NT = (((1,), (1,)), ((), ()))


def _nt(a, b):
    return lax.dot_general(a, b, _NT, preferred_element_type=F32)


def _dot(a, b):
    return jnp.dot(a, b, preferred_element_type=F32)


def _split_bf16(x):
    hi = x.astype(BF16)
    lo = (x - hi.astype(F32)).astype(BF16)
    return hi, lo


LOG2E = 1.4426950408889634


def _softplus2(z2):
    return jnp.maximum(z2, 0.0) + jnp.log(1.0 + jnp.exp2(-jnp.abs(z2))) * LOG2E


def _params(sem):
    return pltpu.CompilerParams(dimension_semantics=sem, vmem_limit_bytes=VMEM_LIMIT)


def _proj_kernel(x_ref, w_ref, cos_ref, sin_ref, flag_ref, o_ref, ob_ref, *, rope):
    acc = _dot(x_ref[...], w_ref[...])
    if not rope:
        ob_ref[...] = acc.astype(ob_ref.dtype)
        return
    cos = cos_ref[...]
    sin = sin_ref[...]
    lane = lax.broadcasted_iota(jnp.int32, (1, LANES), 1)
    first_half = (lane % HEAD_DIM) < (HEAD_DIM // 2)
    for c in range(acc.shape[1] // LANES):
        sl = slice(c * LANES, (c + 1) * LANES)
        a = acc[:, sl]
        sw = jnp.where(first_half, pltpu.roll(a, LANES - HEAD_DIM // 2, 1),
                       pltpu.roll(a, HEAD_DIM // 2, 1))
        r = jnp.where(flag_ref[:, sl] > 0.0, a * cos + sw * sin, a)
        o_ref[:, sl] = r
        ob_ref[:, sl] = r.astype(BF16)


def _proj(x, w, cos, sin, flag, *, rope, tm, tn, name):
    n, d = x.shape
    width = w.shape[1]
    tile = pl.BlockSpec((tm, tn), lambda j, i: (i, j))
    if rope:
        out_specs = [tile, tile]
        out_shape = [jax.ShapeDtypeStruct((n, width), F32), jax.ShapeDtypeStruct((n, width), BF16)]
        body = functools.partial(_proj_kernel, rope=True)
    else:
        out_specs = tile
        out_shape = jax.ShapeDtypeStruct((n, width), BF16)
        flag = jnp.zeros((1, width), F32)
        body = lambda x_r, w_r, c_r, s_r, f_r, ob_r: _proj_kernel(
            x_r, w_r, c_r, s_r, f_r, None, ob_r, rope=False)
    return pl.pallas_call(
        body,
        grid=(width // tn, n // tm),
        in_specs=[
            pl.BlockSpec((tm, d), lambda j, i: (i, 0)),
            pl.BlockSpec((d, tn), lambda j, i: (0, j)),
            pl.BlockSpec((tm, LANES), lambda j, i: (i, 0)),
            pl.BlockSpec((tm, LANES), lambda j, i: (i, 0)),
            pl.BlockSpec((1, tn), lambda j, i: (0, j)),
        ],
        out_specs=out_specs,
        out_shape=out_shape,
        compiler_params=_params(("parallel", "parallel")),
        name=name,
    )(x, w, cos, sin, flag)


def _proj_t_kernel(flag_ref, w_ref, x_ref, cos_ref, sin_ref, o_ref, *, tn, tm, slabs):
    j = pl.program_id(1)
    acc = _nt(w_ref[...], x_ref[...])
    cos = cos_ref[...]
    sin = sin_ref[...]
    half = HEAD_DIM // 2
    for g in range(tn // HEAD_DIM):
        rows = slice(g * HEAD_DIM, (g + 1) * HEAD_DIM)
        a = acc[rows]
        sw = jnp.concatenate([a[half:], a[:half]], axis=0)
        r = jnp.where(flag_ref[j * (tn // HEAD_DIM) + g] > 0, a * cos + sw * sin, a)
        if slabs:
            for s in range(tm // SLAB):
                o_ref[s, rows, :] = r[:, s * SLAB:(s + 1) * SLAB].astype(o_ref.dtype)
        else:
            o_ref[rows, :] = r


def _proj_t(flags, w_t, x, cos_t, sin_t, *, tm, tn, seq=None):
    n, d = x.shape
    rows = w_t.shape[0]
    if seq is None:
        out_spec = pl.BlockSpec((tm // SLAB, tn, SLAB), lambda i, j, f: (i, j, 0))
        out_shape = jax.ShapeDtypeStruct((n // SLAB, rows, SLAB), BF16)
    else:
        per_seq = seq // tm
        out_spec = pl.BlockSpec((None, tn, tm), lambda i, j, f: (i // per_seq, j, i % per_seq))
        out_shape = jax.ShapeDtypeStruct((n // seq, rows, seq), F32)
    grid_spec = pltpu.PrefetchScalarGridSpec(
        num_scalar_prefetch=1,
        grid=(n // tm, rows // tn),
        in_specs=[
            pl.BlockSpec((tn, d), lambda i, j, f: (j, 0)),
            pl.BlockSpec((tm, d), lambda i, j, f: (i, 0)),
            pl.BlockSpec((HEAD_DIM, tm), lambda i, j, f: (0, i)),
            pl.BlockSpec((HEAD_DIM, tm), lambda i, j, f: (0, i)),
        ],
        out_specs=out_spec,
    )
    return pl.pallas_call(
        functools.partial(_proj_t_kernel, tn=tn, tm=tm, slabs=seq is None),
        grid_spec=grid_spec,
        out_shape=out_shape,
        compiler_params=_params(("parallel", "arbitrary")),
        name="in_proj_feature_major" if seq is None else "in_proj_output_rows",
    )(flags, w_t, x, cos_t, sin_t)


def _head_masks(q_pair):
    first = lax.broadcasted_iota(jnp.int32, (PAIR, 1), 0) < HEAD_DIM
    zero = jnp.zeros_like(q_pair)
    return jnp.where(first, q_pair, zero), jnp.where(first, zero, q_pair)


def _online_update(state, scores, masks, v_t):
    m, l, acc = state
    scores = [jnp.where(mk, s, NEG) for s, mk in zip(scores, masks)]
    m_new = m
    for s in scores:
        m_new = jnp.maximum(m_new, jnp.max(s, axis=0, keepdims=True))
    alpha = jnp.exp2(m - m_new)
    probs = [jnp.exp2(s - m_new) for s in scores]
    l = alpha * l
    for p in probs:
        l = l + jnp.sum(p, axis=0, keepdims=True)
    p_cat = probs[0] if len(probs) == 1 else jnp.concatenate(probs, axis=0)
    acc = alpha * acc + _dot(v_t, p_cat.astype(BF16))
    return m_new, l, acc


def _sb_kernel(q_ref, k_ref, v_ref, tri_ref, o_ref):
    t = SLAB
    i = pl.program_id(2)
    qm = _head_masks(q_ref[...])
    tri2 = tri_ref[...]
    key_i = lax.broadcasted_iota(jnp.int32, (t, t), 0)
    qry_i = lax.broadcasted_iota(jnp.int32, (t, t), 1)
    before = key_i < qry_i

    def tiles(js, carry, diag):
        chains = [(n, hh) for n in range(len(js)) for hh in range(2)]
        ks = [k_ref[pl.ds(pl.multiple_of(j * t, t), t), :] for j in js]
        vs = [v_ref[j] for j in js]
        z = {ch: _dot(ks[ch[0]], qm[ch[1]]) for ch in chains}
        sp = {ch: _softplus2(z[ch]) for ch in chains}
        drop = sp if not diag else {ch: jnp.where(before, sp[ch], 0.0) for ch in chains}
        later = {ch: _dot(tri2, jnp.concatenate(_split_bf16(drop[ch]), axis=0)) for ch in chains}
        out = []
        for hh in range(2):
            c, acc = carry[2 * hh:2 * hh + 2]
            a_all = []
            for n in range(len(js)):
                ch = (n, hh)
                a = jnp.exp2(z[ch] - sp[ch] - later[ch] - c)
                if diag:
                    a = jnp.where(before, a, 0.0)
                a_all.append(a.astype(BF16))
                c = c + jnp.sum(drop[ch], axis=0, keepdims=True)
            v_cat = jnp.concatenate([v[hh * HEAD_DIM:(hh + 1) * HEAD_DIM] for v in vs], axis=1)
            acc = acc + _dot(v_cat, jnp.concatenate(a_all, axis=0))
            out += [c, acc]
        return tuple(out)

    init = (jnp.zeros((1, t), F32), jnp.zeros((HEAD_DIM, t), F32)) * 2
    carry = tiles([i], init, True)
    carry = lax.fori_loop(
        0, i // 3,
        lambda n, ca: tiles([i - 1 - 3 * n, i - 2 - 3 * n, i - 3 - 3 * n], ca, False), carry)
    carry = lax.cond(i % 3 == 1, lambda ca: tiles([0], ca, False), lambda ca: ca, carry)
    carry = lax.cond(i % 3 == 2, lambda ca: tiles([1, 0], ca, False), lambda ca: ca, carry)
    o_ref[...] = jnp.concatenate([carry[1], carry[3]], axis=0).astype(o_ref.dtype)


def _tri_upper(n):
    r = lax.broadcasted_iota(jnp.int32, (n, n), 0)
    c = lax.broadcasted_iota(jnp.int32, (n, n), 1)
    return (c > r).astype(BF16)


def _sb_attention(proj_t, k_tok, *, n_b, seq):
    nq = seq // SLAB
    n_pairs = H_SB // 2
    v_blk0 = T_OFF_V // PAIR
    return pl.pallas_call(
        _sb_kernel,
        grid=(n_b, n_pairs, nq),
        in_specs=[
            pl.BlockSpec((None, PAIR, SLAB), lambda b, p, i: (b * nq + i, p, 0)),
            pl.BlockSpec((seq, PAIR), lambda b, p, i: (b, p)),
            pl.BlockSpec((nq, PAIR, SLAB), lambda b, p, i: (b, v_blk0 + p, 0)),
            pl.BlockSpec((SLAB, 2 * SLAB), lambda b, p, i: (0, 0)),
        ],
        out_specs=pl.BlockSpec((None, PAIR, SLAB), lambda b, p, i: (b * nq + i, p, 0)),
        out_shape=jax.ShapeDtypeStruct((n_b * nq, n_pairs * PAIR, SLAB), BF16),
        compiler_params=_params(("parallel", "parallel", "arbitrary")),
        name="sb_attention",
    )(proj_t, k_tok, proj_t, jnp.concatenate([_tri_upper(SLAB)] * 2, axis=1))


def _sort_key(score):
    bits = lax.bitcast_convert_type(score + 0.0, jnp.int32)
    return bits ^ ((bits >> 31) & jnp.int32(0x7FFFFFFF))


_NEG_INF_KEY = -2139095041


def _kth_largest(count_ge, shape, k, bits_per_step=1):
    def body(it, ans):
        unit = jnp.left_shift(jnp.int32(1), 32 - bits_per_step * (it + 1))
        passed = jnp.zeros(shape, jnp.int32)
        for m in range(1, 2 ** bits_per_step):
            passed = passed + jnp.where(count_ge(ans + m * unit) >= k, 1, 0)
        return ans + passed * unit
    return lax.fori_loop(0, 32 // bits_per_step, body, jnp.full(shape, INT_MIN, jnp.int32))


def _tie_cut(count_eq_below, need, shape, n_bits):
    def body(it, a):
        cand = a + jnp.left_shift(jnp.int32(1), n_bits - 1 - it)
        return jnp.where(count_eq_below(cand) < need, cand, a)
    return lax.fori_loop(0, n_bits, body, jnp.zeros(shape, jnp.int32))


DSA_BLOCK_HEADS = 6


def _dsa_kernel(qi_ref, kiwi_ref, ki_ref, q_ref, k_ref, v_ref, o_ref,
                key_scr, m_scr, l_scr, acc_scr, *, nq, n_keep, n_bits):
    t = SLAB
    i = pl.program_id(1)
    n_slab = i + 1
    qpos = i * t + lax.broadcasted_iota(jnp.int32, (1, t), 1)
    kiota = lax.broadcasted_iota(jnp.int32, (t, 1), 0)
    wi_t = kiwi_ref[...].T

    def scores(j, _):
        kij = ki_ref[pl.ds(pl.multiple_of(j * t, t), t), :][:, :IDX_DIM]
        score = jnp.zeros((t, t), F32)
        for h in range(IDX_HEADS):
            sc = _dot(kij, qi_ref[h * IDX_DIM:(h + 1) * IDX_DIM, :])
            score = score + wi_t[IDX_DIM + h:IDX_DIM + h + 1, :] * jnp.maximum(sc, 0.0)
        causal = (j * t + kiota) <= qpos
        key_scr[j] = _sort_key(jnp.where(causal, score, -jnp.inf))
        return 0

    lax.fori_loop(0, n_slab, scores, 0)

    n_pair = (n_slab + 1) // 2

    @pl.when(n_slab % 2 == 1)
    def _():
        key_scr[n_slab] = jnp.full((t, t), _NEG_INF_KEY, jnp.int32)

    def count(pred):
        def body(n, acc):
            for j in (2 * n, 2 * n + 1):
                ones = jnp.where(pred(key_scr[j], j * t + kiota), 1.0, 0.0)
                acc = acc + jnp.sum(ones.reshape(t // 8, 8, t), axis=0)
            return acc
        acc = lax.fori_loop(0, n_pair, body, jnp.zeros((8, t), F32))
        return jnp.sum(acc, axis=0, keepdims=True)

    thr = _kth_largest(lambda cand: count(lambda key, kp: key >= cand), (1, t), float(n_keep))
    need = float(n_keep) - count(lambda key, kp: key > thr)
    n_eq = count(lambda key, kp: key == thr)
    cut = lax.cond(
        jnp.max(n_eq - need) > 0.0,
        lambda: _tie_cut(lambda cand: count(lambda key, kp: (key == thr) & (kp < cand)),
                         need, (1, t), n_bits),
        lambda: jnp.full((1, t), 2 ** n_bits, jnp.int32))

    m_scr[...] = jnp.full(m_scr.shape, NEG, F32)
    l_scr[...] = jnp.zeros(l_scr.shape, F32)
    acc_scr[...] = jnp.zeros(acc_scr.shape, F32)
    qm = []
    for pr in range(DSA_BLOCK_HEADS // 2):
        qm += list(_head_masks(q_ref[pr * PAIR:(pr + 1) * PAIR, :]))

    def attend(n, _):
        sel, z, vs = [], [], []
        for j in (2 * n, 2 * n + 1):
            key = key_scr[j]
            kp = j * t + kiota
            sel.append(((key > thr) | ((key == thr) & (kp <= cut))) & (kp <= qpos))
            jk = jnp.minimum(j, nq - 1)
            start = pl.multiple_of(jk * t, t)
            vs.append(v_ref[jk])
            kj = [k_ref[pl.ds(start, t), pr * PAIR:(pr + 1) * PAIR]
                  for pr in range((H_DSA + 1) // 2)]
            z.append([_dot(kj[h // 2], qm[h]) for h in range(H_DSA)])
        for h in range(H_DSA):
            state = (m_scr[h], l_scr[h], acc_scr[h])
            v_cat = jnp.concatenate([v[h * HEAD_DIM:(h + 1) * HEAD_DIM] for v in vs], axis=1)
            m, l, acc = _online_update(state, [z[0][h], z[1][h]], sel, v_cat)
            m_scr[h] = m
            l_scr[h] = l
            acc_scr[h] = acc
        return 0

    lax.fori_loop(0, n_pair, attend, 0)
    for h in range(H_DSA):
        o_ref[h * HEAD_DIM:(h + 1) * HEAD_DIM, :] = (acc_scr[h] / l_scr[h]).astype(o_ref.dtype)
    o_ref[H_DSA * HEAD_DIM:, :] = jnp.zeros((HEAD_DIM, t), o_ref.dtype)


def _dsa_attention(proj_t, k_tok, kiwi_f32, *, n_b, seq):
    nq = seq // SLAB
    wide = DSA_BLOCK_HEADS * HEAD_DIM
    n_keep = min(DSA_TOPK, seq // 4)
    n_bits = max(1, (seq - 1).bit_length())
    kiwi_blk = N_OFF_KI // LANES
    return pl.pallas_call(
        functools.partial(_dsa_kernel, nq=nq, n_keep=n_keep, n_bits=n_bits),
        grid=(n_b, nq),
        in_specs=[
            pl.BlockSpec((None, QI_W, SLAB), lambda b, i: (b * nq + i, T_OFF_QI // QI_W, 0)),
            pl.BlockSpec((SLAB, LANES), lambda b, i: (b * nq + i, kiwi_blk)),
            pl.BlockSpec((seq, LANES), lambda b, i: (b, kiwi_blk)),
            pl.BlockSpec((None, wide, SLAB), lambda b, i: (b * nq + i, 1, 0)),
            pl.BlockSpec((seq, wide), lambda b, i: (b, 1)),
            pl.BlockSpec((nq, wide, SLAB), lambda b, i: (b, T_OFF_V // wide + 1, 0)),
        ],
        out_specs=pl.BlockSpec((None, wide, SLAB), lambda b, i: (b * nq + i, 0, 0)),
        out_shape=jax.ShapeDtypeStruct((n_b * nq, wide, SLAB), BF16),
        scratch_shapes=[pltpu.VMEM((nq + 1, SLAB, SLAB), jnp.int32),
                        pltpu.VMEM((H_DSA, 1, SLAB), F32), pltpu.VMEM((H_DSA, 1, SLAB), F32),
                        pltpu.VMEM((H_DSA, HEAD_DIM, SLAB), F32)],
        compiler_params=_params(("parallel", "arbitrary")),
        name="dsa_attention",
    )(proj_t, kiwi_f32, k_tok, proj_t, k_tok, proj_t)


def _block_mean_kernel(k_ref, o_ref):
    o_ref[0] = jnp.mean(k_ref[:, :QKV_W], axis=0, keepdims=True)


def _block_means(k_tok_f32, n_tokens):
    nb = n_tokens // MOBA_BLOCK
    return pl.pallas_call(
        _block_mean_kernel,
        grid=(nb,),
        in_specs=[pl.BlockSpec((MOBA_BLOCK, QKV_W), lambda n: (n, 0))],
        out_specs=pl.BlockSpec((1, 1, QKV_W), lambda n: (n, 0, 0)),
        out_shape=jax.ShapeDtypeStruct((nb, 1, QKV_W), F32),
        compiler_params=_params(("parallel",)),
        name="moba_block_means",
    )(k_tok_f32)


MOBA_FIRST_PAIR = (H_SB + H_DSA) // 2


def _moba_kernel(q_ref, k_ref, v_ref, km_ref, o_ref, sel_scr, *, nb):
    t = MOBA_BLOCK
    i = pl.program_id(2)
    n_iota = lax.broadcasted_iota(jnp.int32, (nb, t), 0)
    fi = n_iota.astype(F32)
    valid = n_iota < i
    key_i = lax.broadcasted_iota(jnp.int32, (t, t), 0)
    qry_i = lax.broadcasted_iota(jnp.int32, (t, t), 1)

    def run(heads):
        qm = _head_masks(q_ref[...])
        km_hi, km_lo = _split_bf16(km_ref[...])
        for hh in heads:
            gm = jnp.where(valid, _dot(km_hi, qm[hh]) + _dot(km_lo, qm[hh]), -jnp.inf)
            sel = jnp.zeros((nb, t), jnp.bool_)
            for _ in range(min(MOBA_TOPK, nb)):
                m = jnp.max(gm, axis=0, keepdims=True)
                first = jnp.min(jnp.where(gm == m, fi, float(nb)), axis=0, keepdims=True)
                pick = fi == first
                sel = sel | pick
                gm = jnp.where(pick, -jnp.inf, gm)
            sel_scr[hh] = jnp.where(sel & valid, 1.0, 0.0)

        def blocks(ns, carry, mask_of):
            ks = [k_ref[pl.ds(pl.multiple_of(n * t, t), t), :] for n in ns]
            vs = [v_ref[n] for n in ns]
            z = [[_dot(kj, qm[hh]) for kj in ks] for hh in heads]
            out = []
            for c, hh in enumerate(heads):
                v_cat = jnp.concatenate([v[hh * HEAD_DIM:(hh + 1) * HEAD_DIM] for v in vs], axis=1)
                out += list(_online_update(carry[3 * c:3 * c + 3], z[c],
                                           [mask_of(hh, n) for n in ns], v_cat))
            return tuple(out)

        chosen = lambda hh, n: sel_scr[hh, pl.ds(n, 1), :] > 0.0
        init = (jnp.full((1, t), NEG, F32), jnp.zeros((1, t), F32),
                jnp.zeros((HEAD_DIM, t), F32)) * len(heads)
        carry = blocks([i], init, lambda hh, n: key_i <= qry_i)
        carry = lax.fori_loop(0, i // 2, lambda n, ca: blocks([2 * n, 2 * n + 1], ca, chosen), carry)
        carry = lax.cond(i % 2 == 1, lambda ca: blocks([i - 1], ca, chosen), lambda ca: ca, carry)
        done = {hh: carry[3 * c + 2] / carry[3 * c + 1] for c, hh in enumerate(heads)}
        rows = [done.get(hh, jnp.zeros((HEAD_DIM, t), F32)) for hh in range(2)]
        o_ref[...] = jnp.concatenate(rows, axis=0).astype(o_ref.dtype)

    if (H_SB + H_DSA) % 2 == 1:
        pl.when(pl.program_id(1) == 0)(lambda: run((1,)))
        pl.when(pl.program_id(1) != 0)(lambda: run((0, 1)))
    else:
        run((0, 1))


def _moba_attention(proj_t, k_tok, kmean, *, n_b, seq):
    nq = seq // SLAB
    nb = seq // MOBA_BLOCK
    n_pairs = N_HEADS // 2 - MOBA_FIRST_PAIR
    p0 = MOBA_FIRST_PAIR
    v_blk0 = T_OFF_V // PAIR + p0
    return pl.pallas_call(
        functools.partial(_moba_kernel, nb=nb),
        grid=(n_b, n_pairs, nq),
        in_specs=[
            pl.BlockSpec((None, PAIR, SLAB), lambda b, p, i: (b * nq + i, p0 + p, 0)),
            pl.BlockSpec((seq, PAIR), lambda b, p, i: (b, p0 + p)),
            pl.BlockSpec((nq, PAIR, SLAB), lambda b, p, i: (b, v_blk0 + p, 0)),
            pl.BlockSpec((None, nb, PAIR), lambda b, p, i: (b, 0, p0 + p)),
        ],
        out_specs=pl.BlockSpec((None, PAIR, SLAB), lambda b, p, i: (b * nq + i, p, 0)),
        out_shape=jax.ShapeDtypeStruct((n_b * nq, n_pairs * PAIR, SLAB), BF16),
        scratch_shapes=[pltpu.VMEM((2, nb, SLAB), F32)],
        compiler_params=_params(("parallel", "parallel", "arbitrary")),
        name="moba_attention",
    )(proj_t, k_tok, proj_t, kmean)


ROWS_PER_HEAD = 8


def _top_mask(cols, n_top):
    out = []
    for n, g in enumerate(cols):
        rank = jnp.zeros_like(g)
        for m, o in enumerate(cols):
            if m == n:
                continue
            ahead = (o >= g) if m < n else (o > g)
            rank = rank + jnp.where(ahead, 1.0, 0.0)
        out.append(rank < float(n_top))
    return out


def _sample_mixers(s_scr, p_scr, ki_scr, qi_ref, wi_ref, *, past, n_keep, n_bits):
    lk_len = past + PAGE_SIZE
    r8 = ROWS_PER_HEAD
    kpos = lax.broadcasted_iota(jnp.int32, (1, lk_len), 1)

    def qpos(rows):
        return past + lax.broadcasted_iota(jnp.int32, (rows, 1), 0) % r8

    ra = H_SB * r8
    z = s_scr[:ra, :]
    before = kpos < qpos(ra)
    sp = _softplus2(z)
    drop = jnp.where(before, sp, 0.0)
    hi, lo = _split_bf16(drop)
    tri_r = lax.broadcasted_iota(jnp.int32, (2 * LANES, LANES), 0) % LANES
    tri_c = lax.broadcasted_iota(jnp.int32, (2 * LANES, LANES), 1)
    tri2 = (tri_r > tri_c).astype(BF16)
    c = jnp.zeros((ra, 1), F32)
    for blk in reversed(range(lk_len // LANES)):
        sl = slice(blk * LANES, (blk + 1) * LANES)
        later = _dot(jnp.concatenate([hi[:, sl], lo[:, sl]], axis=1), tri2) + c
        a = jnp.where(before[:, sl], jnp.exp2(z[:, sl] - sp[:, sl] - later), 0.0)
        p_scr[:ra, sl] = a.astype(BF16)
        c = c + jnp.sum(drop[:, sl], axis=1, keepdims=True)

    sc = jnp.maximum(_dot(qi_ref[0], ki_scr[...]), 0.0) * wi_ref[0]
    score = jnp.sum(sc.reshape(r8, IDX_HEADS, lk_len), axis=1)
    causal8 = kpos <= qpos(r8)
    key = _sort_key(jnp.where(causal8, score, -jnp.inf))

    def count(pred):
        return jnp.sum(jnp.where(pred, 1.0, 0.0), axis=1, keepdims=True)

    thr = _kth_largest(lambda cand: count(key >= cand), (r8, 1), float(n_keep), bits_per_step=2)
    need = float(n_keep) - count(key > thr)
    n_eq = count(key == thr)
    cut = lax.cond(
        jnp.max(n_eq - need) > 0.0,
        lambda: _tie_cut(lambda cand: count((key == thr) & (kpos < cand)), need, (r8, 1), n_bits),
        lambda: jnp.full((r8, 1), 2 ** n_bits, jnp.int32))
    sel8 = ((key > thr) | ((key == thr) & (kpos <= cut))) & causal8
    rb = H_DSA * r8
    sel = jnp.concatenate([jnp.where(sel8, 1.0, 0.0)] * H_DSA, axis=0) > 0.0
    s = jnp.where(sel, s_scr[ra:ra + rb, :], NEG)
    p = jnp.where(sel, jnp.exp2(s - jnp.max(s, axis=1, keepdims=True)), 0.0)
    l_b = jnp.sum(p, axis=1, keepdims=True)
    p_scr[ra:ra + rb, :] = p.astype(BF16)

    rc = H_MOBA * r8
    sm = s_scr[ra + rb:, :]
    n_blk = past // MOBA_BLOCK
    gate = [jnp.sum(sm[:, n * MOBA_BLOCK:(n + 1) * MOBA_BLOCK], axis=1, keepdims=True)
            for n in range(n_blk)]
    chosen = _top_mask(gate, min(MOBA_TOPK, n_blk + 1))
    own_ok = kpos[:, past:] <= qpos(rc)
    pieces = [jnp.where(chosen[n], sm[:, n * MOBA_BLOCK:(n + 1) * MOBA_BLOCK], NEG)
              for n in range(n_blk)]
    pieces.append(jnp.where(own_ok, sm[:, past:], NEG))
    m_c = pieces[0].max(axis=1, keepdims=True)
    for pc in pieces[1:]:
        m_c = jnp.maximum(m_c, pc.max(axis=1, keepdims=True))
    l_c = jnp.zeros((rc, 1), F32)
    off = 0
    for pc in pieces:
        e = jnp.where(pc > 0.5 * NEG, jnp.exp2(pc - m_c), 0.0)
        l_c = l_c + jnp.sum(e, axis=1, keepdims=True)
        p_scr[ra + rb:, off:off + pc.shape[1]] = e.astype(BF16)
        off += pc.shape[1]

    return jnp.concatenate([jnp.ones((ra, 1), F32), 1.0 / l_b, 1.0 / l_c], axis=0)


def _sample_kernel(pt_ref, q_ref, qi_ref, wi_ref, knew_ref, vnew_ref, kinew_ref, *rest,
                   n_pages, n_keep, n_bits):
    del pt_ref
    k_pages, v_pages = rest[:n_pages], rest[n_pages:2 * n_pages]
    i_pages = rest[2 * n_pages:3 * n_pages]
    o_ref = rest[3 * n_pages]
    k_scr, v_scr, ki_scr, s_scr, p_scr = rest[3 * n_pages + 1:]
    past = n_pages * PAGE_SIZE
    r8 = ROWS_PER_HEAD

    for pg in range(n_pages):
        cols = slice(pg * PAGE_SIZE, (pg + 1) * PAGE_SIZE)
        k_scr[:, :, cols] = k_pages[pg][...].astype(BF16)
        v_scr[:, :, cols] = v_pages[pg][...].astype(BF16)
        ki_scr[:, cols] = i_pages[pg][...].astype(BF16)
    k_scr[:, :, past:] = knew_ref[0]
    v_scr[:, :, past:] = vnew_ref[0]
    ki_scr[:, past:] = kinew_ref[0]
    for h in range(N_HEADS):
        s_scr[h * r8:(h + 1) * r8, :] = _dot(q_ref[0, h], k_scr[h])
    inv = _sample_mixers(s_scr, p_scr, ki_scr, qi_ref, wi_ref, past=past, n_keep=n_keep,
                         n_bits=n_bits)
    for h in range(N_HEADS):
        rows = slice(h * r8, (h + 1) * r8)
        o_ref[0, :, h * HEAD_DIM:(h + 1) * HEAD_DIM] = _nt(p_scr[rows, :], v_scr[h]) * inv[rows]


def _sample_attention(layer, page_table, q8, qi8, wi8, knew, vnew, kinew,
                      cache_k, cache_v, cache_ik, *, n_new):
    n_seq, n_pages = page_table.shape
    past = n_pages * PAGE_SIZE
    lk_len = past + PAGE_SIZE
    n_keep = min(DSA_TOPK, (past + n_new) // 4)
    n_bits = max(1, (lk_len - 1).bit_length())
    rows = N_HEADS * ROWS_PER_HEAD

    def page_spec(j, *dims):
        return pl.BlockSpec((None, None) + dims + (PAGE_SIZE,),
                            lambda b, pt: (layer, pt[b, j]) + (0,) * (len(dims) + 1))

    def seq_spec(*dims):
        return pl.BlockSpec((1,) + dims, lambda b, pt: (b,) + (0,) * len(dims))

    in_specs = [seq_spec(N_HEADS, ROWS_PER_HEAD, HEAD_DIM),
                seq_spec(ROWS_PER_HEAD * IDX_HEADS, IDX_DIM),
                seq_spec(ROWS_PER_HEAD * IDX_HEADS, 1),
                seq_spec(N_HEADS, HEAD_DIM, PAGE_SIZE), seq_spec(N_HEADS, HEAD_DIM, PAGE_SIZE),
                seq_spec(IDX_DIM, PAGE_SIZE)]
    in_specs += [page_spec(j, N_HEADS, HEAD_DIM) for j in range(n_pages)]
    in_specs += [page_spec(j, N_HEADS, HEAD_DIM) for j in range(n_pages)]
    in_specs += [page_spec(j, IDX_DIM) for j in range(n_pages)]
    grid_spec = pltpu.PrefetchScalarGridSpec(
        num_scalar_prefetch=1,
        grid=(n_seq,),
        in_specs=in_specs,
        out_specs=pl.BlockSpec((1, ROWS_PER_HEAD, QKV_W), lambda b, pt: (b, 0, 0)),
        scratch_shapes=[pltpu.VMEM((N_HEADS, HEAD_DIM, lk_len), BF16),
                        pltpu.VMEM((N_HEADS, HEAD_DIM, lk_len), BF16),
                        pltpu.VMEM((IDX_DIM, lk_len), BF16),
                        pltpu.VMEM((rows, lk_len), F32), pltpu.VMEM((rows, lk_len), BF16)],
    )
    out = pl.pallas_call(
        functools.partial(_sample_kernel, n_pages=n_pages, n_keep=n_keep, n_bits=n_bits),
        grid_spec=grid_spec,
        out_shape=jax.ShapeDtypeStruct((n_seq, ROWS_PER_HEAD, QKV_W), F32),
        compiler_params=_params(("arbitrary",)),
        name="sample_attention",
    )(page_table, q8, qi8, wi8, knew, vnew, kinew,
      *([cache_k] * n_pages), *([cache_v] * n_pages), *([cache_ik] * n_pages))
    return out[:, :n_new]


def _layer_norm(h, g, b):
    mu = jnp.mean(h, axis=-1, keepdims=True)
    d = h - mu
    var = jnp.mean(d * d, axis=-1, keepdims=True)
    return d * lax.rsqrt(var + LN_EPS) * g + b


def _merge_kernel(ya_ref, yb_ref, yc_ref, g_ref, bg_ref, wa_ref, wb_ref, wc_ref, wo_ref,
                  x_ref, lg_ref, lb_ref, o_ref, ob_ref):
    dm = x_ref.shape[1]
    merged = jnp.zeros(x_ref.shape, F32)
    for n, (y_ref, w_ref) in enumerate(((ya_ref, wa_ref), (yb_ref, wb_ref), (yc_ref, wc_ref))):
        sl = slice(n * dm, (n + 1) * dm)
        gate = jax.nn.sigmoid(g_ref[:, sl].astype(F32) + bg_ref[:, sl])
        merged = merged + gate * _dot(y_ref[...], w_ref[...])
    mix = _dot(merged.astype(BF16), wo_ref[...])
    y = _layer_norm(ALPHA * x_ref[...] + mix, lg_ref[...], lb_ref[...])
    o_ref[...] = y
    ob_ref[...] = y.astype(BF16)


def _merge(ya, yb, yc, g, bg, wa, wb, wc, wo, x, lg, lb, *, tm):
    n, dm = x.shape
    row = lambda w: pl.BlockSpec((tm, w), lambda i: (i, 0))
    full = lambda a: pl.BlockSpec(a.shape, lambda i: (0, 0))
    return pl.pallas_call(
        _merge_kernel,
        grid=(n // tm,),
        in_specs=[row(ya.shape[1]), row(yb.shape[1]), row(yc.shape[1]), row(g.shape[1]),
                  full(bg), full(wa), full(wb), full(wc), full(wo), row(dm), full(lg), full(lb)],
        out_specs=[row(dm), row(dm)],
        out_shape=[jax.ShapeDtypeStruct((n, dm), F32), jax.ShapeDtypeStruct((n, dm), BF16)],
        compiler_params=_params(("parallel",)),
        name="merge_out_ln",
    )(ya, yb, yc, g, bg, wa, wb, wc, wo, x, lg, lb)


def _route(x, xb, wr_ref, rb_ref):
    w_hi, w_lo = _split_bf16(wr_ref[...])
    x_lo = (x - xb.astype(F32)).astype(BF16)
    logits = _nt(w_hi, xb) + _nt(w_hi, x_lo) + _nt(w_lo, xb)
    s = jax.nn.sigmoid(logits)
    sb = s + rb_ref[...]
    s_col = [s[e:e + 1, :] for e in range(N_EXPERTS)]
    b_col = [sb[e:e + 1, :] for e in range(N_EXPERTS)]
    best_score = None
    for gi in range(N_GROUPS):
        a, b, c, d = b_col[gi * EXPERTS_PER_GROUP:(gi + 1) * EXPERTS_PER_GROUP]
        m1, n1 = jnp.maximum(a, b), jnp.minimum(a, b)
        m2, n2 = jnp.maximum(c, d), jnp.minimum(c, d)
        score = jnp.maximum(m1, m2) + jnp.maximum(jnp.minimum(m1, m2), jnp.maximum(n1, n2))
        if gi == 0:
            best_score, best = score, jnp.zeros_like(score)
        else:
            better = score > best_score
            best = jnp.where(better, float(gi), best)
            best_score = jnp.where(better, score, best_score)
    picked = []
    for gi in range(N_GROUPS):
        cols = b_col[gi * EXPERTS_PER_GROUP:(gi + 1) * EXPERTS_PER_GROUP]
        top = _top_mask(cols, 2)
        in_group = best == float(gi)
        for j in range(EXPERTS_PER_GROUP):
            picked.append(jnp.where(in_group & top[j], s_col[gi * EXPERTS_PER_GROUP + j], 0.0))
    total = picked[0]
    for c in picked[1:]:
        total = total + c
    return jnp.concatenate([p / total for p in picked], axis=0)


MOE_CHUNK = 128


def _moe_kernel(x_ref, xb_ref, wr_ref, rb_ref, tri_ref, wg_ref, wu_ref, wd_ref, lg_ref, lb_ref,
                o_ref, ob_ref, comb_t_ref, pos_t_ref, comb_c_ref, pos_c_ref, cnt_ref, acc_ref):
    e = pl.program_id(1)
    tm = x_ref.shape[0]

    @pl.when(e == 0)
    def _():
        comb_t = _route(x_ref[...], xb_ref[...], wr_ref, rb_ref)
        routed = jnp.where(comb_t > 0.0, 1.0, 0.0)
        pos_t = _dot(routed.astype(BF16), tri_ref[...])
        comb_t_ref[...] = comb_t
        pos_t_ref[...] = pos_t
        pad = jnp.zeros((LANES - N_EXPERTS, tm), F32)
        comb_c_ref[...] = jnp.concatenate([comb_t, pad], axis=0).T
        pos_c_ref[...] = jnp.concatenate([pos_t, pad], axis=0).T
        for ex in range(N_EXPERTS):
            cnt_ref[ex] = jnp.sum(routed[ex:ex + 1, :]).astype(jnp.int32)
        acc_ref[...] = jnp.zeros_like(acc_ref)

    xb = xb_ref[...]
    comb_row = comb_t_ref[pl.ds(e, 1), :]
    pos_row = pos_t_ref[pl.ds(e, 1), :]
    lane = lax.broadcasted_iota(jnp.int32, comb_c_ref.shape, 1)
    column = lambda ref: jnp.sum(jnp.where(lane == e, ref[...], 0.0), axis=1, keepdims=True)
    comb_col, pos_col = column(comb_c_ref), column(pos_c_ref)
    row_iota = lax.broadcasted_iota(jnp.int32, (MOE_CHUNK, 1), 0).astype(F32)
    lane_iota = lax.broadcasted_iota(jnp.int32, (1, MOE_CHUNK), 1).astype(F32)

    def chunk(c, _):
        base = (c * MOE_CHUNK).astype(F32)
        pick = jnp.where((pos_row == row_iota + base) & (comb_row > 0.0), 1.0, 0.0)
        weight = jnp.sum(pick * comb_row, axis=1, keepdims=True)
        xg = _dot(pick.astype(BF16), xb).astype(BF16)
        gate = _dot(xg, wg_ref[0])
        h = gate * jax.nn.sigmoid(gate) * _dot(xg, wu_ref[0]) * weight
        out = _dot(h.astype(BF16), wd_ref[0]).astype(BF16)
        back = jnp.where((pos_col == lane_iota + base) & (comb_col > 0.0), 1.0, 0.0).astype(BF16)
        acc_ref[...] += _dot(back, out)
        return 0

    lax.fori_loop(0, (cnt_ref[e] + MOE_CHUNK - 1) // MOE_CHUNK, chunk, 0)

    @pl.when(e == pl.num_programs(1) - 1)
    def _():
        y = _layer_norm(ALPHA * x_ref[...] + acc_ref[...], lg_ref[...], lb_ref[...])
        o_ref[...] = y
        ob_ref[...] = y.astype(BF16)


def _moe(x, xb, wr, rb, wg, wu, wd, lg, lb, *, tm):
    n, dm = x.shape
    n_e, _, de = wg.shape
    row = pl.BlockSpec((tm, dm), lambda i, e: (i, 0))
    full = lambda a: pl.BlockSpec(a.shape, lambda i, e: (0, 0))
    r = lax.broadcasted_iota(jnp.int32, (tm, tm), 0)
    c = lax.broadcasted_iota(jnp.int32, (tm, tm), 1)
    tri = (r < c).astype(BF16)
    return pl.pallas_call(
        _moe_kernel,
        grid=(n // tm, n_e),
        in_specs=[row, row, full(wr), full(rb), full(tri),
                  pl.BlockSpec((1, dm, de), lambda i, e: (e, 0, 0)),
                  pl.BlockSpec((1, dm, de), lambda i, e: (e, 0, 0)),
                  pl.BlockSpec((1, de, dm), lambda i, e: (e, 0, 0)),
                  full(lg), full(lb)],
        out_specs=[row, row],
        out_shape=[jax.ShapeDtypeStruct((n, dm), F32), jax.ShapeDtypeStruct((n, dm), BF16)],
        scratch_shapes=[pltpu.VMEM((n_e, tm), F32), pltpu.VMEM((n_e, tm), F32),
                        pltpu.VMEM((tm, LANES), F32), pltpu.VMEM((tm, LANES), F32),
                        pltpu.SMEM((n_e,), jnp.int32), pltpu.VMEM((tm, dm), F32)],
        compiler_params=_params(("parallel", "arbitrary")),
        name="moe_ln",
    )(x, xb, wr, rb, tri, wg, wu, wd, lg, lb)


def _pad_cols(a, width):
    return jnp.pad(a, ((0, 0), (0, width - a.shape[1])))


def _layer_weights(w_in_l):
    scale = HEAD_DIM ** -0.5 * LOG2E
    cuts = [QKV_W, 2 * QKV_W, 3 * QKV_W, 3 * QKV_W + QI_W, 3 * QKV_W + QI_W + IDX_DIM,
            3 * QKV_W + QI_W + IDX_DIM + IDX_HEADS]
    wq, wk, wv, wqi, wki, wwi, wg = jnp.split(w_in_l, cuts, axis=1)
    w_tok = _pad_cols(jnp.concatenate([wk, wki, wwi], axis=1), N_COLS)
    col = jnp.arange(N_COLS)
    flag_tok = ((col < QKV_W) & (col // HEAD_DIM >= H_SB)) | ((col >= N_OFF_KI) & (col < N_OFF_WI))
    w_feat = jnp.concatenate([wq * scale, wqi, wv], axis=1).T
    grp = jnp.arange(T_ROWS // HEAD_DIM)
    flag_feat = ((grp >= H_SB) & (grp < N_HEADS)) | ((grp >= T_OFF_QI // HEAD_DIM)
                                                     & (grp < T_OFF_V // HEAD_DIM))
    w_sq = jnp.concatenate([wq * scale, wqi, wv], axis=1)
    col_s = jnp.arange(T_ROWS)
    flag_sq = (col_s // HEAD_DIM >= H_SB) & (col_s < T_OFF_V)
    head = jnp.arange(N_HEADS)
    f = lambda a: a.astype(F32)[None, :]
    i32 = lambda a: a.astype(jnp.int32)
    return dict(w_tok=w_tok.astype(BF16), flag_tok=f(flag_tok),
                w_feat=w_feat.astype(BF16), flag_feat=i32(flag_feat),
                w_sq=w_sq.astype(BF16), flag_sq=f(flag_sq), w_gate=wg.astype(BF16),
                w_k=wk.T.astype(BF16), flag_k=i32(head >= H_SB),
                w_v=wv.T.astype(BF16), flag_v=i32(head < 0),
                w_ki=wki.T.astype(BF16), flag_ki=jnp.ones((1,), jnp.int32))


def _rope_tables(pos):
    half = HEAD_DIM // 2
    inv = ROPE_THETA ** (-jnp.arange(half, dtype=F32) / half)
    ang = pos.astype(F32)[:, None] * inv[None, :]
    cos, sin = jnp.cos(ang), jnp.sin(ang)
    return jnp.concatenate([cos, cos], axis=1), jnp.concatenate([-sin, sin], axis=1)


def kernel(x_prompt, x_sample, cache_k, cache_v, cache_idx_k, page_table, w_in, w_br_a, w_br_b,
           w_br_c, b_gate, w_out, ln1_g, ln1_b, ln2_g, ln2_b, w_router, router_bias,
           w_exp_gate, w_exp_up, w_exp_down):
    n_b, seq, dm = x_prompt.shape
    n_seq, n_new, _ = x_sample.shape
    depth = w_in.shape[0]
    n_p = n_b * seq
    n_s = n_seq * n_new
    n_pages = page_table.shape[1]
    past = n_pages * PAGE_SIZE
    assert seq % SLAB == 0 and past % MOBA_BLOCK == 0 and n_new <= ROWS_PER_HEAD
    ck_t = cache_k.transpose(0, 1, 3, 4, 2)
    cv_t = cache_v.transpose(0, 1, 3, 4, 2)
    cik_t = cache_idx_k.transpose(0, 1, 3, 2)

    pos = jnp.concatenate([jnp.tile(jnp.arange(seq, dtype=jnp.int32), n_b),
                           jnp.tile(past + jnp.arange(n_new, dtype=jnp.int32), n_seq)])
    cos64, sin64 = _rope_tables(pos)
    cos = jnp.concatenate([cos64, cos64], axis=1)
    sin = jnp.concatenate([sin64, sin64], axis=1)
    cos_t, sin_t = cos64[:n_p].T, sin64[:n_p].T
    x = jnp.concatenate([x_prompt.reshape(n_p, dm), x_sample.reshape(n_s, dm)], axis=0)
    xb = x.astype(BF16)
    proj_tm = max(m * TILE_TOKENS for m in (1, 2, 3) if (n_p + n_s) % (m * TILE_TOKENS) == 0)
    moe_tm = 3 * TILE_TOKENS // 2 if (n_p + n_s) % (3 * TILE_TOKENS // 2) == 0 else TILE_TOKENS

    pad_rows = lambda a, n: jnp.pad(a, ((0, 0), (0, n - a.shape[1])) + ((0, 0),) * (a.ndim - 2))
    zeros_b = jnp.zeros((HEAD_DIM, dm), BF16)
    rows_p, rows_s = [], []
    for l in range(depth):
        w = _layer_weights(w_in[l])
        tok_f32, tok_bf = _proj(xb, w["w_tok"], cos, sin, w["flag_tok"], rope=True,
                                tm=proj_tm, tn=TILE_COLS_TOK, name="in_proj_token_major")
        gates = _proj(xb, w["w_gate"], cos, sin, w["flag_tok"], rope=False,
                      tm=proj_tm, tn=TILE_COLS, name="in_proj_gates")
        xb_p = xb[:n_p]
        proj_t = _proj_t(w["flag_feat"], w["w_feat"], xb_p, cos_t, sin_t,
                         tm=2 * TILE_TOKENS, tn=TILE_ROWS_T)
        sq, _ = _proj(xb[n_p:], w["w_sq"], cos[n_p:], sin[n_p:], w["flag_sq"], rope=True,
                      tm=n_s, tn=TILE_ROWS_T, name="in_proj_sample_q")

        out_rows = lambda name, tn: _proj_t(w["flag_" + name], w["w_" + name], xb_p, cos_t, sin_t,
                                            tm=2 * TILE_TOKENS, tn=tn, seq=seq)
        heads_last = lambda a: a.reshape(n_b, N_HEADS, HEAD_DIM, seq).transpose(0, 3, 1, 2)
        rows_p.append((heads_last(out_rows("k", TILE_TOKENS)), heads_last(out_rows("v", TILE_TOKENS)),
                       out_rows("ki", IDX_DIM).transpose(0, 2, 1)))
        tok_s = tok_f32[n_p:].reshape(n_seq, n_new, -1)
        k_s, ki_s = tok_s[..., :QKV_W], tok_s[..., N_OFF_KI:N_OFF_WI]
        v_s = sq[:, T_OFF_V:].reshape(n_seq, n_new, QKV_W)
        rows_s.append((k_s.reshape(n_seq, n_new, N_HEADS, HEAD_DIM),
                       v_s.reshape(n_seq, n_new, N_HEADS, HEAD_DIM), ki_s))

        y_a = _sb_attention(proj_t, tok_bf, n_b=n_b, seq=seq)
        y_b = _dsa_attention(proj_t, tok_bf, tok_f32, n_b=n_b, seq=seq)
        kmean = _block_means(tok_f32, n_p).reshape(n_b, seq // MOBA_BLOCK, QKV_W)
        y_c = _moba_attention(proj_t, tok_bf, kmean, n_b=n_b, seq=seq)
        tok = lambda y: y.transpose(0, 2, 1).reshape(n_p, -1)

        q_s = sq[:, :QKV_W].reshape(n_seq, n_new, N_HEADS, HEAD_DIM)
        q8 = pad_rows(q_s, ROWS_PER_HEAD).transpose(0, 2, 1, 3).astype(BF16)
        qi8 = pad_rows(sq[:, T_OFF_QI:T_OFF_V].reshape(n_seq, n_new, QI_W), ROWS_PER_HEAD).reshape(
            n_seq, ROWS_PER_HEAD * IDX_HEADS, IDX_DIM).astype(BF16)
        wi8 = pad_rows(tok_s[..., N_OFF_WI:N_OFF_WI + IDX_HEADS], ROWS_PER_HEAD).reshape(
            n_seq, ROWS_PER_HEAD * IDX_HEADS, 1)
        new_heads = lambda a: pad_rows(a.reshape(n_seq, n_new, N_HEADS, HEAD_DIM),
                                       PAGE_SIZE).transpose(0, 2, 3, 1).astype(BF16)
        y_s = _sample_attention(
            l, page_table, q8, qi8, wi8, new_heads(k_s), new_heads(v_s),
            pad_rows(ki_s, PAGE_SIZE).transpose(0, 2, 1).astype(BF16),
            ck_t, cv_t, cik_t, n_new=n_new)
        y_s = y_s.reshape(n_s, QKV_W).astype(BF16)

        wa_cols, wb_cols = H_SB * HEAD_DIM, (H_SB + H_DSA) * HEAD_DIM
        blk = DSA_BLOCK_HEADS * HEAD_DIM
        ya = jnp.concatenate([tok(y_a), y_s[:, :wa_cols]], axis=0)
        yb = jnp.concatenate([tok(y_b), y_s[:, wa_cols:wa_cols + blk]], axis=0)
        yc = jnp.concatenate([tok(y_c), y_s[:, QKV_W - blk:]], axis=0)
        w_b = jnp.concatenate([w_br_b[l].astype(BF16), zeros_b], axis=0)
        w_c = jnp.concatenate([zeros_b, w_br_c[l].astype(BF16)], axis=0)
        del wb_cols
        x, xb = _merge(ya, yb, yc, gates, b_gate[l][None, :], w_br_a[l].astype(BF16), w_b, w_c,
                       w_out[l].astype(BF16), x, ln1_g[l][None, :], ln1_b[l][None, :],
                       tm=TILE_TOKENS)
        x, xb = _moe(x, xb, w_router.T, router_bias[:, None], w_exp_gate[l].astype(BF16),
                     w_exp_up[l].astype(BF16), w_exp_down[l].astype(BF16),
                     ln2_g[l][None, :], ln2_b[l][None, :], tm=moe_tm)

    stack = lambda rows, i: jnp.stack([r[i] for r in rows], 0)
    return (x[:n_p].reshape(n_b, seq, dm), x[n_p:].reshape(n_seq, n_new, dm),
            stack(rows_p, 0), stack(rows_p, 1), stack(rows_p, 2),
            stack(rows_s, 0), stack(rows_s, 1), stack(rows_s, 2))
```

```python
import functools

import jax
import jax.numpy as jnp
from jax import lax
from jax.experimental import pallas as pl
from jax.experimental.pallas import tpu as pltpu

F32 = jnp.float32
BF16 = jnp.bfloat16

HEAD_DIM = 64
H_SB, H_DSA, H_MOBA = 6, 5, 5
N_HEADS = H_SB + H_DSA + H_MOBA
IDX_HEADS = 8
IDX_DIM = 64
DSA_TOPK = 256
MOBA_BLOCK = 256
MOBA_TOPK = 3
N_EXPERTS = 16
EXPERTS_PER_GROUP = 4
N_GROUPS = N_EXPERTS // EXPERTS_PER_GROUP
ROPE_THETA = 10000.0
LN_EPS = 1e-5
DEPTH = 2
ALPHA = (2 * DEPTH) ** 0.25
PAGE_SIZE = 128

LANES = 128
SLAB = 256
PAIR = 2 * HEAD_DIM
NEG = -1e30
INT_MIN = -2 ** 31
VMEM_LIMIT = 56 * 1024 * 1024

QKV_W = N_HEADS * HEAD_DIM
QI_W = IDX_HEADS * IDX_DIM
T_OFF_QI = QKV_W
T_OFF_V = QKV_W + QI_W
T_ROWS = 2 * QKV_W + QI_W
N_OFF_KI = QKV_W
N_OFF_WI = N_OFF_KI + IDX_DIM
N_COLS = QKV_W + LANES
TILE_TOKENS = 512
TILE_COLS = 768
TILE_COLS_TOK = N_COLS // 3
TILE_ROWS_T = 640

_NT = (((1,), (1,)), ((), ()))


def _nt(a, b):
    return lax.dot_general(a, b, _NT, preferred_element_type=F32)


def _dot(a, b):
    return jnp.dot(a, b, preferred_element_type=F32)


def _split_bf16(x):
    hi = x.astype(BF16)
    lo = (x - hi.astype(F32)).astype(BF16)
    return hi, lo


LOG2E = 1.4426950408889634


def _softplus2(z2):
    return jnp.maximum(z2, 0.0) + jnp.log(1.0 + jnp.exp2(-jnp.abs(z2))) * LOG2E


def _params(sem):
    return pltpu.CompilerParams(dimension_semantics=sem, vmem_limit_bytes=VMEM_LIMIT)


def _proj_kernel(x_ref, w_ref, cos_ref, sin_ref, flag_ref, o_ref, ob_ref, *, rope):
    acc = _dot(x_ref[...], w_ref[...])
    if not rope:
        ob_ref[...] = acc.astype(ob_ref.dtype)
        return
    cos = cos_ref[...]
    sin = sin_ref[...]
    lane = lax.broadcasted_iota(jnp.int32, (1, LANES), 1)
    first_half = (lane % HEAD_DIM) < (HEAD_DIM // 2)
    for c in range(acc.shape[1] // LANES):
        sl = slice(c * LANES, (c + 1) * LANES)
        a = acc[:, sl]
        sw = jnp.where(first_half, pltpu.roll(a, LANES - HEAD_DIM // 2, 1),
                       pltpu.roll(a, HEAD_DIM // 2, 1))
        r = jnp.where(flag_ref[:, sl] > 0.0, a * cos + sw * sin, a)
        o_ref[:, sl] = r
        ob_ref[:, sl] = r.astype(BF16)


def _proj(x, w, cos, sin, flag, *, rope, tm, tn, name):
    n, d = x.shape
    width = w.shape[1]
    tile = pl.BlockSpec((tm, tn), lambda j, i: (i, j))
    if rope:
        out_specs = [tile, tile]
        out_shape = [jax.ShapeDtypeStruct((n, width), F32), jax.ShapeDtypeStruct((n, width), BF16)]
        body = functools.partial(_proj_kernel, rope=True)
    else:
        out_specs = tile
        out_shape = jax.ShapeDtypeStruct((n, width), BF16)
        flag = jnp.zeros((1, width), F32)
        body = lambda x_r, w_r, c_r, s_r, f_r, ob_r: _proj_kernel(
            x_r, w_r, c_r, s_r, f_r, None, ob_r, rope=False)
    return pl.pallas_call(
        body,
        grid=(width // tn, n // tm),
        in_specs=[
            pl.BlockSpec((tm, d), lambda j, i: (i, 0)),
            pl.BlockSpec((d, tn), lambda j, i: (0, j)),
            pl.BlockSpec((tm, LANES), lambda j, i: (i, 0)),
            pl.BlockSpec((tm, LANES), lambda j, i: (i, 0)),
            pl.BlockSpec((1, tn), lambda j, i: (0, j)),
        ],
        out_specs=out_specs,
        out_shape=out_shape,
        compiler_params=_params(("parallel", "parallel")),
        name=name,
    )(x, w, cos, sin, flag)


def _proj_t_kernel(flag_ref, w_ref, x_ref, cos_ref, sin_ref, o_ref, *, tn, tm, slabs):
    j = pl.program_id(1)
    acc = _nt(w_ref[...], x_ref[...])
    cos = cos_ref[...]
    sin = sin_ref[...]
    half = HEAD_DIM // 2
    for g in range(tn // HEAD_DIM):
        rows = slice(g * HEAD_DIM, (g + 1) * HEAD_DIM)
        a = acc[rows]
        sw = jnp.concatenate([a[half:], a[:half]], axis=0)
        r = jnp.where(flag_ref[j * (tn // HEAD_DIM) + g] > 0, a * cos + sw * sin, a)
        if slabs:
            for s in range(tm // SLAB):
                o_ref[s, rows, :] = r[:, s * SLAB:(s + 1) * SLAB].astype(o_ref.dtype)
        else:
            o_ref[rows, :] = r


def _proj_t(flags, w_t, x, cos_t, sin_t, *, tm, tn, seq=None):
    n, d = x.shape
    rows = w_t.shape[0]
    if seq is None:
        out_spec = pl.BlockSpec((tm // SLAB, tn, SLAB), lambda i, j, f: (i, j, 0))
        out_shape = jax.ShapeDtypeStruct((n // SLAB, rows, SLAB), BF16)
    else:
        per_seq = seq // tm
        out_spec = pl.BlockSpec((None, tn, tm), lambda i, j, f: (i // per_seq, j, i % per_seq))
        out_shape = jax.ShapeDtypeStruct((n // seq, rows, seq), F32)
    grid_spec = pltpu.PrefetchScalarGridSpec(
        num_scalar_prefetch=1,
        grid=(n // tm, rows // tn),
        in_specs=[
            pl.BlockSpec((tn, d), lambda i, j, f: (j, 0)),
            pl.BlockSpec((tm, d), lambda i, j, f: (i, 0)),
            pl.BlockSpec((HEAD_DIM, tm), lambda i, j, f: (0, i)),
            pl.BlockSpec((HEAD_DIM, tm), lambda i, j, f: (0, i)),
        ],
        out_specs=out_spec,
    )
    return pl.pallas_call(
        functools.partial(_proj_t_kernel, tn=tn, tm=tm, slabs=seq is None),
        grid_spec=grid_spec,
        out_shape=out_shape,
        compiler_params=_params(("parallel", "arbitrary")),
        name="in_proj_feature_major" if seq is None else "in_proj_output_rows",
    )(flags, w_t, x, cos_t, sin_t)


def _head_masks(q_pair):
    first = lax.broadcasted_iota(jnp.int32, (PAIR, 1), 0) < HEAD_DIM
    zero = jnp.zeros_like(q_pair)
    return jnp.where(first, q_pair, zero), jnp.where(first, zero, q_pair)


def _online_update(state, scores, masks, v_t):
    m, l, acc = state
    scores = [jnp.where(mk, s, NEG) for s, mk in zip(scores, masks)]
    m_new = m
    for s in scores:
        m_new = jnp.maximum(m_new, jnp.max(s, axis=0, keepdims=True))
    alpha = jnp.exp2(m - m_new)
    probs = [jnp.exp2(s - m_new) for s in scores]
    l = alpha * l
    for p in probs:
        l = l + jnp.sum(p, axis=0, keepdims=True)
    p_cat = probs[0] if len(probs) == 1 else jnp.concatenate(probs, axis=0)
    acc = alpha * acc + _dot(v_t, p_cat.astype(BF16))
    return m_new, l, acc


def _sb_kernel(q_ref, k_ref, v_ref, tri_ref, o_ref):
    t = SLAB
    i = pl.program_id(2)
    qm = _head_masks(q_ref[...])
    tri2 = tri_ref[...]
    key_i = lax.broadcasted_iota(jnp.int32, (t, t), 0)
    qry_i = lax.broadcasted_iota(jnp.int32, (t, t), 1)
    before = key_i < qry_i

    def tiles(js, carry, diag):
        chains = [(n, hh) for n in range(len(js)) for hh in range(2)]
        ks = [k_ref[pl.ds(pl.multiple_of(j * t, t), t), :] for j in js]
        vs = [v_ref[j] for j in js]
        z = {ch: _dot(ks[ch[0]], qm[ch[1]]) for ch in chains}
        sp = {ch: _softplus2(z[ch]) for ch in chains}
        drop = sp if not diag else {ch: jnp.where(before, sp[ch], 0.0) for ch in chains}
        later = {ch: _dot(tri2, jnp.concatenate(_split_bf16(drop[ch]), axis=0)) for ch in chains}
        out = []
        for hh in range(2):
            c, acc = carry[2 * hh:2 * hh + 2]
            a_all = []
            for n in range(len(js)):
                ch = (n, hh)
                a = jnp.exp2(z[ch] - sp[ch] - later[ch] - c)
                if diag:
                    a = jnp.where(before, a, 0.0)
                a_all.append(a.astype(BF16))
                c = c + jnp.sum(drop[ch], axis=0, keepdims=True)
            v_cat = jnp.concatenate([v[hh * HEAD_DIM:(hh + 1) * HEAD_DIM] for v in vs], axis=1)
            acc = acc + _dot(v_cat, jnp.concatenate(a_all, axis=0))
            out += [c, acc]
        return tuple(out)

    init = (jnp.zeros((1, t), F32), jnp.zeros((HEAD_DIM, t), F32)) * 2
    carry = tiles([i], init, True)
    carry = lax.fori_loop(
        0, i // 3,
        lambda n, ca: tiles([i - 1 - 3 * n, i - 2 - 3 * n, i - 3 - 3 * n], ca, False), carry)
    carry = lax.cond(i % 3 == 1, lambda ca: tiles([0], ca, False), lambda ca: ca, carry)
    carry = lax.cond(i % 3 == 2, lambda ca: tiles([1, 0], ca, False), lambda ca: ca, carry)
    o_ref[...] = jnp.concatenate([carry[1], carry[3]], axis=0).astype(o_ref.dtype)


def _tri_upper(n):
    r = lax.broadcasted_iota(jnp.int32, (n, n), 0)
    c = lax.broadcasted_iota(jnp.int32, (n, n), 1)
    return (c > r).astype(BF16)


def _sb_attention(proj_t, k_tok, *, n_b, seq):
    nq = seq // SLAB
    n_pairs = H_SB // 2
    v_blk0 = T_OFF_V // PAIR
    return pl.pallas_call(
        _sb_kernel,
        grid=(n_b, n_pairs, nq),
        in_specs=[
            pl.BlockSpec((None, PAIR, SLAB), lambda b, p, i: (b * nq + i, p, 0)),
            pl.BlockSpec((seq, PAIR), lambda b, p, i: (b, p)),
            pl.BlockSpec((nq, PAIR, SLAB), lambda b, p, i: (b, v_blk0 + p, 0)),
            pl.BlockSpec((SLAB, 2 * SLAB), lambda b, p, i: (0, 0)),
        ],
        out_specs=pl.BlockSpec((None, PAIR, SLAB), lambda b, p, i: (b * nq + i, p, 0)),
        out_shape=jax.ShapeDtypeStruct((n_b * nq, n_pairs * PAIR, SLAB), BF16),
        compiler_params=_params(("parallel", "parallel", "arbitrary")),
        name="sb_attention",
    )(proj_t, k_tok, proj_t, jnp.concatenate([_tri_upper(SLAB)] * 2, axis=1))


def _sort_key(score):
    bits = lax.bitcast_convert_type(score + 0.0, jnp.int32)
    return bits ^ ((bits >> 31) & jnp.int32(0x7FFFFFFF))


_NEG_INF_KEY = -2139095041


def _kth_largest(count_ge, shape, k, bits_per_step=1):
    def body(it, ans):
        unit = jnp.left_shift(jnp.int32(1), 32 - bits_per_step * (it + 1))
        passed = jnp.zeros(shape, jnp.int32)
        for m in range(1, 2 ** bits_per_step):
            passed = passed + jnp.where(count_ge(ans + m * unit) >= k, 1, 0)
        return ans + passed * unit
    return lax.fori_loop(0, 32 // bits_per_step, body, jnp.full(shape, INT_MIN, jnp.int32))


def _tie_cut(count_eq_below, need, shape, n_bits):
    def body(it, a):
        cand = a + jnp.left_shift(jnp.int32(1), n_bits - 1 - it)
        return jnp.where(count_eq_below(cand) < need, cand, a)
    return lax.fori_loop(0, n_bits, body, jnp.zeros(shape, jnp.int32))


DSA_BLOCK_HEADS = 6


def _dsa_kernel(qi_ref, kiwi_ref, ki_ref, q_ref, k_ref, v_ref, o_ref,
                key_scr, m_scr, l_scr, acc_scr, *, nq, n_keep, n_bits):
    t = SLAB
    i = pl.program_id(1)
    n_slab = i + 1
    qpos = i * t + lax.broadcasted_iota(jnp.int32, (1, t), 1)
    kiota = lax.broadcasted_iota(jnp.int32, (t, 1), 0)
    wi_t = kiwi_ref[...].T

    def scores(j, _):
        kij = ki_ref[pl.ds(pl.multiple_of(j * t, t), t), :][:, :IDX_DIM]
        score = jnp.zeros((t, t), F32)
        for h in range(IDX_HEADS):
            sc = _dot(kij, qi_ref[h * IDX_DIM:(h + 1) * IDX_DIM, :])
            score = score + wi_t[IDX_DIM + h:IDX_DIM + h + 1, :] * jnp.maximum(sc, 0.0)
        causal = (j * t + kiota) <= qpos
        key_scr[j] = _sort_key(jnp.where(causal, score, -jnp.inf))
        return 0

    lax.fori_loop(0, n_slab, scores, 0)

    n_pair = (n_slab + 1) // 2

    @pl.when(n_slab % 2 == 1)
    def _():
        key_scr[n_slab] = jnp.full((t, t), _NEG_INF_KEY, jnp.int32)

    def count(pred):
        def body(n, acc):
            for j in (2 * n, 2 * n + 1):
                ones = jnp.where(pred(key_scr[j], j * t + kiota), 1.0, 0.0)
                acc = acc + jnp.sum(ones.reshape(t // 8, 8, t), axis=0)
            return acc
        acc = lax.fori_loop(0, n_pair, body, jnp.zeros((8, t), F32))
        return jnp.sum(acc, axis=0, keepdims=True)

    thr = _kth_largest(lambda cand: count(lambda key, kp: key >= cand), (1, t), float(n_keep))
    need = float(n_keep) - count(lambda key, kp: key > thr)
    n_eq = count(lambda key, kp: key == thr)
    cut = lax.cond(
        jnp.max(n_eq - need) > 0.0,
        lambda: _tie_cut(lambda cand: count(lambda key, kp: (key == thr) & (kp < cand)),
                         need, (1, t), n_bits),
        lambda: jnp.full((1, t), 2 ** n_bits, jnp.int32))

    m_scr[...] = jnp.full(m_scr.shape, NEG, F32)
    l_scr[...] = jnp.zeros(l_scr.shape, F32)
    acc_scr[...] = jnp.zeros(acc_scr.shape, F32)
    qm = []
    for pr in range(DSA_BLOCK_HEADS // 2):
        qm += list(_head_masks(q_ref[pr * PAIR:(pr + 1) * PAIR, :]))

    def attend(n, _):
        sel, z, vs = [], [], []
        for j in (2 * n, 2 * n + 1):
            key = key_scr[j]
            kp = j * t + kiota
            sel.append(((key > thr) | ((key == thr) & (kp <= cut))) & (kp <= qpos))
            jk = jnp.minimum(j, nq - 1)
            start = pl.multiple_of(jk * t, t)
            vs.append(v_ref[jk])
            kj = [k_ref[pl.ds(start, t), pr * PAIR:(pr + 1) * PAIR]
                  for pr in range((H_DSA + 1) // 2)]
            z.append([_dot(kj[h // 2], qm[h]) for h in range(H_DSA)])
        for h in range(H_DSA):
            state = (m_scr[h], l_scr[h], acc_scr[h])
            v_cat = jnp.concatenate([v[h * HEAD_DIM:(h + 1) * HEAD_DIM] for v in vs], axis=1)
            m, l, acc = _online_update(state, [z[0][h], z[1][h]], sel, v_cat)
            m_scr[h] = m
            l_scr[h] = l
            acc_scr[h] = acc
        return 0

    lax.fori_loop(0, n_pair, attend, 0)
    for h in range(H_DSA):
        o_ref[h * HEAD_DIM:(h + 1) * HEAD_DIM, :] = (acc_scr[h] / l_scr[h]).astype(o_ref.dtype)
    o_ref[H_DSA * HEAD_DIM:, :] = jnp.zeros((HEAD_DIM, t), o_ref.dtype)


def _dsa_attention(proj_t, k_tok, kiwi_f32, *, n_b, seq):
    nq = seq // SLAB
    wide = DSA_BLOCK_HEADS * HEAD_DIM
    n_keep = min(DSA_TOPK, seq // 4)
    n_bits = max(1, (seq - 1).bit_length())
    kiwi_blk = N_OFF_KI // LANES
    return pl.pallas_call(
        functools.partial(_dsa_kernel, nq=nq, n_keep=n_keep, n_bits=n_bits),
        grid=(n_b, nq),
        in_specs=[
            pl.BlockSpec((None, QI_W, SLAB), lambda b, i: (b * nq + i, T_OFF_QI // QI_W, 0)),
            pl.BlockSpec((SLAB, LANES), lambda b, i: (b * nq + i, kiwi_blk)),
            pl.BlockSpec((seq, LANES), lambda b, i: (b, kiwi_blk)),
            pl.BlockSpec((None, wide, SLAB), lambda b, i: (b * nq + i, 1, 0)),
            pl.BlockSpec((seq, wide), lambda b, i: (b, 1)),
            pl.BlockSpec((nq, wide, SLAB), lambda b, i: (b, T_OFF_V // wide + 1, 0)),
        ],
        out_specs=pl.BlockSpec((None, wide, SLAB), lambda b, i: (b * nq + i, 0, 0)),
        out_shape=jax.ShapeDtypeStruct((n_b * nq, wide, SLAB), BF16),
        scratch_shapes=[pltpu.VMEM((nq + 1, SLAB, SLAB), jnp.int32),
                        pltpu.VMEM((H_DSA, 1, SLAB), F32), pltpu.VMEM((H_DSA, 1, SLAB), F32),
                        pltpu.VMEM((H_DSA, HEAD_DIM, SLAB), F32)],
        compiler_params=_params(("parallel", "arbitrary")),
        name="dsa_attention",
    )(proj_t, kiwi_f32, k_tok, proj_t, k_tok, proj_t)


def _block_mean_kernel(k_ref, o_ref):
    o_ref[0] = jnp.mean(k_ref[:, :QKV_W], axis=0, keepdims=True)


def _block_means(k_tok_f32, n_tokens):
    nb = n_tokens // MOBA_BLOCK
    return pl.pallas_call(
        _block_mean_kernel,
        grid=(nb,),
        in_specs=[pl.BlockSpec((MOBA_BLOCK, QKV_W), lambda n: (n, 0))],
        out_specs=pl.BlockSpec((1, 1, QKV_W), lambda n: (n, 0, 0)),
        out_shape=jax.ShapeDtypeStruct((nb, 1, QKV_W), F32),
        compiler_params=_params(("parallel",)),
        name="moba_block_means",
    )(k_tok_f32)


MOBA_FIRST_PAIR = (H_SB + H_DSA) // 2


def _moba_kernel(q_ref, k_ref, v_ref, km_ref, o_ref, sel_scr, *, nb):
    t = MOBA_BLOCK
    i = pl.program_id(2)
    n_iota = lax.broadcasted_iota(jnp.int32, (nb, t), 0)
    fi = n_iota.astype(F32)
    valid = n_iota < i
    key_i = lax.broadcasted_iota(jnp.int32, (t, t), 0)
    qry_i = lax.broadcasted_iota(jnp.int32, (t, t), 1)

    def run(heads):
        qm = _head_masks(q_ref[...])
        km_hi, km_lo = _split_bf16(km_ref[...])
        for hh in heads:
            gm = jnp.where(valid, _dot(km_hi, qm[hh]) + _dot(km_lo, qm[hh]), -jnp.inf)
            sel = jnp.zeros((nb, t), jnp.bool_)
            for _ in range(min(MOBA_TOPK, nb)):
                m = jnp.max(gm, axis=0, keepdims=True)
                first = jnp.min(jnp.where(gm == m, fi, float(nb)), axis=0, keepdims=True)
                pick = fi == first
                sel = sel | pick
                gm = jnp.where(pick, -jnp.inf, gm)
            sel_scr[hh] = jnp.where(sel & valid, 1.0, 0.0)

        def blocks(ns, carry, mask_of):
            ks = [k_ref[pl.ds(pl.multiple_of(n * t, t), t), :] for n in ns]
            vs = [v_ref[n] for n in ns]
            z = [[_dot(kj, qm[hh]) for kj in ks] for hh in heads]
            out = []
            for c, hh in enumerate(heads):
                v_cat = jnp.concatenate([v[hh * HEAD_DIM:(hh + 1) * HEAD_DIM] for v in vs], axis=1)
                out += list(_online_update(carry[3 * c:3 * c + 3], z[c],
                                           [mask_of(hh, n) for n in ns], v_cat))
            return tuple(out)

        chosen = lambda hh, n: sel_scr[hh, pl.ds(n, 1), :] > 0.0
        init = (jnp.full((1, t), NEG, F32), jnp.zeros((1, t), F32),
                jnp.zeros((HEAD_DIM, t), F32)) * len(heads)
        carry = blocks([i], init, lambda hh, n: key_i <= qry_i)
        carry = lax.fori_loop(
            0, i // 3, lambda n, ca: blocks([3 * n, 3 * n + 1, 3 * n + 2], ca, chosen), carry)
        carry = lax.cond(i % 3 == 1, lambda ca: blocks([i - 1], ca, chosen), lambda ca: ca, carry)
        carry = lax.cond(i % 3 == 2, lambda ca: blocks([i - 2, i - 1], ca, chosen),
                         lambda ca: ca, carry)
        done = {hh: carry[3 * c + 2] / carry[3 * c + 1] for c, hh in enumerate(heads)}
        rows = [done.get(hh, jnp.zeros((HEAD_DIM, t), F32)) for hh in range(2)]
        o_ref[...] = jnp.concatenate(rows, axis=0).astype(o_ref.dtype)

    if (H_SB + H_DSA) % 2 == 1:
        pl.when(pl.program_id(1) == 0)(lambda: run((1,)))
        pl.when(pl.program_id(1) != 0)(lambda: run((0, 1)))
    else:
        run((0, 1))


def _moba_attention(proj_t, k_tok, kmean, *, n_b, seq):
    nq = seq // SLAB
    nb = seq // MOBA_BLOCK
    n_pairs = N_HEADS // 2 - MOBA_FIRST_PAIR
    p0 = MOBA_FIRST_PAIR
    v_blk0 = T_OFF_V // PAIR + p0
    return pl.pallas_call(
        functools.partial(_moba_kernel, nb=nb),
        grid=(n_b, n_pairs, nq),
        in_specs=[
            pl.BlockSpec((None, PAIR, SLAB), lambda b, p, i: (b * nq + i, p0 + p, 0)),
            pl.BlockSpec((seq, PAIR), lambda b, p, i: (b, p0 + p)),
            pl.BlockSpec((nq, PAIR, SLAB), lambda b, p, i: (b, v_blk0 + p, 0)),
            pl.BlockSpec((None, nb, PAIR), lambda b, p, i: (b, 0, p0 + p)),
        ],
        out_specs=pl.BlockSpec((None, PAIR, SLAB), lambda b, p, i: (b * nq + i, p, 0)),
        out_shape=jax.ShapeDtypeStruct((n_b * nq, n_pairs * PAIR, SLAB), BF16),
        scratch_shapes=[pltpu.VMEM((2, nb, SLAB), F32)],
        compiler_params=_params(("parallel", "parallel", "arbitrary")),
        name="moba_attention",
    )(proj_t, k_tok, proj_t, kmean)


ROWS_PER_HEAD = 8


def _top_mask(cols, n_top):
    out = []
    for n, g in enumerate(cols):
        rank = jnp.zeros_like(g)
        for m, o in enumerate(cols):
            if m == n:
                continue
            ahead = (o >= g) if m < n else (o > g)
            rank = rank + jnp.where(ahead, 1.0, 0.0)
        out.append(rank < float(n_top))
    return out


def _sample_mixers(s_scr, p_scr, ki_scr, qi_ref, wi_ref, *, past, n_keep, n_bits):
    lk_len = past + PAGE_SIZE
    r8 = ROWS_PER_HEAD
    kpos = lax.broadcasted_iota(jnp.int32, (1, lk_len), 1)

    def qpos(rows):
        return past + lax.broadcasted_iota(jnp.int32, (rows, 1), 0) % r8

    ra = H_SB * r8
    z = s_scr[:ra, :]
    before = kpos < qpos(ra)
    sp = _softplus2(z)
    drop = jnp.where(before, sp, 0.0)
    hi, lo = _split_bf16(drop)
    tri_r = lax.broadcasted_iota(jnp.int32, (2 * LANES, LANES), 0) % LANES
    tri_c = lax.broadcasted_iota(jnp.int32, (2 * LANES, LANES), 1)
    tri2 = (tri_r > tri_c).astype(BF16)
    c = jnp.zeros((ra, 1), F32)
    for blk in reversed(range(lk_len // LANES)):
        sl = slice(blk * LANES, (blk + 1) * LANES)
        later = _dot(jnp.concatenate([hi[:, sl], lo[:, sl]], axis=1), tri2) + c
        a = jnp.where(before[:, sl], jnp.exp2(z[:, sl] - sp[:, sl] - later), 0.0)
        p_scr[:ra, sl] = a.astype(BF16)
        c = c + jnp.sum(drop[:, sl], axis=1, keepdims=True)

    sc = jnp.maximum(_dot(qi_ref[0], ki_scr[...]), 0.0) * wi_ref[0]
    score = jnp.sum(sc.reshape(r8, IDX_HEADS, lk_len), axis=1)
    causal8 = kpos <= qpos(r8)
    key = _sort_key(jnp.where(causal8, score, -jnp.inf))

    def count(pred):
        return jnp.sum(jnp.where(pred, 1.0, 0.0), axis=1, keepdims=True)

    thr = _kth_largest(lambda cand: count(key >= cand), (r8, 1), float(n_keep), bits_per_step=2)
    need = float(n_keep) - count(key > thr)
    n_eq = count(key == thr)
    cut = lax.cond(
        jnp.max(n_eq - need) > 0.0,
        lambda: _tie_cut(lambda cand: count((key == thr) & (kpos < cand)), need, (r8, 1), n_bits),
        lambda: jnp.full((r8, 1), 2 ** n_bits, jnp.int32))
    sel8 = ((key > thr) | ((key == thr) & (kpos <= cut))) & causal8
    rb = H_DSA * r8
    sel = jnp.concatenate([jnp.where(sel8, 1.0, 0.0)] * H_DSA, axis=0) > 0.0
    s = jnp.where(sel, s_scr[ra:ra + rb, :], NEG)
    p = jnp.where(sel, jnp.exp2(s - jnp.max(s, axis=1, keepdims=True)), 0.0)
    l_b = jnp.sum(p, axis=1, keepdims=True)
    p_scr[ra:ra + rb, :] = p.astype(BF16)

    rc = H_MOBA * r8
    sm = s_scr[ra + rb:, :]
    n_blk = past // MOBA_BLOCK
    gate = [jnp.sum(sm[:, n * MOBA_BLOCK:(n + 1) * MOBA_BLOCK], axis=1, keepdims=True)
            for n in range(n_blk)]
    chosen = _top_mask(gate, min(MOBA_TOPK, n_blk + 1))
    own_ok = kpos[:, past:] <= qpos(rc)
    pieces = [jnp.where(chosen[n], sm[:, n * MOBA_BLOCK:(n + 1) * MOBA_BLOCK], NEG)
              for n in range(n_blk)]
    pieces.append(jnp.where(own_ok, sm[:, past:], NEG))
    m_c = pieces[0].max(axis=1, keepdims=True)
    for pc in pieces[1:]:
        m_c = jnp.maximum(m_c, pc.max(axis=1, keepdims=True))
    l_c = jnp.zeros((rc, 1), F32)
    off = 0
    for pc in pieces:
        e = jnp.where(pc > 0.5 * NEG, jnp.exp2(pc - m_c), 0.0)
        l_c = l_c + jnp.sum(e, axis=1, keepdims=True)
        p_scr[ra + rb:, off:off + pc.shape[1]] = e.astype(BF16)
        off += pc.shape[1]

    return jnp.concatenate([jnp.ones((ra, 1), F32), 1.0 / l_b, 1.0 / l_c], axis=0)


def _sample_kernel(pt_ref, q_ref, qi_ref, wi_ref, knew_ref, vnew_ref, kinew_ref, *rest,
                   n_pages, n_keep, n_bits):
    del pt_ref
    k_pages, v_pages = rest[:n_pages], rest[n_pages:2 * n_pages]
    i_pages = rest[2 * n_pages:3 * n_pages]
    o_ref = rest[3 * n_pages]
    k_scr, v_scr, ki_scr, s_scr, p_scr = rest[3 * n_pages + 1:]
    past = n_pages * PAGE_SIZE
    r8 = ROWS_PER_HEAD

    for pg in range(n_pages):
        cols = slice(pg * PAGE_SIZE, (pg + 1) * PAGE_SIZE)
        k_scr[:, :, cols] = k_pages[pg][...].astype(BF16)
        v_scr[:, :, cols] = v_pages[pg][...].astype(BF16)
        ki_scr[:, cols] = i_pages[pg][...].astype(BF16)
    k_scr[:, :, past:] = knew_ref[0]
    v_scr[:, :, past:] = vnew_ref[0]
    ki_scr[:, past:] = kinew_ref[0]
    for h in range(N_HEADS):
        s_scr[h * r8:(h + 1) * r8, :] = _dot(q_ref[0, h], k_scr[h])
    inv = _sample_mixers(s_scr, p_scr, ki_scr, qi_ref, wi_ref, past=past, n_keep=n_keep,
                         n_bits=n_bits)
    for h in range(N_HEADS):
        rows = slice(h * r8, (h + 1) * r8)
        o_ref[0, :, h * HEAD_DIM:(h + 1) * HEAD_DIM] = _nt(p_scr[rows, :], v_scr[h]) * inv[rows]


def _sample_attention(layer, page_table, q8, qi8, wi8, knew, vnew, kinew,
                      cache_k, cache_v, cache_ik, *, n_new):
    n_seq, n_pages = page_table.shape
    past = n_pages * PAGE_SIZE
    lk_len = past + PAGE_SIZE
    n_keep = min(DSA_TOPK, (past + n_new) // 4)
    n_bits = max(1, (lk_len - 1).bit_length())
    rows = N_HEADS * ROWS_PER_HEAD

    def page_spec(j, *dims):
        return pl.BlockSpec((None, None) + dims + (PAGE_SIZE,),
                            lambda b, pt: (layer, pt[b, j]) + (0,) * (len(dims) + 1))

    def seq_spec(*dims):
        return pl.BlockSpec((1,) + dims, lambda b, pt: (b,) + (0,) * len(dims))

    in_specs = [seq_spec(N_HEADS, ROWS_PER_HEAD, HEAD_DIM),
                seq_spec(ROWS_PER_HEAD * IDX_HEADS, IDX_DIM),
                seq_spec(ROWS_PER_HEAD * IDX_HEADS, 1),
                seq_spec(N_HEADS, HEAD_DIM, PAGE_SIZE), seq_spec(N_HEADS, HEAD_DIM, PAGE_SIZE),
                seq_spec(IDX_DIM, PAGE_SIZE)]
    in_specs += [page_spec(j, N_HEADS, HEAD_DIM) for j in range(n_pages)]
    in_specs += [page_spec(j, N_HEADS, HEAD_DIM) for j in range(n_pages)]
    in_specs += [page_spec(j, IDX_DIM) for j in range(n_pages)]
    grid_spec = pltpu.PrefetchScalarGridSpec(
        num_scalar_prefetch=1,
        grid=(n_seq,),
        in_specs=in_specs,
        out_specs=pl.BlockSpec((1, ROWS_PER_HEAD, QKV_W), lambda b, pt: (b, 0, 0)),
        scratch_shapes=[pltpu.VMEM((N_HEADS, HEAD_DIM, lk_len), BF16),
                        pltpu.VMEM((N_HEADS, HEAD_DIM, lk_len), BF16),
                        pltpu.VMEM((IDX_DIM, lk_len), BF16),
                        pltpu.VMEM((rows, lk_len), F32), pltpu.VMEM((rows, lk_len), BF16)],
    )
    out = pl.pallas_call(
        functools.partial(_sample_kernel, n_pages=n_pages, n_keep=n_keep, n_bits=n_bits),
        grid_spec=grid_spec,
        out_shape=jax.ShapeDtypeStruct((n_seq, ROWS_PER_HEAD, QKV_W), F32),
        compiler_params=_params(("arbitrary",)),
        name="sample_attention",
    )(page_table, q8, qi8, wi8, knew, vnew, kinew,
      *([cache_k] * n_pages), *([cache_v] * n_pages), *([cache_ik] * n_pages))
    return out[:, :n_new]


def _layer_norm(h, g, b):
    mu = jnp.mean(h, axis=-1, keepdims=True)
    d = h - mu
    var = jnp.mean(d * d, axis=-1, keepdims=True)
    return d * lax.rsqrt(var + LN_EPS) * g + b


def _merge_kernel(ya_ref, yb_ref, yc_ref, g_ref, bg_ref, wa_ref, wb_ref, wc_ref, wo_ref,
                  x_ref, lg_ref, lb_ref, o_ref, ob_ref):
    dm = x_ref.shape[1]
    merged = jnp.zeros(x_ref.shape, F32)
    for n, (y_ref, w_ref) in enumerate(((ya_ref, wa_ref), (yb_ref, wb_ref), (yc_ref, wc_ref))):
        sl = slice(n * dm, (n + 1) * dm)
        gate = jax.nn.sigmoid(g_ref[:, sl].astype(F32) + bg_ref[:, sl])
        merged = merged + gate * _dot(y_ref[...], w_ref[...])
    mix = _dot(merged.astype(BF16), wo_ref[...])
    y = _layer_norm(ALPHA * x_ref[...] + mix, lg_ref[...], lb_ref[...])
    o_ref[...] = y
    ob_ref[...] = y.astype(BF16)


def _merge(ya, yb, yc, g, bg, wa, wb, wc, wo, x, lg, lb, *, tm):
    n, dm = x.shape
    row = lambda w: pl.BlockSpec((tm, w), lambda i: (i, 0))
    full = lambda a: pl.BlockSpec(a.shape, lambda i: (0, 0))
    return pl.pallas_call(
        _merge_kernel,
        grid=(n // tm,),
        in_specs=[row(ya.shape[1]), row(yb.shape[1]), row(yc.shape[1]), row(g.shape[1]),
                  full(bg), full(wa), full(wb), full(wc), full(wo), row(dm), full(lg), full(lb)],
        out_specs=[row(dm), row(dm)],
        out_shape=[jax.ShapeDtypeStruct((n, dm), F32), jax.ShapeDtypeStruct((n, dm), BF16)],
        compiler_params=_params(("parallel",)),
        name="merge_out_ln",
    )(ya, yb, yc, g, bg, wa, wb, wc, wo, x, lg, lb)


def _route(x, xb, wr_ref, rb_ref):
    w_hi, w_lo = _split_bf16(wr_ref[...])
    x_lo = (x - xb.astype(F32)).astype(BF16)
    logits = _nt(w_hi, xb) + _nt(w_hi, x_lo) + _nt(w_lo, xb)
    s = jax.nn.sigmoid(logits)
    sb = s + rb_ref[...]
    s_col = [s[e:e + 1, :] for e in range(N_EXPERTS)]
    b_col = [sb[e:e + 1, :] for e in range(N_EXPERTS)]
    best_score = None
    for gi in range(N_GROUPS):
        a, b, c, d = b_col[gi * EXPERTS_PER_GROUP:(gi + 1) * EXPERTS_PER_GROUP]
        m1, n1 = jnp.maximum(a, b), jnp.minimum(a, b)
        m2, n2 = jnp.maximum(c, d), jnp.minimum(c, d)
        score = jnp.maximum(m1, m2) + jnp.maximum(jnp.minimum(m1, m2), jnp.maximum(n1, n2))
        if gi == 0:
            best_score, best = score, jnp.zeros_like(score)
        else:
            better = score > best_score
            best = jnp.where(better, float(gi), best)
            best_score = jnp.where(better, score, best_score)
    picked = []
    for gi in range(N_GROUPS):
        cols = b_col[gi * EXPERTS_PER_GROUP:(gi + 1) * EXPERTS_PER_GROUP]
        top = _top_mask(cols, 2)
        in_group = best == float(gi)
        for j in range(EXPERTS_PER_GROUP):
            picked.append(jnp.where(in_group & top[j], s_col[gi * EXPERTS_PER_GROUP + j], 0.0))
    total = picked[0]
    for c in picked[1:]:
        total = total + c
    return jnp.concatenate([p / total for p in picked], axis=0)


MOE_CHUNK = 128


def _moe_kernel(x_ref, xb_ref, wr_ref, rb_ref, tri_ref, wg_ref, wu_ref, wd_ref, lg_ref, lb_ref,
                o_ref, ob_ref, comb_t_ref, pos_t_ref, comb_c_ref, pos_c_ref, cnt_ref, acc_ref):
    e = pl.program_id(1)
    tm = x_ref.shape[0]

    @pl.when(e == 0)
    def _():
        comb_t = _route(x_ref[...], xb_ref[...], wr_ref, rb_ref)
        routed = jnp.where(comb_t > 0.0, 1.0, 0.0)
        pos_t = _dot(routed.astype(BF16), tri_ref[...])
        comb_t_ref[...] = comb_t
        pos_t_ref[...] = pos_t
        pad = jnp.zeros((LANES - N_EXPERTS, tm), F32)
        comb_c_ref[...] = jnp.concatenate([comb_t, pad], axis=0).T
        pos_c_ref[...] = jnp.concatenate([pos_t, pad], axis=0).T
        for ex in range(N_EXPERTS):
            cnt_ref[ex] = jnp.sum(routed[ex:ex + 1, :]).astype(jnp.int32)
        acc_ref[...] = jnp.zeros_like(acc_ref)

    xb = xb_ref[...]
    comb_row = comb_t_ref[pl.ds(e, 1), :]
    pos_row = pos_t_ref[pl.ds(e, 1), :]
    lane = lax.broadcasted_iota(jnp.int32, comb_c_ref.shape, 1)
    column = lambda ref: jnp.sum(jnp.where(lane == e, ref[...], 0.0), axis=1, keepdims=True)
    comb_col, pos_col = column(comb_c_ref), column(pos_c_ref)
    row_iota = lax.broadcasted_iota(jnp.int32, (MOE_CHUNK, 1), 0).astype(F32)
    lane_iota = lax.broadcasted_iota(jnp.int32, (1, MOE_CHUNK), 1).astype(F32)

    def chunk(c, _):
        base = (c * MOE_CHUNK).astype(F32)
        pick = jnp.where((pos_row == row_iota + base) & (comb_row > 0.0), 1.0, 0.0)
        weight = jnp.sum(pick * comb_row, axis=1, keepdims=True)
        xg = _dot(pick.astype(BF16), xb).astype(BF16)
        gate = _dot(xg, wg_ref[0])
        h = gate * jax.nn.sigmoid(gate) * _dot(xg, wu_ref[0]) * weight
        out = _dot(h.astype(BF16), wd_ref[0]).astype(BF16)
        back = jnp.where((pos_col == lane_iota + base) & (comb_col > 0.0), 1.0, 0.0).astype(BF16)
        acc_ref[...] += _dot(back, out)
        return 0

    lax.fori_loop(0, (cnt_ref[e] + MOE_CHUNK - 1) // MOE_CHUNK, chunk, 0)

    @pl.when(e == pl.num_programs(1) - 1)
    def _():
        y = _layer_norm(ALPHA * x_ref[...] + acc_ref[...], lg_ref[...], lb_ref[...])
        o_ref[...] = y
        ob_ref[...] = y.astype(BF16)


def _moe(x, xb, wr, rb, wg, wu, wd, lg, lb, *, tm):
    n, dm = x.shape
    n_e, _, de = wg.shape
    row = pl.BlockSpec((tm, dm), lambda i, e: (i, 0))
    full = lambda a: pl.BlockSpec(a.shape, lambda i, e: (0, 0))
    r = lax.broadcasted_iota(jnp.int32, (tm, tm), 0)
    c = lax.broadcasted_iota(jnp.int32, (tm, tm), 1)
    tri = (r < c).astype(BF16)
    return pl.pallas_call(
        _moe_kernel,
        grid=(n // tm, n_e),
        in_specs=[row, row, full(wr), full(rb), full(tri),
                  pl.BlockSpec((1, dm, de), lambda i, e: (e, 0, 0)),
                  pl.BlockSpec((1, dm, de), lambda i, e: (e, 0, 0)),
                  pl.BlockSpec((1, de, dm), lambda i, e: (e, 0, 0)),
                  full(lg), full(lb)],
        out_specs=[row, row],
        out_shape=[jax.ShapeDtypeStruct((n, dm), F32), jax.ShapeDtypeStruct((n, dm), BF16)],
        scratch_shapes=[pltpu.VMEM((n_e, tm), F32), pltpu.VMEM((n_e, tm), F32),
                        pltpu.VMEM((tm, LANES), F32), pltpu.VMEM((tm, LANES), F32),
                        pltpu.SMEM((n_e,), jnp.int32), pltpu.VMEM((tm, dm), F32)],
        compiler_params=_params(("parallel", "arbitrary")),
        name="moe_ln",
    )(x, xb, wr, rb, tri, wg, wu, wd, lg, lb)


def _pad_cols(a, width):
    return jnp.pad(a, ((0, 0), (0, width - a.shape[1])))


def _layer_weights(w_in_l):
    scale = HEAD_DIM ** -0.5 * LOG2E
    cuts = [QKV_W, 2 * QKV_W, 3 * QKV_W, 3 * QKV_W + QI_W, 3 * QKV_W + QI_W + IDX_DIM,
            3 * QKV_W + QI_W + IDX_DIM + IDX_HEADS]
    wq, wk, wv, wqi, wki, wwi, wg = jnp.split(w_in_l, cuts, axis=1)
    w_tok = _pad_cols(jnp.concatenate([wk, wki, wwi], axis=1), N_COLS)
    col = jnp.arange(N_COLS)
    flag_tok = ((col < QKV_W) & (col // HEAD_DIM >= H_SB)) | ((col >= N_OFF_KI) & (col < N_OFF_WI))
    w_feat = jnp.concatenate([wq * scale, wqi, wv], axis=1).T
    grp = jnp.arange(T_ROWS // HEAD_DIM)
    flag_feat = ((grp >= H_SB) & (grp < N_HEADS)) | ((grp >= T_OFF_QI // HEAD_DIM)
                                                     & (grp < T_OFF_V // HEAD_DIM))
    w_sq = jnp.concatenate([wq * scale, wqi, wv], axis=1)
    col_s = jnp.arange(T_ROWS)
    flag_sq = (col_s // HEAD_DIM >= H_SB) & (col_s < T_OFF_V)
    head = jnp.arange(N_HEADS)
    f = lambda a: a.astype(F32)[None, :]
    i32 = lambda a: a.astype(jnp.int32)
    return dict(w_tok=w_tok.astype(BF16), flag_tok=f(flag_tok),
                w_feat=w_feat.astype(BF16), flag_feat=i32(flag_feat),
                w_sq=w_sq.astype(BF16), flag_sq=f(flag_sq), w_gate=wg.astype(BF16),
                w_k=wk.T.astype(BF16), flag_k=i32(head >= H_SB),
                w_v=wv.T.astype(BF16), flag_v=i32(head < 0),
                w_ki=wki.T.astype(BF16), flag_ki=jnp.ones((1,), jnp.int32))


def _rope_tables(pos):
    half = HEAD_DIM // 2
    inv = ROPE_THETA ** (-jnp.arange(half, dtype=F32) / half)
    ang = pos.astype(F32)[:, None] * inv[None, :]
    cos, sin = jnp.cos(ang), jnp.sin(ang)
    return jnp.concatenate([cos, cos], axis=1), jnp.concatenate([-sin, sin], axis=1)


def kernel(x_prompt, x_sample, cache_k, cache_v, cache_idx_k, page_table, w_in, w_br_a, w_br_b,
           w_br_c, b_gate, w_out, ln1_g, ln1_b, ln2_g, ln2_b, w_router, router_bias,
           w_exp_gate, w_exp_up, w_exp_down):
    n_b, seq, dm = x_prompt.shape
    n_seq, n_new, _ = x_sample.shape
    depth = w_in.shape[0]
    n_p = n_b * seq
    n_s = n_seq * n_new
    n_pages = page_table.shape[1]
    past = n_pages * PAGE_SIZE
    assert seq % SLAB == 0 and past % MOBA_BLOCK == 0 and n_new <= ROWS_PER_HEAD
    ck_t = cache_k.transpose(0, 1, 3, 4, 2)
    cv_t = cache_v.transpose(0, 1, 3, 4, 2)
    cik_t = cache_idx_k.transpose(0, 1, 3, 2)

    pos = jnp.concatenate([jnp.tile(jnp.arange(seq, dtype=jnp.int32), n_b),
                           jnp.tile(past + jnp.arange(n_new, dtype=jnp.int32), n_seq)])
    cos64, sin64 = _rope_tables(pos)
    cos = jnp.concatenate([cos64, cos64], axis=1)
    sin = jnp.concatenate([sin64, sin64], axis=1)
    cos_t, sin_t = cos64[:n_p].T, sin64[:n_p].T
    x = jnp.concatenate([x_prompt.reshape(n_p, dm), x_sample.reshape(n_s, dm)], axis=0)
    xb = x.astype(BF16)
    proj_tm = max(m * TILE_TOKENS for m in (1, 2, 3) if (n_p + n_s) % (m * TILE_TOKENS) == 0)
    moe_tm = 3 * TILE_TOKENS // 2 if (n_p + n_s) % (3 * TILE_TOKENS // 2) == 0 else TILE_TOKENS

    pad_rows = lambda a, n: jnp.pad(a, ((0, 0), (0, n - a.shape[1])) + ((0, 0),) * (a.ndim - 2))
    zeros_b = jnp.zeros((HEAD_DIM, dm), BF16)
    rows_p, rows_s = [], []
    for l in range(depth):
        w = _layer_weights(w_in[l])
        tok_f32, tok_bf = _proj(xb, w["w_tok"], cos, sin, w["flag_tok"], rope=True,
                                tm=proj_tm, tn=TILE_COLS_TOK, name="in_proj_token_major")
        gates = _proj(xb, w["w_gate"], cos, sin, w["flag_tok"], rope=False,
                      tm=proj_tm, tn=TILE_COLS, name="in_proj_gates")
        xb_p = xb[:n_p]
        proj_t = _proj_t(w["flag_feat"], w["w_feat"], xb_p, cos_t, sin_t,
                         tm=2 * TILE_TOKENS, tn=TILE_ROWS_T)
        sq, _ = _proj(xb[n_p:], w["w_sq"], cos[n_p:], sin[n_p:], w["flag_sq"], rope=True,
                      tm=n_s, tn=TILE_ROWS_T, name="in_proj_sample_q")

        out_rows = lambda name, tn: _proj_t(w["flag_" + name], w["w_" + name], xb_p, cos_t, sin_t,
                                            tm=2 * TILE_TOKENS, tn=tn, seq=seq)
        heads_last = lambda a: a.reshape(n_b, N_HEADS, HEAD_DIM, seq).transpose(0, 3, 1, 2)
        rows_p.append((heads_last(out_rows("k", TILE_TOKENS)), heads_last(out_rows("v", TILE_TOKENS)),
                       out_rows("ki", IDX_DIM).transpose(0, 2, 1)))
        tok_s = tok_f32[n_p:].reshape(n_seq, n_new, -1)
        k_s, ki_s = tok_s[..., :QKV_W], tok_s[..., N_OFF_KI:N_OFF_WI]
        v_s = sq[:, T_OFF_V:].reshape(n_seq, n_new, QKV_W)
        rows_s.append((k_s.reshape(n_seq, n_new, N_HEADS, HEAD_DIM),
                       v_s.reshape(n_seq, n_new, N_HEADS, HEAD_DIM), ki_s))

        y_a = _sb_attention(proj_t, tok_bf, n_b=n_b, seq=seq)
        y_b = _dsa_attention(proj_t, tok_bf, tok_f32, n_b=n_b, seq=seq)
        kmean = _block_means(tok_f32, n_p).reshape(n_b, seq // MOBA_BLOCK, QKV_W)
        y_c = _moba_attention(proj_t, tok_bf, kmean, n_b=n_b, seq=seq)
        tok = lambda y: y.transpose(0, 2, 1).reshape(n_p, -1)

        q_s = sq[:, :QKV_W].reshape(n_seq, n_new, N_HEADS, HEAD_DIM)
        q8 = pad_rows(q_s, ROWS_PER_HEAD).transpose(0, 2, 1, 3).astype(BF16)
        qi8 = pad_rows(sq[:, T_OFF_QI:T_OFF_V].reshape(n_seq, n_new, QI_W), ROWS_PER_HEAD).reshape(
            n_seq, ROWS_PER_HEAD * IDX_HEADS, IDX_DIM).astype(BF16)
        wi8 = pad_rows(tok_s[..., N_OFF_WI:N_OFF_WI + IDX_HEADS], ROWS_PER_HEAD).reshape(
            n_seq, ROWS_PER_HEAD * IDX_HEADS, 1)
        new_heads = lambda a: pad_rows(a.reshape(n_seq, n_new, N_HEADS, HEAD_DIM),
                                       PAGE_SIZE).transpose(0, 2, 3, 1).astype(BF16)
        y_s = _sample_attention(
            l, page_table, q8, qi8, wi8, new_heads(k_s), new_heads(v_s),
            pad_rows(ki_s, PAGE_SIZE).transpose(0, 2, 1).astype(BF16),
            ck_t, cv_t, cik_t, n_new=n_new)
        y_s = y_s.reshape(n_s, QKV_W).astype(BF16)

        wa_cols, wb_cols = H_SB * HEAD_DIM, (H_SB + H_DSA) * HEAD_DIM
        blk = DSA_BLOCK_HEADS * HEAD_DIM
        ya = jnp.concatenate([tok(y_a), y_s[:, :wa_cols]], axis=0)
        yb = jnp.concatenate([tok(y_b), y_s[:, wa_cols:wa_cols + blk]], axis=0)
        yc = jnp.concatenate([tok(y_c), y_s[:, QKV_W - blk:]], axis=0)
        w_b = jnp.concatenate([w_br_b[l].astype(BF16), zeros_b], axis=0)
        w_c = jnp.concatenate([zeros_b, w_br_c[l].astype(BF16)], axis=0)
        del wb_cols
        x, xb = _merge(ya, yb, yc, gates, b_gate[l][None, :], w_br_a[l].astype(BF16), w_b, w_c,
                       w_out[l].astype(BF16), x, ln1_g[l][None, :], ln1_b[l][None, :],
                       tm=TILE_TOKENS)
        x, xb = _moe(x, xb, w_router.T, router_bias[:, None], w_exp_gate[l].astype(BF16),
                     w_exp_up[l].astype(BF16), w_exp_down[l].astype(BF16),
                     ln2_g[l][None, :], ln2_b[l][None, :], tm=moe_tm)

    stack = lambda rows, i: jnp.stack([r[i] for r in rows], 0)
    return (x[:n_p].reshape(n_b, seq, dm), x[n_p:].reshape(n_seq, n_new, dm),
            stack(rows_p, 0), stack(rows_p, 1), stack(rows_p, 2),
            stack(rows_s, 0), stack(rows_s, 1), stack(rows_s, 2))
```
